```python
import jax, jax.numpy as jnp
from jax import lax
import numpy as np

D_MODEL = 1024
BATCH = 8
SEQ = 4096
DEPTH = 2

HEAD_DIM = 64
GROUP_WIDTH = D_MODEL // 4
GROUP_HEADS = GROUP_WIDTH // HEAD_DIM
MIX_WIDTH = 4 * GROUP_WIDTH
ROPE_THETA = 10000.0
Q_BLOCK = 128

MOBA_BLOCK = 256
MOBA_TOPK = 3
RET_CHUNK = 128
RG_CONV = 4
RG_C = 8.0
RG_BLOCKS = GROUP_HEADS
RG_BLOCK_W = GROUP_WIDTH // RG_BLOCKS
IDX_HEADS = 8
IDX_DIM = HEAD_DIM
IDX_SCALE = (IDX_HEADS ** -0.5) * (IDX_DIM ** -0.5)
DSA_TOPK = 256

N_EXPERTS = 32
TOP_K = 4
D_FF = D_MODEL
SWIGLU_LIMIT = 7.0
SWIGLU_ALPHA = 1.702

ALPHA = (2 * DEPTH) ** 0.25
BETA = (8 * DEPTH) ** -0.25
LN_EPS = 1e-5

IN_SIZES = (GROUP_WIDTH,) * 3 + (GROUP_WIDTH,) * 4 + (GROUP_WIDTH,) * 2 + (GROUP_WIDTH,) * 3 + (IDX_HEADS * IDX_DIM, IDX_DIM, IDX_HEADS)
IN_WIDTH = 12 * GROUP_WIDTH + IDX_HEADS * IDX_DIM + IDX_DIM + IDX_HEADS

kernel_name = 'hybrid_moba_retnet_rglru_dsa_moe'

F32 = jnp.float32


def _layer_norm(x, g, b):
    xf = x.astype(F32)
    mu = jnp.mean(xf, axis=-1, keepdims=True)
    var = jnp.mean(jnp.square(xf - mu), axis=-1, keepdims=True)
    return ((xf - mu) * lax.rsqrt(var + LN_EPS) * g.astype(F32) + b.astype(F32)).astype(x.dtype)


def _rope_tables(T):
    inv = ROPE_THETA ** (-jnp.arange(0, HEAD_DIM, 2, dtype=F32) / HEAD_DIM)
    ang = jnp.arange(T, dtype=F32)[:, None] * inv[None, :]
    return jnp.cos(ang), jnp.sin(ang)


def _rope(x, cos, sin):
    xf = x.astype(F32)
    x1, x2 = jnp.split(xf, 2, axis=-1)
    c = cos[None, :, None, :]
    s = sin[None, :, None, :]
    return jnp.concatenate([x1 * c - x2 * s, x2 * c + x1 * s], axis=-1).astype(x.dtype)


def _moba_attention(q, k, v):
    B, T, H, d = q.shape
    nB = -(-T // MOBA_BLOCK)
    pad = nB * MOBA_BLOCK - T
    kp = jnp.pad(k, ((0, 0), (0, pad), (0, 0), (0, 0)))
    vp = jnp.pad(v, ((0, 0), (0, pad), (0, 0), (0, 0)))
    kb = kp.reshape(B, nB, MOBA_BLOCK, H, d).transpose(0, 3, 1, 2, 4)
    vb = vp.reshape(B, nB, MOBA_BLOCK, H, d).transpose(0, 3, 1, 2, 4)
    kmean = jnp.mean(kb.astype(F32), axis=3).astype(q.dtype)
    kt = min(MOBA_TOPK, nB)
    nQ = T // Q_BLOCK
    qc = q.reshape(B, nQ, Q_BLOCK, H, d)
    scale = d ** -0.5
    h_idx = jnp.arange(H)[None, :, None]

    def per_seq(args):
        qs, kps, vps, kbs, vbs, kms = args

        def per_chunk(cargs):
            qq, j = cargs
            start = j * Q_BLOCK
            own = start // MOBA_BLOCK
            pos_q = start + jnp.arange(Q_BLOCK)
            gate = jnp.einsum('qhd,hnd->qhn', qq, kms).astype(F32)
            gate = jnp.where(jnp.arange(nB) < own, gate, -jnp.inf)
            _, sel = lax.top_k(gate, kt)
            valid = sel < own
            kg = kbs[h_idx, sel]
            vg = vbs[h_idx, sel]
            s_sel = jnp.einsum('qhd,qhnkd->qhnk', qq, kg).astype(F32) * scale
            s_sel = jnp.where(valid[..., None], s_sel, -jnp.inf).reshape(Q_BLOCK, H, kt * MOBA_BLOCK)
            k_own = lax.dynamic_slice_in_dim(kps, own * MOBA_BLOCK, MOBA_BLOCK, axis=0)
            v_own = lax.dynamic_slice_in_dim(vps, own * MOBA_BLOCK, MOBA_BLOCK, axis=0)
            pos_k = own * MOBA_BLOCK + jnp.arange(MOBA_BLOCK)
            s_own = jnp.einsum('qhd,khd->qhk', qq, k_own).astype(F32) * scale
            s_own = jnp.where((pos_k[None, :] <= pos_q[:, None])[:, None, :], s_own, -jnp.inf)
            p = jax.nn.softmax(jnp.concatenate([s_sel, s_own], axis=-1), axis=-1).astype(vg.dtype)
            p_sel = p[..., :kt * MOBA_BLOCK].reshape(Q_BLOCK, H, kt, MOBA_BLOCK)
            p_own = p[..., kt * MOBA_BLOCK:]
            return jnp.einsum('qhnk,qhnkd->qhd', p_sel, vg) + jnp.einsum('qhk,khd->qhd', p_own, v_own)

        return lax.map(per_chunk, (qs, jnp.arange(nQ)))

    out = lax.map(per_seq, (qc, kp, vp, kb, vb, kmean))
    return out.reshape(B, T, H, d)


def _retention(q, k, v):
    B, T, H, d = q.shape
    dv = v.shape[-1]
    C = RET_CHUNK
    N = T // C
    log_g = jnp.log(1.0 - 2.0 ** (-5.0 - jnp.arange(H, dtype=F32)))
    n = jnp.arange(C, dtype=F32)
    diff = n[:, None] - n[None, :]
    d_mask = jnp.where(diff >= 0, jnp.exp(log_g[:, None, None] * jnp.maximum(diff, 0.0)), 0.0)
    xi = jnp.exp(log_g[:, None] * (n + 1.0))[None, :, :, None]
    zeta = jnp.exp(log_g[:, None] * (C - 1.0 - n))[None, :, :, None]
    g_chunk = jnp.exp(log_g * C)[None, :, None, None]
    to_chunks = lambda t: t.reshape(B, N, C, H, t.shape[-1]).transpose(1, 0, 3, 2, 4)

    def step(R, inp):
        qc, kc, vc = inp
        inner = jnp.einsum('bhnd,bhmd->bhnm', qc, kc) * d_mask
        o = jnp.einsum('bhnm,bhme->bhne', inner, vc) + jnp.einsum('bhnd,bhde->bhne', qc, R) * xi
        R = jnp.einsum('bhmd,bhme->bhde', kc * zeta, vc) + g_chunk * R
        return R, o

    R0 = jnp.zeros((B, H, d, dv), F32)
    _, o = lax.scan(step, R0, (to_chunks(q), to_chunks(k), to_chunks(v)))
    return o.transpose(1, 0, 3, 2, 4).reshape(B, T, H, dv)


def _rg_lru(xr, xg, conv_w, conv_b, wx, bx, wa, ba, lam):
    B, T, C = xr.shape
    xc = lax.conv_general_dilated(xr, conv_w[:, None, :], window_strides=(1,), padding=[(RG_CONV - 1, 0)],
                                  dimension_numbers=('NWC', 'WIO', 'NWC'), feature_group_count=C) + conv_b
    xb = xc.reshape(B, T, RG_BLOCKS, RG_BLOCK_W)
    gate_x = jax.nn.sigmoid((jnp.einsum('btnc,ncd->btnd', xb, wx).reshape(B, T, C) + bx).astype(F32))
    gate_a = jax.nn.sigmoid((jnp.einsum('btnc,ncd->btnd', xb, wa).reshape(B, T, C) + ba).astype(F32))
    log_a = -RG_C * gate_a * jax.nn.softplus(-lam.astype(F32))
    a = jnp.exp(log_a)
    b = jnp.sqrt(-jnp.expm1(2.0 * log_a)) * (gate_x * xc.astype(F32))

    def combine(left, right):
        a_l, b_l = left
        a_r, b_r = right
        return a_l * a_r, a_r * b_l + b_r

    _, h = lax.associative_scan(combine, (a, b), axis=1)
    return (h * jax.nn.gelu(xg.astype(F32))).astype(xr.dtype)


def _dsa_attention(q, k, v, qi, ki, wi):
    B, T, H, d = q.shape
    n_sel = min(DSA_TOPK, T // 4)
    nQ = T // Q_BLOCK
    scale = d ** -0.5
    qc = q.reshape(B, nQ, Q_BLOCK, H, d)
    qic = qi.reshape(B, nQ, Q_BLOCK, IDX_HEADS, IDX_DIM)
    wic = wi.reshape(B, nQ, Q_BLOCK, IDX_HEADS)
    pos_s = jnp.arange(T)

    def per_seq(args):
        qs, qis, wis, ks, vs, kis = args

        def per_chunk(cargs):
            qq, qiq, wq, j = cargs
            pos_q = j * Q_BLOCK + jnp.arange(Q_BLOCK)
            rel = jax.nn.relu(jnp.einsum('qhd,sd->qhs', qiq, kis).astype(F32))
            score = jnp.einsum('qh,qhs->qs', wq.astype(F32), rel) * IDX_SCALE
            score = jnp.where(pos_s[None, :] <= pos_q[:, None], score, -jnp.inf)
            _, idx = lax.top_k(score, n_sel)
            valid = idx <= pos_q[:, None]
            kg = ks[idx]
            vg = vs[idx]
            s = jnp.einsum('qhd,qkhd->qhk', qq, kg).astype(F32) * scale
            s = jnp.where(valid[:, None, :], s, -jnp.inf)
            p = jax.nn.softmax(s, axis=-1).astype(vg.dtype)
            return jnp.einsum('qhk,qkhd->qhd', p, vg)

        return lax.map(per_chunk, (qs, qis, wis, jnp.arange(nQ)))

    out = lax.map(per_seq, (qc, qic, wic, k, v, ki))
    return out.reshape(B, T, H, d)


def _moe(x, router_w, router_b, w1, b1, w2, b2):
    B, T, D = x.shape
    xf = x.reshape(B * T, D)
    logits = jnp.einsum('nd,de->ne', xf, router_w).astype(F32) + router_b.astype(F32)
    top_v, top_i = lax.top_k(logits, TOP_K)
    gates = jax.nn.softmax(top_v, axis=-1)
    comb = jnp.sum(jax.nn.one_hot(top_i, N_EXPERTS, dtype=F32) * gates[..., None], axis=1)

    def expert_step(acc, e):
        w1e, b1e, w2e, b2e, ce = e
        h = xf @ w1e + b1e
        glu_in = jnp.minimum(h[:, :D_FF], SWIGLU_LIMIT)
        up = jnp.clip(h[:, D_FF:], -SWIGLU_LIMIT, SWIGLU_LIMIT)
        glu = glu_in * jax.nn.sigmoid(SWIGLU_ALPHA * glu_in)
        y = ((up + 1.0) * glu) @ w2e + b2e
        return acc + ce[:, None] * y.astype(F32), None

    acc, _ = lax.scan(expert_step, jnp.zeros((B * T, D), F32), (w1, b1, w2, b2, comb.T))
    return acc.astype(x.dtype).reshape(B, T, D)


def _hybrid_layer(x, cos, sin, w_in, ret_gn_g, ret_gn_b, conv_w, conv_b, rg_wx, rg_bx, rg_wa, rg_ba,
                  rg_lambda, w_out, ln1_g, ln1_b, router_w, router_b, exp_w1, exp_b1, exp_w2, exp_b2,
                  ln2_g, ln2_b):
    B, T, _ = x.shape
    proj = jnp.einsum('btd,dp->btp', x, w_in)
    (a_q, a_k, a_v, r_q, r_k, r_v, r_g, c_x, c_g, d_q, d_k, d_v, d_qi, d_ki, d_w) = jnp.split(
        proj, np.cumsum(IN_SIZES)[:-1].tolist(), axis=-1)
    heads = lambda t: t.reshape(B, T, GROUP_HEADS, HEAD_DIM)

    o_a = _moba_attention(_rope(heads(a_q), cos, sin), _rope(heads(a_k), cos, sin), heads(a_v))

    rq = _rope(heads(r_q).astype(F32), cos, sin)
    rk = _rope(heads(r_k).astype(F32), cos, sin) * (HEAD_DIM ** -0.5)
    o_r = _retention(rq, rk, heads(r_v).astype(F32))
    o_r = _layer_norm(o_r, ret_gn_g.reshape(GROUP_HEADS, HEAD_DIM), ret_gn_b.reshape(GROUP_HEADS, HEAD_DIM))
    o_r = (o_r.reshape(B, T, GROUP_WIDTH) * jax.nn.silu(r_g.astype(F32))).astype(x.dtype)

    o_c = _rg_lru(c_x, c_g, conv_w, conv_b, rg_wx, rg_bx, rg_wa, rg_ba, rg_lambda)

    qi = _rope(d_qi.reshape(B, T, IDX_HEADS, IDX_DIM), cos, sin)
    ki = _rope(d_ki.reshape(B, T, 1, IDX_DIM), cos, sin)[:, :, 0]
    o_d = _dsa_attention(_rope(heads(d_q), cos, sin), _rope(heads(d_k), cos, sin), heads(d_v), qi, ki, d_w)

    mixed = jnp.concatenate([o_a.reshape(B, T, GROUP_WIDTH), o_r, o_c, o_d.reshape(B, T, GROUP_WIDTH)], axis=-1)
    x = _layer_norm(ALPHA * x + jnp.einsum('btm,md->btd', mixed, w_out), ln1_g, ln1_b)
    x = _layer_norm(ALPHA * x + _moe(x, router_w, router_b, exp_w1, exp_b1, exp_w2, exp_b2), ln2_g, ln2_b)
    return x


def setup_inputs(seed: int = 0) -> dict:
    key = jax.random.key(seed)
    ks = jax.random.split(key, 24)
    nrm = lambda k, shape, s: jax.random.normal(k, shape, F32) * s
    u = jax.random.uniform(ks[9], (DEPTH, GROUP_WIDTH), F32, 0.9, 0.999)
    a0 = u ** (1.0 / RG_C)
    return {
        'x': jax.random.normal(ks[0], (BATCH, SEQ, D_MODEL), F32),
        'w_in': nrm(ks[1], (DEPTH, D_MODEL, IN_WIDTH), D_MODEL ** -0.5),
        'ret_gn_g': 1.0 + nrm(ks[2], (DEPTH, GROUP_WIDTH), 0.02),
        'ret_gn_b': nrm(ks[3], (DEPTH, GROUP_WIDTH), 0.02),
        'conv_w': nrm(ks[4], (DEPTH, RG_CONV, GROUP_WIDTH), RG_CONV ** -0.5),
        'conv_b': nrm(ks[5], (DEPTH, GROUP_WIDTH), 0.02),
        'rg_wx': nrm(ks[6], (DEPTH, RG_BLOCKS, RG_BLOCK_W, RG_BLOCK_W), RG_BLOCK_W ** -0.5),
        'rg_bx': nrm(ks[7], (DEPTH, GROUP_WIDTH), 0.02),
        'rg_wa': nrm(ks[8], (DEPTH, RG_BLOCKS, RG_BLOCK_W, RG_BLOCK_W), RG_BLOCK_W ** -0.5),
        'rg_ba': nrm(ks[10], (DEPTH, GROUP_WIDTH), 0.02),
        'rg_lambda': jnp.log(a0) - jnp.log1p(-a0),
        'w_out': nrm(ks[11], (DEPTH, MIX_WIDTH, D_MODEL), BETA * MIX_WIDTH ** -0.5),
        'ln1_g': 1.0 + nrm(ks[12], (DEPTH, D_MODEL), 0.02),
        'ln1_b': nrm(ks[13], (DEPTH, D_MODEL), 0.02),
        'router_w': nrm(ks[14], (DEPTH, D_MODEL, N_EXPERTS), D_MODEL ** -0.5),
        'router_b': nrm(ks[15], (DEPTH, N_EXPERTS), 0.01),
        'exp_w1': nrm(ks[16], (DEPTH, N_EXPERTS, D_MODEL, 2 * D_FF), D_MODEL ** -0.5),
        'exp_b1': nrm(ks[17], (DEPTH, N_EXPERTS, 2 * D_FF), 0.02),
        'exp_w2': nrm(ks[18], (DEPTH, N_EXPERTS, D_FF, D_MODEL), BETA * D_FF ** -0.5),
        'exp_b2': nrm(ks[19], (DEPTH, N_EXPERTS, D_MODEL), 0.02),
        'ln2_g': 1.0 + nrm(ks[20], (DEPTH, D_MODEL), 0.02),
        'ln2_b': nrm(ks[21], (DEPTH, D_MODEL), 0.02),
    }


def reference(x, w_in, ret_gn_g, ret_gn_b, conv_w, conv_b, rg_wx, rg_bx, rg_wa, rg_ba, rg_lambda, w_out,
              ln1_g, ln1_b, router_w, router_b, exp_w1, exp_b1, exp_w2, exp_b2, ln2_g, ln2_b):
    cos, sin = _rope_tables(x.shape[1])
    for l in range(DEPTH):
        x = _hybrid_layer(x, cos, sin, w_in[l], ret_gn_g[l], ret_gn_b[l], conv_w[l], conv_b[l], rg_wx[l],
                          rg_bx[l], rg_wa[l], rg_ba[l], rg_lambda[l], w_out[l], ln1_g[l], ln1_b[l],
                          router_w[l], router_b[l], exp_w1[l], exp_b1[l], exp_w2[l], exp_b2[l],
                          ln2_g[l], ln2_b[l])
    return x
```

```python
import functools

import numpy as np
import jax
import jax.numpy as jnp
from jax import lax
from jax.experimental import pallas as pl
from jax.experimental.pallas import tpu as pltpu

F32 = jnp.float32
BF16 = jnp.bfloat16
I32 = jnp.int32

D_MODEL = 1024
DEPTH = 2
HEAD_DIM = 64
GROUP_WIDTH = 256
GROUP_HEADS = 4
ROPE_THETA = 10000.0
Q_BLOCK = 128
MOBA_BLOCK = 256
MOBA_TOPK = 3
MOBA_MAX_BLOCKS = 16
RET_CHUNK = 128
RG_CONV = 4
RG_C = 8.0
IDX_HEADS = 8
IDX_DIM = 64
IDX_SCALE = (IDX_HEADS ** -0.5) * (IDX_DIM ** -0.5)
DSA_TOPK = 256
DSA_KEY_BLOCK = 512
N_EXPERTS = 32
TOP_K = 4
D_FF = 1024
SWIGLU_LIMIT = 7.0
SWIGLU_ALPHA = 1.702
ALPHA = (2 * DEPTH) ** 0.25
LN_EPS = 1e-5
IN_WIDTH = 12 * GROUP_WIDTH + IDX_HEADS * IDX_DIM + IDX_DIM + IDX_HEADS
IN_PAD = 15 * GROUP_WIDTH

NEG = -1e30
BIG = 1e30
VMEM_LIMIT = 56 * 1024 * 1024

PROJ_TM = 256
OUT_TM = 512
RG_TC = 256
MOE_TM = 256
CMB_TM = 256


def _cparams(ndims):
    return pltpu.CompilerParams(dimension_semantics=("arbitrary",) * ndims,
                                vmem_limit_bytes=VMEM_LIMIT)


def _dot(a, b, precision=None):
    return jnp.dot(a, b, preferred_element_type=F32, precision=precision)


def _dot_nt(a, b, precision=None):
    return lax.dot_general(a, b, (((1,), (1,)), ((), ())), preferred_element_type=F32,
                           precision=precision)


def _dot_tn(a, b):
    return lax.dot_general(a, b, (((0,), (0,)), ((), ())), preferred_element_type=F32)


def _head_stack(q):
    head = lax.shift_right_logical(lax.broadcasted_iota(I32, q.shape, 1), 6)
    qf = q.astype(F32)
    return jnp.concatenate([jnp.where(head == h, qf, 0.0) for h in range(GROUP_HEADS)],
                           axis=0).astype(q.dtype)


def _head_unstack(s, rows):
    head = lax.shift_right_logical(lax.broadcasted_iota(I32, (rows, GROUP_WIDTH), 1), 6)
    out = jnp.zeros((rows, GROUP_WIDTH), F32)
    for h in range(GROUP_HEADS):
        out = out + jnp.where(head == h, s[h * rows:(h + 1) * rows], 0.0)
    return out


def _layer_norm_rows(y, g, b):
    mu = jnp.mean(y, axis=-1, keepdims=True)
    yc = y - mu
    var = jnp.mean(yc * yc, axis=-1, keepdims=True)
    return yc * lax.rsqrt(var + LN_EPS) * g + b


def _proj_kernel(x_ref, w_ref, cos_ref, sin_ref,
                 aq, ak, av, rq, rk, rv, rg, cx, cg, dq, dk, dv, dqi, dki, dw):
    xb = x_ref[...].astype(BF16)
    cos = cos_ref[...]
    sin = sin_ref[...]
    first_half = (lax.broadcasted_iota(I32, cos.shape, 1) & (HEAD_DIM - 1)) < (HEAD_DIM // 2)

    def seg(i):
        return _dot(xb, w_ref[:, i * GROUP_WIDTH:(i + 1) * GROUP_WIDTH])

    def rope(p):
        rot = jnp.where(first_half, pltpu.roll(p, GROUP_WIDTH - HEAD_DIM // 2, 1),
                        pltpu.roll(p, HEAD_DIM // 2, 1))
        return p * cos + rot * sin

    aq[...] = rope(seg(0)).astype(BF16)
    ak[...] = rope(seg(1)).astype(BF16)
    av[...] = seg(2).astype(BF16)
    rq[...] = rope(seg(3)).astype(BF16)
    rk[...] = (rope(seg(4)) * (HEAD_DIM ** -0.5)).astype(BF16)
    rv[...] = seg(5).astype(BF16)
    rg[...] = seg(6)
    cx[...] = seg(7)
    cg[...] = seg(8)
    dq[...] = rope(seg(9)).astype(BF16)
    dk[...] = rope(seg(10)).astype(BF16)
    dv[...] = seg(11).astype(BF16)
    dqi[:, 0:GROUP_WIDTH] = rope(seg(12)).astype(BF16)
    dqi[:, GROUP_WIDTH:2 * GROUP_WIDTH] = rope(seg(13)).astype(BF16)
    last = seg(14)
    dki[...] = rope(last)[:, 0:IDX_DIM].astype(BF16)
    dw[...] = last[:, 128:128 + IDX_HEADS]


def _proj(x2, w_pad, cos_t, sin_t, T):
    N = x2.shape[0]
    tm = PROJ_TM
    tpb = T // tm
    row = lambda w: pl.BlockSpec((tm, w), lambda i: (i, 0))
    tab = pl.BlockSpec((tm, GROUP_WIDTH), lambda i: (i % tpb, 0))
    widths = [256] * 12 + [512, IDX_DIM, IDX_HEADS]
    dtypes = [BF16, BF16, BF16, BF16, BF16, BF16, F32, F32, F32, BF16, BF16, BF16, BF16, BF16, F32]
    return pl.pallas_call(
        _proj_kernel,
        grid=(N // tm,),
        in_specs=[row(D_MODEL), pl.BlockSpec((D_MODEL, IN_PAD), lambda i: (0, 0)), tab, tab],
        out_specs=[row(w) for w in widths],
        out_shape=[jax.ShapeDtypeStruct((N, w), d) for w, d in zip(widths, dtypes)],
        compiler_params=_cparams(1),
        name="proj_rope",
    )(x2, w_pad, cos_t, sin_t)


def _moba_kernel(q_ref, k_ref, v_ref, o_ref, kmean_ref, m_ref, l_ref, acc_ref, *, n_blocks):
    j = pl.program_id(1)
    R = Q_BLOCK
    SR = GROUP_HEADS * R

    @pl.when(j == 0)
    def _():
        kmean_ref[...] = jnp.zeros_like(kmean_ref)
        for n in range(n_blocks):
            kb = k_ref[0, n * MOBA_BLOCK:(n + 1) * MOBA_BLOCK, :].astype(F32)
            kmean_ref[n:n + 1, :] = jnp.mean(kb, axis=0, keepdims=True)

    own = j // (MOBA_BLOCK // Q_BLOCK)
    q_stack = _head_stack(q_ref[0])

    gate = _dot_nt(q_stack.astype(F32), kmean_ref[...], precision=lax.Precision.HIGHEST)
    col = lax.broadcasted_iota(I32, gate.shape, 1)
    past = col < own
    g = jnp.where(past, gate, -jnp.inf)
    sel = jnp.zeros(gate.shape, F32)
    for _ in range(MOBA_TOPK):
        mx = jnp.max(g, axis=1, keepdims=True)
        first = jnp.min(jnp.where(g == mx, col, MOBA_MAX_BLOCKS), axis=1, keepdims=True)
        pick = col == first
        sel = jnp.where(pick & past, 1.0, sel)
        g = jnp.where(pick, -jnp.inf, g)
    sel_b = sel.astype(BF16)

    scale = HEAD_DIM ** -0.5
    m_ref[...] = jnp.full(m_ref.shape, NEG, F32)
    l_ref[...] = jnp.zeros(l_ref.shape, F32)
    acc_ref[...] = jnp.zeros(acc_ref.shape, F32)

    def attend(kb, vb, mask):
        s = jnp.where(mask, _dot_nt(q_stack, kb) * scale, NEG)
        m_old = m_ref[...]
        m_new = jnp.maximum(m_old, jnp.max(s, axis=1, keepdims=True))
        alpha = jnp.exp(m_old - m_new)
        p = jnp.exp(s - m_new)
        l_ref[...] = alpha * l_ref[...] + jnp.sum(p, axis=1, keepdims=True)
        acc_ref[...] = alpha * acc_ref[...] + _dot(p.astype(BF16), vb)
        m_ref[...] = m_new

    rowq = lax.broadcasted_iota(I32, (SR, MOBA_BLOCK), 0) & (R - 1)
    colk = lax.broadcasted_iota(I32, (SR, MOBA_BLOCK), 1)
    st = pl.multiple_of(own * MOBA_BLOCK, MOBA_BLOCK)
    attend(k_ref[0, pl.ds(st, MOBA_BLOCK), :], v_ref[0, pl.ds(st, MOBA_BLOCK), :],
           own * MOBA_BLOCK + colk <= j * R + rowq)

    def body(n, c):
        s0 = pl.multiple_of(n * MOBA_BLOCK, MOBA_BLOCK)
        onehot = (lax.broadcasted_iota(I32, (MOBA_MAX_BLOCKS, MOBA_BLOCK), 0) == n).astype(BF16)
        mask = _dot(sel_b, onehot) > 0.5
        attend(k_ref[0, pl.ds(s0, MOBA_BLOCK), :], v_ref[0, pl.ds(s0, MOBA_BLOCK), :], mask)
        return c

    lax.fori_loop(0, own, body, 0)
    o_ref[0] = _head_unstack(acc_ref[...] / l_ref[...], R).astype(o_ref.dtype)


def _moba(q, k, v):
    B, T, _ = q.shape
    n_blocks = T // MOBA_BLOCK
    assert T % MOBA_BLOCK == 0 and n_blocks <= MOBA_MAX_BLOCKS
    SR = GROUP_HEADS * Q_BLOCK
    return pl.pallas_call(
        functools.partial(_moba_kernel, n_blocks=n_blocks),
        grid=(B, T // Q_BLOCK),
        in_specs=[pl.BlockSpec((1, Q_BLOCK, GROUP_WIDTH), lambda b, j: (b, j, 0)),
                  pl.BlockSpec((1, T, GROUP_WIDTH), lambda b, j: (b, 0, 0)),
                  pl.BlockSpec((1, T, GROUP_WIDTH), lambda b, j: (b, 0, 0))],
        out_specs=pl.BlockSpec((1, Q_BLOCK, GROUP_WIDTH), lambda b, j: (b, j, 0)),
        out_shape=jax.ShapeDtypeStruct((B, T, GROUP_WIDTH), BF16),
        scratch_shapes=[pltpu.VMEM((MOBA_MAX_BLOCKS, GROUP_WIDTH), F32),
                        pltpu.VMEM((SR, 1), F32), pltpu.VMEM((SR, 1), F32),
                        pltpu.VMEM((SR, GROUP_WIDTH), F32)],
        compiler_params=_cparams(2),
        name="moba_attention",
    )(q, k, v)


def _ret_kernel(q_ref, k_ref, v_ref, g_ref, dmask_ref, xi_ref, zeta_ref, gdec_ref, bd_ref, avg_ref,
                gng_ref, gnb_ref, o_ref, r_ref):
    j = pl.program_id(1)

    @pl.when(j == 0)
    def _():
        r_ref[...] = jnp.zeros_like(r_ref)

    C = RET_CHUNK
    q = q_ref[0]
    k = k_ref[0]
    v = v_ref[0]
    q_stack = _head_stack(q)
    inner = _dot_nt(q_stack, k) * dmask_ref[...]
    o = _head_unstack(_dot(inner.astype(BF16), v), C)
    R = r_ref[...]
    o = o + _dot(q, R.astype(BF16)) * xi_ref[...]
    kz = (k.astype(F32) * zeta_ref[...]).astype(BF16)
    r_ref[...] = gdec_ref[...] * R + bd_ref[...] * _dot_tn(kz, v)

    hp = lax.Precision.HIGHEST
    mu = _dot(o, avg_ref[...], precision=hp)
    oc = o - mu
    var = _dot(oc * oc, avg_ref[...], precision=hp)
    y = oc * lax.rsqrt(var + LN_EPS) * gng_ref[...] + gnb_ref[...]
    gte = g_ref[0]
    o_ref[0] = (y * (gte * jax.nn.sigmoid(gte))).astype(o_ref.dtype)


def _ret_tables():
    H, C, d = GROUP_HEADS, RET_CHUNK, HEAD_DIM
    log_g = np.log(1.0 - 2.0 ** (-5.0 - np.arange(H, dtype=np.float64)))
    n = np.arange(C, dtype=np.float64)
    diff = n[:, None] - n[None, :]
    dmask = np.where(diff >= 0, np.exp(log_g[:, None, None] * np.maximum(diff, 0.0)), 0.0)
    xi = np.exp(log_g[:, None] * (n + 1.0))
    zeta = np.exp(log_g[:, None] * (C - 1.0 - n))
    g_chunk = np.exp(log_g * C)
    head = np.arange(GROUP_WIDTH) // d
    bd = (head[:, None] == head[None, :]).astype(np.float64)
    to32 = lambda a: jnp.asarray(a, dtype=F32)
    return dict(dmask=to32(dmask.reshape(H * C, C)), xi=to32(xi.T[:, head]), zeta=to32(zeta.T[:, head]),
                gdec=to32(bd * g_chunk[head][:, None]), bd=to32(bd), avg=to32(bd / d))


def _retention(rq, rk, rv, rg, gn_g, gn_b, tabs):
    B, T, _ = rq.shape
    C = RET_CHUNK
    blk = pl.BlockSpec((1, C, GROUP_WIDTH), lambda b, j: (b, j, 0))
    const = lambda a: pl.BlockSpec(a.shape, lambda b, j: (0,) * a.ndim)
    consts = [tabs["dmask"], tabs["xi"], tabs["zeta"], tabs["gdec"], tabs["bd"], tabs["avg"], gn_g, gn_b]
    return pl.pallas_call(
        _ret_kernel,
        grid=(B, T // C),
        in_specs=[blk, blk, blk, blk] + [const(a) for a in consts],
        out_specs=blk,
        out_shape=jax.ShapeDtypeStruct((B, T, GROUP_WIDTH), BF16),
        scratch_shapes=[pltpu.VMEM((GROUP_WIDTH, GROUP_WIDTH), F32)],
        compiler_params=_cparams(2),
        name="retention",
    )(rq, rk, rv, rg, *consts)


def _rglru_kernel(x_ref, g_ref, cw_ref, cb_ref, wx_ref, bx_ref, wa_ref, ba_ref, lam_ref, o_ref,
                  xbuf, h_ref):
    j = pl.program_id(1)
    tc = RG_TC

    @pl.when(j == 0)
    def _():
        xbuf[0:8, :] = jnp.zeros((8, GROUP_WIDTH), F32)
        h_ref[...] = jnp.zeros_like(h_ref)

    xbuf[8:8 + tc, :] = x_ref[0]
    xc = cb_ref[...] + cw_ref[RG_CONV - 1:RG_CONV, :] * xbuf[8:8 + tc, :]
    for i in range(RG_CONV - 1):
        off = 8 - (RG_CONV - 1) + i
        xc = xc + cw_ref[i:i + 1, :] * xbuf[off:off + tc, :]
    xbuf[0:8, :] = xbuf[tc:tc + 8, :]

    xcb = xc.astype(BF16)
    gate_x = jax.nn.sigmoid(_dot(xcb, wx_ref[...]) + bx_ref[...])
    gate_a = jax.nn.sigmoid(_dot(xcb, wa_ref[...]) + ba_ref[...])
    lam = lam_ref[...]
    softplus_neg = jnp.maximum(-lam, 0.0) + jnp.log1p(jnp.exp(-jnp.abs(lam)))
    log_a = -RG_C * gate_a * softplus_neg
    a = jnp.exp(log_a)
    th = jnp.tanh(log_a)
    b = jnp.sqrt(-2.0 * th / (1.0 - th)) * (gate_x * xc)

    row = lax.broadcasted_iota(I32, (tc, GROUP_WIDTH), 0)
    d = 1
    while d < tc:
        keep = row >= d
        a_sh = jnp.where(keep, pltpu.roll(a, d, 0), 1.0)
        b_sh = jnp.where(keep, pltpu.roll(b, d, 0), 0.0)
        b = a * b_sh + b
        a = a * a_sh
        d *= 2
    h = b + a * h_ref[...]
    h_ref[...] = h[tc - 1:tc, :]

    xg = g_ref[0]
    gelu = 0.5 * xg * (1.0 + jnp.tanh(np.sqrt(2.0 / np.pi) * (xg + 0.044715 * xg * xg * xg)))
    o_ref[0] = (h * gelu).astype(o_ref.dtype)


def _block_diag(w):
    n, c, _ = w.shape
    eye = jnp.eye(n, dtype=w.dtype)
    return (eye[:, None, :, None] * w[:, :, None, :]).reshape(n * c, n * c)


def _rglru(cx, cg, conv_w, conv_b, wx, bx, wa, ba, lam):
    B, T, _ = cx.shape
    tc = RG_TC
    blk = pl.BlockSpec((1, tc, GROUP_WIDTH), lambda b, j: (b, j, 0))
    const = lambda a: pl.BlockSpec(a.shape, lambda b, j: (0,) * a.ndim)
    consts = [conv_w, conv_b, wx, bx, wa, ba, lam]
    return pl.pallas_call(
        _rglru_kernel,
        grid=(B, T // tc),
        in_specs=[blk, blk] + [const(a) for a in consts],
        out_specs=blk,
        out_shape=jax.ShapeDtypeStruct((B, T, GROUP_WIDTH), BF16),
        scratch_shapes=[pltpu.VMEM((tc + 8, GROUP_WIDTH), F32), pltpu.VMEM((1, GROUP_WIDTH), F32)],
        compiler_params=_cparams(2),
        name="rg_lru",
    )(cx, cg, *consts)


def _dsa_kernel(q_ref, k_ref, v_ref, qi_ref, ki_ref, w_ref, o_ref,
                sc_ref, lo_ref, hi_ref, cgt_ref, act_ref, tie_ref, jlo_ref, jhi_ref,
                m_ref, l_ref, acc_ref, *, n_sel, n_keys):
    j = pl.program_id(1)
    R = Q_BLOCK
    KB = DSA_KEY_BLOCK
    SR = GROUP_HEADS * R
    nkb = (j * R + R + KB - 1) // KB
    nsel = float(n_sel)

    rowq = lax.broadcasted_iota(I32, (R, KB), 0)
    colk = lax.broadcasted_iota(I32, (R, KB), 1)
    pos_q = j * R + rowq

    qi = qi_ref[0]
    qi_stack = jnp.concatenate([qi[:, h * IDX_DIM:(h + 1) * IDX_DIM] for h in range(IDX_HEADS)], axis=0)
    w = w_ref[0]

    def score_body(kb, c):
        st = pl.multiple_of(kb * KB, KB)
        rel = jnp.maximum(_dot_nt(qi_stack, ki_ref[0, pl.ds(st, KB), :]), 0.0)
        sc = w[:, 0:1] * rel[0:R]
        for h in range(1, IDX_HEADS):
            sc = sc + w[:, h:h + 1] * rel[h * R:(h + 1) * R]
        sc_ref[kb] = jnp.where(st + colk <= pos_q, sc * IDX_SCALE, NEG)
        return c

    lax.fori_loop(0, nkb, score_body, 0)

    def minmax_body(kb, c):
        mn, mx = c
        s = sc_ref[kb]
        mn = jnp.minimum(mn, jnp.min(jnp.where(s > 0.5 * NEG, s, BIG), axis=1, keepdims=True))
        mx = jnp.maximum(mx, jnp.max(s, axis=1, keepdims=True))
        return mn, mx

    mn, mx = lax.fori_loop(0, nkb, minmax_body,
                           (jnp.full((R, 1), BIG, F32), jnp.full((R, 1), NEG, F32)))

    def count_ge(th):
        def body(kb, c):
            return c + jnp.sum(jnp.where(sc_ref[kb] >= th, 1.0, 0.0), axis=1, keepdims=True)
        return lax.fori_loop(0, nkb, body, jnp.zeros((R, 1), F32))

    n_adm = (j * R + 1 + lax.broadcasted_iota(I32, (R, 1), 0)).astype(F32)
    need = n_adm > nsel
    c_max = count_ge(mx)
    top_tie = need & (c_max >= nsel)
    lo_ref[...] = jnp.where(need, jnp.where(top_tie, mx, mn), NEG)
    hi_ref[...] = mx
    cgt_ref[...] = jnp.where(top_tie, 0.0, c_max)
    act_ref[...] = jnp.where(need & jnp.logical_not(top_tie), 1.0, 0.0)
    tie_ref[...] = jnp.where(top_tie, 1.0, 0.0)

    def bis_cond(c):
        it, n_act = c
        return (n_act > 0.0) & (it < 400)

    def bis_body(c):
        it, _ = c
        lo = lo_ref[...]
        hi = hi_ref[...]
        act = act_ref[...] > 0.0
        mid = lo + (hi - lo) * 0.5
        stuck = (mid <= lo) | (mid >= hi)
        cnt = count_ge(mid)
        go = act & jnp.logical_not(stuck)
        up = go & (cnt >= nsel)
        dn = go & (cnt < nsel)
        lo_ref[...] = jnp.where(up, mid, lo)
        hi_ref[...] = jnp.where(dn, mid, hi)
        cgt_ref[...] = jnp.where(dn, cnt, cgt_ref[...])
        tie_ref[...] = jnp.where(act & stuck, 1.0, tie_ref[...])
        still = go & (cnt != nsel)
        act_ref[...] = jnp.where(still, 1.0, 0.0)
        return it + 1, jnp.max(jnp.where(still, 1.0, 0.0))

    lax.while_loop(bis_cond, bis_body, (jnp.int32(0), jnp.max(act_ref[...])))

    jlo_ref[...] = jnp.full((R, 1), -1.0, F32)
    jhi_ref[...] = jnp.full((R, 1), float(n_keys), F32)

    @pl.when(jnp.max(tie_ref[...]) > 0.0)
    def _():
        lo = lo_ref[...]
        want = nsel - cgt_ref[...]
        tied = tie_ref[...] > 0.0
        jhi_ref[...] = jnp.where(tied, float(n_keys - 1), float(n_keys))

        def jb(it, c):
            a = jlo_ref[...]
            b = jhi_ref[...]
            mid = jnp.floor((a + b) * 0.5)

            def body(kb, cc):
                idx = (kb * KB + colk).astype(F32)
                hit = (sc_ref[kb] == lo) & (idx <= mid)
                return cc + jnp.sum(jnp.where(hit, 1.0, 0.0), axis=1, keepdims=True)

            cnt = lax.fori_loop(0, nkb, body, jnp.zeros((R, 1), F32))
            ok = tied & (cnt >= want)
            jhi_ref[...] = jnp.where(ok, mid, b)
            jlo_ref[...] = jnp.where(tied & jnp.logical_not(ok), mid, a)
            return c

        lax.fori_loop(0, int(np.ceil(np.log2(n_keys))) + 1, jb, 0)

    q_stack = _head_stack(q_ref[0])
    scale = HEAD_DIM ** -0.5
    m_ref[...] = jnp.full(m_ref.shape, NEG, F32)
    l_ref[...] = jnp.zeros(l_ref.shape, F32)
    acc_ref[...] = jnp.zeros(acc_ref.shape, F32)
    lo = lo_ref[...]
    jcut = jhi_ref[...]

    def att_body(kb, c):
        st = pl.multiple_of(kb * KB, KB)
        sc = sc_ref[kb]
        idx = (st + colk).astype(F32)
        keep = (sc > 0.5 * NEG) & ((sc > lo) | ((sc == lo) & (idx <= jcut)))
        keep4 = jnp.concatenate([keep] * GROUP_HEADS, axis=0)
        s = jnp.where(keep4, _dot_nt(q_stack, k_ref[0, pl.ds(st, KB), :]) * scale, NEG)
        m_old = m_ref[...]
        m_new = jnp.maximum(m_old, jnp.max(s, axis=1, keepdims=True))
        alpha = jnp.exp(m_old - m_new)
        p = jnp.where(keep4, jnp.exp(s - m_new), 0.0)
        l_ref[...] = alpha * l_ref[...] + jnp.sum(p, axis=1, keepdims=True)
        acc_ref[...] = alpha * acc_ref[...] + _dot(p.astype(BF16), v_ref[0, pl.ds(st, KB), :])
        m_ref[...] = m_new
        return c

    lax.fori_loop(0, nkb, att_body, 0)
    o_ref[0] = _head_unstack(acc_ref[...] / l_ref[...], R).astype(o_ref.dtype)


def _dsa(q, k, v, qi, ki, w):
    B, T, _ = q.shape
    KB = DSA_KEY_BLOCK
    assert T % KB == 0
    n_sel = min(DSA_TOPK, T // 4)
    SR = GROUP_HEADS * Q_BLOCK
    col = pltpu.VMEM((Q_BLOCK, 1), F32)
    qblk = lambda wd: pl.BlockSpec((1, Q_BLOCK, wd), lambda b, j: (b, j, 0))
    full = lambda wd: pl.BlockSpec((1, T, wd), lambda b, j: (b, 0, 0))
    return pl.pallas_call(
        functools.partial(_dsa_kernel, n_sel=n_sel, n_keys=T),
        grid=(B, T // Q_BLOCK),
        in_specs=[qblk(GROUP_WIDTH), full(GROUP_WIDTH), full(GROUP_WIDTH),
                  qblk(IDX_HEADS * IDX_DIM), full(IDX_DIM), qblk(IDX_HEADS)],
        out_specs=qblk(GROUP_WIDTH),
        out_shape=jax.ShapeDtypeStruct((B, T, GROUP_WIDTH), BF16),
        scratch_shapes=[pltpu.VMEM((T // KB, Q_BLOCK, KB), F32),
                        col, col, col, col, col, col, col,
                        pltpu.VMEM((SR, 1), F32), pltpu.VMEM((SR, 1), F32),
                        pltpu.VMEM((SR, GROUP_WIDTH), F32)],
        compiler_params=_cparams(2),
        name="dsa_attention",
    )(q, k, v, qi, ki, w)


def _outproj_kernel(oa, orr, oc, od, x_ref, w_ref, g_ref, b_ref, rw_ref, rb_ref,
                    x1_ref, ti_ref, tg_ref):
    GW = GROUP_WIDTH
    acc = _dot(oa[...], w_ref[0:GW, :])
    acc = acc + _dot(orr[...], w_ref[GW:2 * GW, :])
    acc = acc + _dot(oc[...], w_ref[2 * GW:3 * GW, :])
    acc = acc + _dot(od[...], w_ref[3 * GW:4 * GW, :])
    x1 = _layer_norm_rows(ALPHA * x_ref[...] + acc, g_ref[...], b_ref[...])
    x1_ref[...] = x1

    logits = _dot(x1.astype(BF16), rw_ref[...]) + rb_ref[...]
    col = lax.broadcasted_iota(I32, logits.shape, 1)
    kcol = lax.broadcasted_iota(I32, ti_ref.shape, 1)
    g = logits
    ti = jnp.zeros(ti_ref.shape, I32)
    tv = jnp.zeros(tg_ref.shape, F32)
    for kk in range(TOP_K):
        mx = jnp.max(g, axis=1, keepdims=True)
        first = jnp.min(jnp.where(g == mx, col, N_EXPERTS), axis=1, keepdims=True)
        ti = jnp.where(kcol == kk, first, ti)
        tv = jnp.where(kcol == kk, mx, tv)
        g = jnp.where(col == first, -jnp.inf, g)
    e = jnp.exp(tv - jnp.max(tv, axis=1, keepdims=True))
    ti_ref[...] = ti
    tg_ref[...] = e / jnp.sum(e, axis=1, keepdims=True)


def _outproj(oa, orr, oc, od, x2, w_out, g, b, rw, rb):
    N = x2.shape[0]
    tm = OUT_TM
    row = lambda w: pl.BlockSpec((tm, w), lambda i: (i, 0))
    const = lambda a: pl.BlockSpec(a.shape, lambda i: (0,) * a.ndim)
    return pl.pallas_call(
        _outproj_kernel,
        grid=(N // tm,),
        in_specs=[row(GROUP_WIDTH)] * 4 + [row(D_MODEL), const(w_out), const(g), const(b), const(rw), const(rb)],
        out_specs=[row(D_MODEL), row(TOP_K), row(TOP_K)],
        out_shape=[jax.ShapeDtypeStruct((N, D_MODEL), F32), jax.ShapeDtypeStruct((N, TOP_K), I32),
                   jax.ShapeDtypeStruct((N, TOP_K), F32)],
        compiler_params=_cparams(1),
        name="outproj_ln_router",
    )(oa, orr, oc, od, x2, w_out, g, b, rw, rb)


def _moe_kernel(te_ref, nu_ref, tok_ref, tokn_ref, x_hbm, w1_ref, b1_ref, w2_ref, b2_ref, y_ref,
                xbuf, sem):
    i = pl.program_id(0)
    tm = MOE_TM
    n_used = nu_ref[0]
    slot = i % 2

    def row_copy(tok, s, r):
        return pltpu.make_async_copy(x_hbm.at[pl.ds(tok, 1)], xbuf.at[s, pl.ds(r, 1)], sem.at[s])

    def issue(idx_ref, s):
        def body(r, c):
            row_copy(idx_ref[r], s, r).start()
            return c
        lax.fori_loop(0, tm, body, 0)

    @pl.when(i == 0)
    def _():
        issue(tok_ref, 0)

    @pl.when(i + 1 < n_used)
    def _():
        issue(tokn_ref, 1 - slot)

    @pl.when(i < n_used)
    def _():
        pltpu.make_async_copy(xbuf.at[slot], xbuf.at[slot], sem.at[slot]).wait()
        xb = xbuf[slot].astype(BF16)
        h = _dot(xb, w1_ref[0]) + b1_ref[0]
        glu_in = jnp.minimum(h[:, :D_FF], SWIGLU_LIMIT)
        up = jnp.clip(h[:, D_FF:], -SWIGLU_LIMIT, SWIGLU_LIMIT)
        glu = glu_in * jax.nn.sigmoid(SWIGLU_ALPHA * glu_in)
        y_ref[...] = _dot(((up + 1.0) * glu).astype(BF16), w2_ref[0]) + b2_ref[0]

    @pl.when(i >= n_used)
    def _():
        y_ref[...] = jnp.zeros_like(y_ref)


def _moe_experts(x1, tile_expert, n_used, tok_of_slot, w1, b1, w2, b2):
    tm = MOE_TM
    n_tiles = tile_expert.shape[0]
    last = n_tiles - 1
    grid_spec = pltpu.PrefetchScalarGridSpec(
        num_scalar_prefetch=2,
        grid=(n_tiles,),
        in_specs=[
            pl.BlockSpec((tm,), lambda i, te, nu: (i,), memory_space=pltpu.SMEM),
            pl.BlockSpec((tm,), lambda i, te, nu: (jnp.minimum(i + 1, last),), memory_space=pltpu.SMEM),
            pl.BlockSpec(memory_space=pl.ANY),
            pl.BlockSpec((1, D_MODEL, 2 * D_FF), lambda i, te, nu: (te[i], 0, 0)),
            pl.BlockSpec((1, 1, 2 * D_FF), lambda i, te, nu: (te[i], 0, 0)),
            pl.BlockSpec((1, D_FF, D_MODEL), lambda i, te, nu: (te[i], 0, 0)),
            pl.BlockSpec((1, 1, D_MODEL), lambda i, te, nu: (te[i], 0, 0)),
        ],
        out_specs=pl.BlockSpec((tm, D_MODEL), lambda i, te, nu: (i, 0)),
        scratch_shapes=[pltpu.VMEM((2, tm, D_MODEL), F32), pltpu.SemaphoreType.DMA((2,))],
    )
    return pl.pallas_call(
        _moe_kernel,
        grid_spec=grid_spec,
        out_shape=jax.ShapeDtypeStruct((n_tiles * tm, D_MODEL), F32),
        compiler_params=_cparams(1),
        name="moe_experts",
    )(tile_expert, n_used, tok_of_slot, tok_of_slot, x1, w1, b1, w2, b2)


def _combine_kernel(pos_ref, posn_ref, y_hbm, x1_ref, tg_ref, g_ref, b_ref, o_ref, ybuf, sem):
    i = pl.program_id(0)
    n = pl.num_programs(0)
    tm = CMB_TM
    slot = i % 2

    def issue(idx_ref, s):
        def body(r, c):
            for kk in range(TOP_K):
                pltpu.make_async_copy(y_hbm.at[pl.ds(idx_ref[r * TOP_K + kk], 1)],
                                      ybuf.at[s, kk, pl.ds(r, 1)], sem.at[s]).start()
            return c
        lax.fori_loop(0, tm, body, 0)

    @pl.when(i == 0)
    def _():
        issue(pos_ref, 0)

    @pl.when(i + 1 < n)
    def _():
        issue(posn_ref, 1 - slot)

    pltpu.make_async_copy(ybuf.at[slot], ybuf.at[slot], sem.at[slot]).wait()
    tg = tg_ref[...]
    moe = tg[:, 0:1] * ybuf[slot, 0]
    for kk in range(1, TOP_K):
        moe = moe + tg[:, kk:kk + 1] * ybuf[slot, kk]
    o_ref[...] = _layer_norm_rows(ALPHA * x1_ref[...] + moe, g_ref[...], b_ref[...])


def _combine(pos, y_sorted, x1, tg, g, b):
    N = x1.shape[0]
    tm = CMB_TM
    n = N // tm
    row = lambda w: pl.BlockSpec((tm, w), lambda i: (i, 0))
    const = lambda a: pl.BlockSpec(a.shape, lambda i: (0,) * a.ndim)
    return pl.pallas_call(
        _combine_kernel,
        grid=(n,),
        in_specs=[pl.BlockSpec((tm * TOP_K,), lambda i: (i,), memory_space=pltpu.SMEM),
                  pl.BlockSpec((tm * TOP_K,), lambda i: (jnp.minimum(i + 1, n - 1),), memory_space=pltpu.SMEM),
                  pl.BlockSpec(memory_space=pl.ANY), row(D_MODEL), row(TOP_K), const(g), const(b)],
        out_specs=row(D_MODEL),
        out_shape=jax.ShapeDtypeStruct((N, D_MODEL), F32),
        scratch_shapes=[pltpu.VMEM((2, TOP_K, tm, D_MODEL), F32), pltpu.SemaphoreType.DMA((2,))],
        compiler_params=_cparams(1),
        name="moe_combine_ln",
    )(pos, pos, y_sorted, x1, tg, g, b)


def _routing_tables(top_i, n_tiles):
    tm = MOE_TM
    flat = top_i.reshape(-1)
    onehot = (flat[:, None] == jnp.arange(N_EXPERTS, dtype=I32)[None, :]).astype(I32)
    incl = jnp.cumsum(onehot, axis=0)
    rank = jnp.sum((incl - onehot) * onehot, axis=1)
    counts = incl[-1]
    padded = ((counts + tm - 1) // tm) * tm
    ends = jnp.cumsum(padded)
    offsets = ends - padded
    pos = offsets[flat] + rank
    tok_of_slot = jnp.zeros((n_tiles * tm,), I32).at[pos].set(jnp.arange(flat.shape[0], dtype=I32) // TOP_K)
    n_used = (ends[-1] // tm).astype(I32)
    tile_start = jnp.arange(n_tiles, dtype=I32) * tm
    tile_expert = jnp.minimum(jnp.searchsorted(ends, tile_start, side="right"), N_EXPERTS - 1).astype(I32)
    last_expert = tile_expert[jnp.maximum(n_used - 1, 0)]
    tile_expert = jnp.where(tile_start < ends[-1], tile_expert, last_expert)
    return pos.astype(I32), tok_of_slot, tile_expert, n_used.reshape(1)


def _rope_tables(T):
    inv = ROPE_THETA ** (-jnp.arange(0, HEAD_DIM, 2, dtype=F32) / HEAD_DIM)
    ang = jnp.arange(T, dtype=F32)[:, None] * inv[None, :]
    cos, sin = jnp.cos(ang), jnp.sin(ang)
    cos_t = jnp.tile(jnp.concatenate([cos, cos], axis=-1), (1, GROUP_HEADS))
    sin_t = jnp.tile(jnp.concatenate([-sin, sin], axis=-1), (1, GROUP_HEADS))
    return cos_t, sin_t


def _pad_w_in(w_in):
    base = 12 * GROUP_WIDTH + IDX_HEADS * IDX_DIM
    w = jnp.zeros((D_MODEL, IN_PAD), F32)
    w = w.at[:, :base + IDX_DIM].set(w_in[:, :base + IDX_DIM])
    w = w.at[:, base + 128:base + 128 + IDX_HEADS].set(w_in[:, base + IDX_DIM:])
    return w.astype(BF16)


def _layer(x2, B, T, cos_t, sin_t, tabs, w_in, ret_gn_g, ret_gn_b, conv_w, conv_b, rg_wx, rg_bx, rg_wa,
           rg_ba, rg_lambda, w_out, ln1_g, ln1_b, router_w, router_b, exp_w1, exp_b1, exp_w2, exp_b2,
           ln2_g, ln2_b):
    N = B * T
    r2 = lambda a: a.reshape(1, -1)
    (aq, ak, av, rq, rk, rv, rg, cx, cg, dq, dk, dv, dqi, dki, dw) = _proj(x2, _pad_w_in(w_in), cos_t, sin_t, T)
    seq = lambda a: a.reshape(B, T, a.shape[-1])
    o_a = _moba(seq(aq), seq(ak), seq(av))
    o_r = _retention(seq(rq), seq(rk), seq(rv), seq(rg), r2(ret_gn_g), r2(ret_gn_b), tabs)
    o_c = _rglru(seq(cx), seq(cg), conv_w, r2(conv_b), _block_diag(rg_wx).astype(BF16), r2(rg_bx),
                 _block_diag(rg_wa).astype(BF16), r2(rg_ba), r2(rg_lambda))
    o_d = _dsa(seq(dq), seq(dk), seq(dv), seq(dqi), seq(dki), seq(dw))
    flat = lambda a: a.reshape(N, GROUP_WIDTH)
    x1, top_i, top_g = _outproj(flat(o_a), flat(o_r), flat(o_c), flat(o_d), x2, w_out.astype(BF16),
                                r2(ln1_g), r2(ln1_b), router_w.astype(BF16), r2(router_b))
    n_tiles = (N * TOP_K) // MOE_TM + N_EXPERTS
    pos, tok_of_slot, tile_expert, n_used = _routing_tables(top_i, n_tiles)
    y_sorted = _moe_experts(x1, tile_expert, n_used, tok_of_slot, exp_w1.astype(BF16),
                            exp_b1.reshape(N_EXPERTS, 1, -1), exp_w2.astype(BF16),
                            exp_b2.reshape(N_EXPERTS, 1, -1))
    return _combine(pos, y_sorted, x1, top_g, r2(ln2_g), r2(ln2_b))


def kernel(x, w_in, ret_gn_g, ret_gn_b, conv_w, conv_b, rg_wx, rg_bx, rg_wa, rg_ba, rg_lambda, w_out,
           ln1_g, ln1_b, router_w, router_b, exp_w1, exp_b1, exp_w2, exp_b2, ln2_g, ln2_b):
    B, T, D = x.shape
    cos_t, sin_t = _rope_tables(T)
    tabs = _ret_tables()
    x2 = x.reshape(B * T, D)
    for l in range(w_in.shape[0]):
        x2 = _layer(x2, B, T, cos_t, sin_t, tabs, w_in[l], ret_gn_g[l], ret_gn_b[l], conv_w[l], conv_b[l],
                    rg_wx[l], rg_bx[l], rg_wa[l], rg_ba[l], rg_lambda[l], w_out[l], ln1_g[l], ln1_b[l],
                    router_w[l], router_b[l], exp_w1[l], exp_b1[l], exp_w2[l], exp_b2[l], ln2_g[l], ln2_b[l])
    return x2.reshape(B, T, D)
```

```python
import functools

import numpy as np
import jax
import jax.numpy as jnp
from jax import lax
from jax.experimental import pallas as pl
from jax.experimental.pallas import tpu as pltpu

F32 = jnp.float32
BF16 = jnp.bfloat16
I32 = jnp.int32

D_MODEL = 1024
DEPTH = 2
HEAD_DIM = 64
GROUP_WIDTH = 256
GROUP_HEADS = 4
ROPE_THETA = 10000.0
Q_BLOCK = 128
MOBA_BLOCK = 256
MOBA_TOPK = 3
MOBA_MAX_BLOCKS = 16
RET_CHUNK = 128
RG_CONV = 4
RG_C = 8.0
IDX_HEADS = 8
IDX_DIM = 64
IDX_SCALE = (IDX_HEADS ** -0.5) * (IDX_DIM ** -0.5)
DSA_TOPK = 256
DSA_KEY_BLOCK = 512
N_EXPERTS = 32
TOP_K = 4
D_FF = 1024
SWIGLU_LIMIT = 7.0
SWIGLU_ALPHA = 1.702
ALPHA = (2 * DEPTH) ** 0.25
LN_EPS = 1e-5
IN_WIDTH = 12 * GROUP_WIDTH + IDX_HEADS * IDX_DIM + IDX_DIM + IDX_HEADS
IN_PAD = 15 * GROUP_WIDTH

NEG = -1e30
BIG = 1e30
VMEM_LIMIT = 56 * 1024 * 1024
FOLD_ROWS = 64

PROJ_TM = 256
OUT_TM = 512
RG_TC = 256
MOE_TM = 256
CMB_TM = 256


def _cparams(ndims):
    return pltpu.CompilerParams(dimension_semantics=("arbitrary",) * ndims,
                                vmem_limit_bytes=VMEM_LIMIT)


def _dot(a, b, precision=None):
    return jnp.dot(a, b, preferred_element_type=F32, precision=precision)


def _dot_nt(a, b, precision=None):
    return lax.dot_general(a, b, (((1,), (1,)), ((), ())), preferred_element_type=F32,
                           precision=precision)


def _dot_tn(a, b):
    return lax.dot_general(a, b, (((0,), (0,)), ((), ())), preferred_element_type=F32)


def _head_stack(q):
    head = lax.shift_right_logical(lax.broadcasted_iota(I32, q.shape, 1), 6)
    qf = q.astype(F32)
    return jnp.concatenate([jnp.where(head == h, qf, 0.0) for h in range(GROUP_HEADS)],
                           axis=0).astype(q.dtype)


def _head_unstack(s, rows):
    head = lax.shift_right_logical(lax.broadcasted_iota(I32, (rows, GROUP_WIDTH), 1), 6)
    out = jnp.zeros((rows, GROUP_WIDTH), F32)
    for h in range(GROUP_HEADS):
        out = out + jnp.where(head == h, s[h * rows:(h + 1) * rows], 0.0)
    return out


def _fold_rows(x, op):
    return op(x.reshape(x.shape[0] // FOLD_ROWS, FOLD_ROWS, x.shape[1]), axis=0)


def _layer_norm_rows(y, g, b):
    mu = jnp.mean(y, axis=-1, keepdims=True)
    yc = y - mu
    var = jnp.mean(yc * yc, axis=-1, keepdims=True)
    return yc * lax.rsqrt(var + LN_EPS) * g + b


def _proj_kernel(x_ref, w_ref, cos_ref, sin_ref,
                 aq, ak, av, rq, rk, rv, rg, cx, cg, dq, dk, dv, dqi, dki, dwt):
    xb = x_ref[...].astype(BF16)
    cos = cos_ref[...]
    sin = sin_ref[...]
    first_half = (lax.broadcasted_iota(I32, cos.shape, 1) & (HEAD_DIM - 1)) < (HEAD_DIM // 2)

    def seg(i):
        return _dot(xb, w_ref[:, i * GROUP_WIDTH:(i + 1) * GROUP_WIDTH])

    def rope(p):
        rot = jnp.where(first_half, pltpu.roll(p, GROUP_WIDTH - HEAD_DIM // 2, 1),
                        pltpu.roll(p, HEAD_DIM // 2, 1))
        return p * cos + rot * sin

    aq[...] = rope(seg(0)).astype(BF16)
    ak[...] = rope(seg(1)).astype(BF16)
    av[...] = seg(2).astype(BF16)
    rq[...] = rope(seg(3)).astype(BF16)
    rk[...] = (rope(seg(4)) * (HEAD_DIM ** -0.5)).astype(BF16)
    rv[...] = seg(5).astype(BF16)
    rg[...] = seg(6)
    cx[...] = seg(7)
    cg[...] = seg(8)
    dq[...] = rope(seg(9)).astype(BF16)
    dk[...] = rope(seg(10)).astype(BF16)
    dv[...] = seg(11).astype(BF16)
    dqi[:, 0:GROUP_WIDTH] = rope(seg(12)).astype(BF16)
    dqi[:, GROUP_WIDTH:2 * GROUP_WIDTH] = rope(seg(13)).astype(BF16)
    last = seg(14)
    dki[...] = rope(last)[:, 0:IDX_DIM].astype(BF16)
    dwt[...] = last[:, 128:256].T[0:IDX_HEADS, :]


def _proj(x2, w_pad, cos_t, sin_t, T):
    N = x2.shape[0]
    tm = PROJ_TM
    tpb = T // tm
    row = lambda w: pl.BlockSpec((tm, w), lambda i: (i, 0))
    tab = pl.BlockSpec((tm, GROUP_WIDTH), lambda i: (i % tpb, 0))
    widths = [256] * 12 + [512, IDX_DIM]
    dtypes = [BF16, BF16, BF16, BF16, BF16, BF16, F32, F32, F32, BF16, BF16, BF16, BF16, BF16]
    return pl.pallas_call(
        _proj_kernel,
        grid=(N // tm,),
        in_specs=[row(D_MODEL), pl.BlockSpec((D_MODEL, IN_PAD), lambda i: (0, 0)), tab, tab],
        out_specs=[row(w) for w in widths] + [pl.BlockSpec((IDX_HEADS, tm), lambda i: (0, i))],
        out_shape=[jax.ShapeDtypeStruct((N, w), d) for w, d in zip(widths, dtypes)]
        + [jax.ShapeDtypeStruct((IDX_HEADS, N), F32)],
        compiler_params=_cparams(1),
        name="proj_rope",
    )(x2, w_pad, cos_t, sin_t)


def _moba_kernel(q_ref, k_ref, v_ref, o_ref, kmean_ref, sel_ref, m_ref, l_ref, acc_ref, *, n_blocks):
    j = pl.program_id(1)
    R = Q_BLOCK
    SR = GROUP_HEADS * R
    KB = MOBA_BLOCK

    @pl.when(j == 0)
    def _():
        kmean_ref[...] = jnp.zeros_like(kmean_ref)
        for n in range(n_blocks):
            kb = k_ref[0, n * KB:(n + 1) * KB, :].astype(F32)
            kmean_ref[n:n + 1, :] = jnp.mean(kb, axis=0, keepdims=True)

    own = j // (KB // R)
    q_stack = _head_stack(q_ref[0])

    gate = _dot_nt(kmean_ref[...], q_stack.astype(F32), precision=lax.Precision.HIGHEST)
    blk = lax.broadcasted_iota(I32, gate.shape, 0)
    past = blk < own
    g = jnp.where(past, gate, -jnp.inf)
    sel = jnp.zeros(gate.shape, F32)
    for _ in range(MOBA_TOPK):
        mx = jnp.max(g, axis=0, keepdims=True)
        first = jnp.min(jnp.where(g == mx, blk, MOBA_MAX_BLOCKS), axis=0, keepdims=True)
        pick = blk == first
        sel = jnp.where(pick & past, 1.0, sel)
        g = jnp.where(pick, -jnp.inf, g)
    sel_ref[...] = sel

    scale = HEAD_DIM ** -0.5
    m_ref[...] = jnp.full(m_ref.shape, NEG, F32)
    l_ref[...] = jnp.zeros(l_ref.shape, F32)
    acc_ref[...] = jnp.zeros(acc_ref.shape, F32)

    def attend(kb, vb, mask):
        s = jnp.where(mask, _dot_nt(kb, q_stack) * scale, NEG)
        m_old = m_ref[...]
        m_new = jnp.maximum(m_old, jnp.max(s, axis=0, keepdims=True))
        alpha = jnp.exp(m_old - m_new)
        p = jnp.exp(s - m_new)
        l_ref[...] = alpha * l_ref[...] + jnp.sum(p, axis=0, keepdims=True)
        acc_ref[...] = alpha * acc_ref[...] + _dot_tn(vb, p.astype(BF16))
        m_ref[...] = m_new

    keypos = own * KB + lax.broadcasted_iota(I32, (KB, SR), 0)
    qpos = j * R + (lax.broadcasted_iota(I32, (KB, SR), 1) & (R - 1))
    st = pl.multiple_of(own * KB, KB)
    attend(k_ref[0, pl.ds(st, KB), :], v_ref[0, pl.ds(st, KB), :], keypos <= qpos)

    def body(n, c):
        s0 = pl.multiple_of(n * KB, KB)
        mask = jnp.broadcast_to(sel_ref[pl.ds(n, 1), :] > 0.5, (KB, SR))
        attend(k_ref[0, pl.ds(s0, KB), :], v_ref[0, pl.ds(s0, KB), :], mask)
        return c

    lax.fori_loop(0, own, body, 0)
    o_ref[0] = _head_unstack((acc_ref[...] / l_ref[...]).T, R).astype(o_ref.dtype)


def _moba(q, k, v):
    B, T, _ = q.shape
    n_blocks = T // MOBA_BLOCK
    assert T % MOBA_BLOCK == 0 and n_blocks <= MOBA_MAX_BLOCKS
    SR = GROUP_HEADS * Q_BLOCK
    return pl.pallas_call(
        functools.partial(_moba_kernel, n_blocks=n_blocks),
        grid=(B, T // Q_BLOCK),
        in_specs=[pl.BlockSpec((1, Q_BLOCK, GROUP_WIDTH), lambda b, j: (b, j, 0)),
                  pl.BlockSpec((1, T, GROUP_WIDTH), lambda b, j: (b, 0, 0)),
                  pl.BlockSpec((1, T, GROUP_WIDTH), lambda b, j: (b, 0, 0))],
        out_specs=pl.BlockSpec((1, Q_BLOCK, GROUP_WIDTH), lambda b, j: (b, j, 0)),
        out_shape=jax.ShapeDtypeStruct((B, T, GROUP_WIDTH), BF16),
        scratch_shapes=[pltpu.VMEM((MOBA_MAX_BLOCKS, GROUP_WIDTH), F32),
                        pltpu.VMEM((MOBA_MAX_BLOCKS, SR), F32),
                        pltpu.VMEM((1, SR), F32), pltpu.VMEM((1, SR), F32),
                        pltpu.VMEM((GROUP_WIDTH, SR), F32)],
        compiler_params=_cparams(2),
        name="moba_attention",
    )(q, k, v)


def _ret_kernel(q_ref, k_ref, v_ref, g_ref, dmask_ref, xi_ref, zeta_ref, gdec_ref, bd_ref, avg_ref,
                gng_ref, gnb_ref, o_ref, r_ref):
    j = pl.program_id(1)

    @pl.when(j == 0)
    def _():
        r_ref[...] = jnp.zeros_like(r_ref)

    C = RET_CHUNK
    q = q_ref[0]
    k = k_ref[0]
    v = v_ref[0]
    q_stack = _head_stack(q)
    inner = _dot_nt(q_stack, k) * dmask_ref[...]
    o = _head_unstack(_dot(inner.astype(BF16), v), C)
    R = r_ref[...]
    o = o + _dot(q, R.astype(BF16)) * xi_ref[...]
    kz = (k.astype(F32) * zeta_ref[...]).astype(BF16)
    r_ref[...] = gdec_ref[...] * R + bd_ref[...] * _dot_tn(kz, v)

    hp = lax.Precision.HIGHEST
    mu = _dot(o, avg_ref[...], precision=hp)
    oc = o - mu
    var = _dot(oc * oc, avg_ref[...], precision=hp)
    y = oc * lax.rsqrt(var + LN_EPS) * gng_ref[...] + gnb_ref[...]
    gte = g_ref[0]
    o_ref[0] = (y * (gte * jax.nn.sigmoid(gte))).astype(o_ref.dtype)


def _ret_tables():
    H, C, d = GROUP_HEADS, RET_CHUNK, HEAD_DIM
    log_g = np.log(1.0 - 2.0 ** (-5.0 - np.arange(H, dtype=np.float64)))
    n = np.arange(C, dtype=np.float64)
    diff = n[:, None] - n[None, :]
    dmask = np.where(diff >= 0, np.exp(log_g[:, None, None] * np.maximum(diff, 0.0)), 0.0)
    xi = np.exp(log_g[:, None] * (n + 1.0))
    zeta = np.exp(log_g[:, None] * (C - 1.0 - n))
    g_chunk = np.exp(log_g * C)
    head = np.arange(GROUP_WIDTH) // d
    bd = (head[:, None] == head[None, :]).astype(np.float64)
    to32 = lambda a: jnp.asarray(a, dtype=F32)
    return dict(dmask=to32(dmask.reshape(H * C, C)), xi=to32(xi.T[:, head]), zeta=to32(zeta.T[:, head]),
                gdec=to32(bd * g_chunk[head][:, None]), bd=to32(bd), avg=to32(bd / d))


def _retention(rq, rk, rv, rg, gn_g, gn_b, tabs):
    B, T, _ = rq.shape
    C = RET_CHUNK
    blk = pl.BlockSpec((1, C, GROUP_WIDTH), lambda b, j: (b, j, 0))
    const = lambda a: pl.BlockSpec(a.shape, lambda b, j: (0,) * a.ndim)
    consts = [tabs["dmask"], tabs["xi"], tabs["zeta"], tabs["gdec"], tabs["bd"], tabs["avg"], gn_g, gn_b]
    return pl.pallas_call(
        _ret_kernel,
        grid=(B, T // C),
        in_specs=[blk, blk, blk, blk] + [const(a) for a in consts],
        out_specs=blk,
        out_shape=jax.ShapeDtypeStruct((B, T, GROUP_WIDTH), BF16),
        scratch_shapes=[pltpu.VMEM((GROUP_WIDTH, GROUP_WIDTH), F32)],
        compiler_params=_cparams(2),
        name="retention",
    )(rq, rk, rv, rg, *consts)


def _rglru_kernel(x_ref, g_ref, cw_ref, cb_ref, wx_ref, bx_ref, wa_ref, ba_ref, lam_ref, o_ref,
                  xbuf, h_ref):
    j = pl.program_id(1)
    tc = RG_TC

    @pl.when(j == 0)
    def _():
        xbuf[0:8, :] = jnp.zeros((8, GROUP_WIDTH), F32)
        h_ref[...] = jnp.zeros_like(h_ref)

    xbuf[8:8 + tc, :] = x_ref[0]
    xc = cb_ref[...] + cw_ref[RG_CONV - 1:RG_CONV, :] * xbuf[8:8 + tc, :]
    for i in range(RG_CONV - 1):
        off = 8 - (RG_CONV - 1) + i
        xc = xc + cw_ref[i:i + 1, :] * xbuf[off:off + tc, :]
    xbuf[0:8, :] = xbuf[tc:tc + 8, :]

    xcb = xc.astype(BF16)
    gate_x = jax.nn.sigmoid(_dot(xcb, wx_ref[...]) + bx_ref[...])
    gate_a = jax.nn.sigmoid(_dot(xcb, wa_ref[...]) + ba_ref[...])
    lam = lam_ref[...]
    softplus_neg = jnp.maximum(-lam, 0.0) + jnp.log1p(jnp.exp(-jnp.abs(lam)))
    log_a = -RG_C * gate_a * softplus_neg
    a = jnp.exp(log_a)
    th = jnp.tanh(log_a)
    b = jnp.sqrt(-2.0 * th / (1.0 - th)) * (gate_x * xc)

    row = lax.broadcasted_iota(I32, (tc, GROUP_WIDTH), 0)
    d = 1
    while d < tc:
        keep = row >= d
        a_sh = jnp.where(keep, pltpu.roll(a, d, 0), 1.0)
        b_sh = jnp.where(keep, pltpu.roll(b, d, 0), 0.0)
        b = a * b_sh + b
        a = a * a_sh
        d *= 2
    h = b + a * h_ref[...]
    h_ref[...] = h[tc - 1:tc, :]

    xg = g_ref[0]
    gelu = 0.5 * xg * (1.0 + jnp.tanh(np.sqrt(2.0 / np.pi) * (xg + 0.044715 * xg * xg * xg)))
    o_ref[0] = (h * gelu).astype(o_ref.dtype)


def _block_diag(w):
    n, c, _ = w.shape
    eye = jnp.eye(n, dtype=w.dtype)
    return (eye[:, None, :, None] * w[:, :, None, :]).reshape(n * c, n * c)


def _rglru(cx, cg, conv_w, conv_b, wx, bx, wa, ba, lam):
    B, T, _ = cx.shape
    tc = RG_TC
    blk = pl.BlockSpec((1, tc, GROUP_WIDTH), lambda b, j: (b, j, 0))
    const = lambda a: pl.BlockSpec(a.shape, lambda b, j: (0,) * a.ndim)
    consts = [conv_w, conv_b, wx, bx, wa, ba, lam]
    return pl.pallas_call(
        _rglru_kernel,
        grid=(B, T // tc),
        in_specs=[blk, blk] + [const(a) for a in consts],
        out_specs=blk,
        out_shape=jax.ShapeDtypeStruct((B, T, GROUP_WIDTH), BF16),
        scratch_shapes=[pltpu.VMEM((tc + 8, GROUP_WIDTH), F32), pltpu.VMEM((1, GROUP_WIDTH), F32)],
        compiler_params=_cparams(2),
        name="rg_lru",
    )(cx, cg, *consts)


def _dsa_kernel(q_ref, k_ref, v_ref, qi_ref, ki_ref, wt_ref, o_ref,
                sc_ref, jcut_ref, m_ref, l_ref, acc_ref, *, n_sel, n_keys):
    j = pl.program_id(1)
    R = Q_BLOCK
    KB = DSA_KEY_BLOCK
    SR = GROUP_HEADS * R
    nkb = (j * R + R + KB - 1) // KB
    nsel = float(n_sel)

    rowk = lax.broadcasted_iota(I32, (KB, R), 0)
    qpos = j * R + lax.broadcasted_iota(I32, (KB, R), 1)

    qi = qi_ref[0]
    qi_stack = jnp.concatenate([qi[:, h * IDX_DIM:(h + 1) * IDX_DIM] for h in range(IDX_HEADS)], axis=0)
    wt = wt_ref[...]

    def score_body(kb, c):
        st = pl.multiple_of(kb * KB, KB)
        rel = _dot_nt(ki_ref[0, pl.ds(st, KB), :], qi_stack)
        sc = wt[0:1, :] * jnp.maximum(rel[:, 0:R], 0.0)
        for h in range(1, IDX_HEADS):
            sc = sc + wt[h:h + 1, :] * jnp.maximum(rel[:, h * R:(h + 1) * R], 0.0)
        sc_ref[pl.ds(st, KB), :] = jnp.where(st + rowk <= qpos, sc * IDX_SCALE, NEG)
        return c

    lax.fori_loop(0, nkb, score_body, 0)

    def blocks(fn, init):
        def body(kb, c):
            st = pl.multiple_of(kb * KB, KB)
            return fn(st, sc_ref[pl.ds(st, KB), :], c)
        return lax.fori_loop(0, nkb, body, init)

    def minmax(st, s, c):
        mn, mx = c
        return (jnp.minimum(mn, _fold_rows(jnp.where(s > 0.5 * NEG, s, BIG), jnp.min)),
                jnp.maximum(mx, _fold_rows(s, jnp.max)))

    mn8, mx8 = blocks(minmax, (jnp.full((FOLD_ROWS, R), BIG, F32), jnp.full((FOLD_ROWS, R), NEG, F32)))
    mn = jnp.min(mn8, axis=0, keepdims=True)
    mx = jnp.max(mx8, axis=0, keepdims=True)

    def count_ge(th):
        acc = blocks(lambda st, s, c: c + _fold_rows(jnp.where(s >= th, 1.0, 0.0), jnp.sum),
                     jnp.zeros((FOLD_ROWS, R), F32))
        return jnp.sum(acc, axis=0, keepdims=True)

    n_adm = (j * R + 1 + lax.broadcasted_iota(I32, (1, R), 1)).astype(F32)
    need = n_adm > nsel
    c_max = count_ge(mx)
    top_tie = need & (c_max >= nsel)
    lo0 = jnp.where(need, jnp.where(top_tie, mx, mn), NEG)
    cgt0 = jnp.where(top_tie, 0.0, c_max)
    act0 = jnp.where(need & jnp.logical_not(top_tie), 1.0, 0.0)
    tie0 = jnp.where(top_tie, 1.0, 0.0)

    def bis_step(lo, hi, cgt, act, tie):
        on = act > 0.0
        mid = lo + (hi - lo) * 0.5
        stuck = (mid <= lo) | (mid >= hi)
        cnt = count_ge(mid)
        go = on & jnp.logical_not(stuck)
        up = go & (cnt >= nsel)
        dn = go & (cnt < nsel)
        return (jnp.where(up, mid, lo), jnp.where(dn, mid, hi), jnp.where(dn, cnt, cgt),
                jnp.where(go & (cnt != nsel), 1.0, 0.0), jnp.where(on & stuck, 1.0, tie))

    def bis_cond(c):
        return (c[1] > 0.0) & (c[0] < 400)

    def bis_body(c):
        st = bis_step(*bis_step(*c[2:]))
        return (c[0] + 2, jnp.max(st[3])) + st

    res = lax.while_loop(bis_cond, bis_body, (jnp.int32(0), jnp.max(act0), lo0, mx, cgt0, act0, tie0))
    lo, cgt, tie = res[2], res[4], res[6]

    jcut_ref[...] = jnp.full((1, R), float(n_keys), F32)

    @pl.when(jnp.max(tie) > 0.0)
    def _():
        want = nsel - cgt
        tied = tie > 0.0

        def jb(it, c):
            a, b = c
            mid = jnp.floor((a + b) * 0.5)
            hit8 = blocks(lambda st, s, cc: cc + _fold_rows(
                jnp.where((s == lo) & ((st + rowk).astype(F32) <= mid), 1.0, 0.0), jnp.sum),
                jnp.zeros((FOLD_ROWS, R), F32))
            ok = jnp.sum(hit8, axis=0, keepdims=True) >= want
            return jnp.where(ok, a, mid), jnp.where(ok, mid, b)

        _, b = lax.fori_loop(0, int(np.ceil(np.log2(n_keys))) + 1, jb,
                             (jnp.full((1, R), -1.0, F32), jnp.full((1, R), float(n_keys - 1), F32)))
        jcut_ref[...] = jnp.where(tied, b, float(n_keys))

    q_stack = _head_stack(q_ref[0])
    scale = HEAD_DIM ** -0.5
    m_ref[...] = jnp.full(m_ref.shape, NEG, F32)
    l_ref[...] = jnp.zeros(l_ref.shape, F32)
    acc_ref[...] = jnp.zeros(acc_ref.shape, F32)
    jcut = jcut_ref[...]

    def att_body(kb, c):
        st = pl.multiple_of(kb * KB, KB)
        sc = sc_ref[pl.ds(st, KB), :]
        keep = (sc > 0.5 * NEG) & ((sc > lo) | ((sc == lo) & ((st + rowk).astype(F32) <= jcut)))
        keepf = jnp.where(keep, 1.0, 0.0)
        keep4 = jnp.concatenate([keepf] * GROUP_HEADS, axis=1) > 0.5
        s = jnp.where(keep4, _dot_nt(k_ref[0, pl.ds(st, KB), :], q_stack) * scale, NEG)
        m_old = m_ref[...]
        m_new = jnp.maximum(m_old, jnp.max(s, axis=0, keepdims=True))
        alpha = jnp.exp(m_old - m_new)
        p = jnp.where(keep4, jnp.exp(s - m_new), 0.0)
        l_ref[...] = alpha * l_ref[...] + jnp.sum(p, axis=0, keepdims=True)
        acc_ref[...] = alpha * acc_ref[...] + _dot_tn(v_ref[0, pl.ds(st, KB), :], p.astype(BF16))
        m_ref[...] = m_new
        return c

    lax.fori_loop(0, nkb, att_body, 0)
    o_ref[0] = _head_unstack((acc_ref[...] / l_ref[...]).T, R).astype(o_ref.dtype)


def _dsa(q, k, v, qi, ki, wt):
    B, T, _ = q.shape
    KB = DSA_KEY_BLOCK
    assert T % KB == 0
    n_sel = min(DSA_TOPK, T // 4)
    SR = GROUP_HEADS * Q_BLOCK
    nq = T // Q_BLOCK
    qblk = lambda wd: pl.BlockSpec((1, Q_BLOCK, wd), lambda b, j: (b, j, 0))
    full = lambda wd: pl.BlockSpec((1, T, wd), lambda b, j: (b, 0, 0))
    return pl.pallas_call(
        functools.partial(_dsa_kernel, n_sel=n_sel, n_keys=T),
        grid=(B, nq),
        in_specs=[qblk(GROUP_WIDTH), full(GROUP_WIDTH), full(GROUP_WIDTH),
                  qblk(IDX_HEADS * IDX_DIM), full(IDX_DIM),
                  pl.BlockSpec((IDX_HEADS, Q_BLOCK), lambda b, j: (0, b * nq + j))],
        out_specs=qblk(GROUP_WIDTH),
        out_shape=jax.ShapeDtypeStruct((B, T, GROUP_WIDTH), BF16),
        scratch_shapes=[pltpu.VMEM((T, Q_BLOCK), F32), pltpu.VMEM((1, Q_BLOCK), F32),
                        pltpu.VMEM((1, SR), F32), pltpu.VMEM((1, SR), F32),
                        pltpu.VMEM((GROUP_WIDTH, SR), F32)],
        compiler_params=_cparams(2),
        name="dsa_attention",
    )(q, k, v, qi, ki, wt)


def _outproj_kernel(oa, orr, oc, od, x_ref, w_ref, g_ref, b_ref, rw_ref, rb_ref,
                    x1_ref, ti_ref, tg_ref):
    GW = GROUP_WIDTH
    acc = _dot(oa[...], w_ref[0:GW, :])
    acc = acc + _dot(orr[...], w_ref[GW:2 * GW, :])
    acc = acc + _dot(oc[...], w_ref[2 * GW:3 * GW, :])
    acc = acc + _dot(od[...], w_ref[3 * GW:4 * GW, :])
    x1 = _layer_norm_rows(ALPHA * x_ref[...] + acc, g_ref[...], b_ref[...])
    x1_ref[...] = x1

    logits = _dot(x1.astype(BF16), rw_ref[...]) + rb_ref[...]
    col = lax.broadcasted_iota(I32, logits.shape, 1)
    kcol = lax.broadcasted_iota(I32, ti_ref.shape, 1)
    g = logits
    ti = jnp.zeros(ti_ref.shape, I32)
    tv = jnp.zeros(tg_ref.shape, F32)
    for kk in range(TOP_K):
        mx = jnp.max(g, axis=1, keepdims=True)
        first = jnp.min(jnp.where(g == mx, col, N_EXPERTS), axis=1, keepdims=True)
        ti = jnp.where(kcol == kk, first, ti)
        tv = jnp.where(kcol == kk, mx, tv)
        g = jnp.where(col == first, -jnp.inf, g)
    e = jnp.exp(tv - jnp.max(tv, axis=1, keepdims=True))
    ti_ref[...] = ti
    tg_ref[...] = e / jnp.sum(e, axis=1, keepdims=True)


def _outproj(oa, orr, oc, od, x2, w_out, g, b, rw, rb):
    N = x2.shape[0]
    tm = OUT_TM
    row = lambda w: pl.BlockSpec((tm, w), lambda i: (i, 0))
    const = lambda a: pl.BlockSpec(a.shape, lambda i: (0,) * a.ndim)
    return pl.pallas_call(
        _outproj_kernel,
        grid=(N // tm,),
        in_specs=[row(GROUP_WIDTH)] * 4 + [row(D_MODEL), const(w_out), const(g), const(b), const(rw), const(rb)],
        out_specs=[row(D_MODEL), row(TOP_K), row(TOP_K)],
        out_shape=[jax.ShapeDtypeStruct((N, D_MODEL), F32), jax.ShapeDtypeStruct((N, TOP_K), I32),
                   jax.ShapeDtypeStruct((N, TOP_K), F32)],
        compiler_params=_cparams(1),
        name="outproj_ln_router",
    )(oa, orr, oc, od, x2, w_out, g, b, rw, rb)


def _moe_kernel(te_ref, nu_ref, tok_ref, tokn_ref, x_hbm, w1_ref, b1_ref, w2_ref, b2_ref, y_ref,
                xbuf, sem):
    i = pl.program_id(0)
    tm = MOE_TM
    n_used = nu_ref[0]
    slot = i % 2

    def row_copy(tok, s, r):
        return pltpu.make_async_copy(x_hbm.at[pl.ds(tok, 1)], xbuf.at[s, pl.ds(r, 1)], sem.at[s])

    def issue(idx_ref, s):
        def body(r, c):
            row_copy(idx_ref[r], s, r).start()
            return c
        lax.fori_loop(0, tm, body, 0)

    @pl.when(i == 0)
    def _():
        issue(tok_ref, 0)

    @pl.when(i + 1 < n_used)
    def _():
        issue(tokn_ref, 1 - slot)

    @pl.when(i < n_used)
    def _():
        pltpu.make_async_copy(xbuf.at[slot], xbuf.at[slot], sem.at[slot]).wait()
        xb = xbuf[slot].astype(BF16)
        h = _dot(xb, w1_ref[0]) + b1_ref[0]
        glu_in = jnp.minimum(h[:, :D_FF], SWIGLU_LIMIT)
        up = jnp.clip(h[:, D_FF:], -SWIGLU_LIMIT, SWIGLU_LIMIT)
        glu = glu_in * jax.nn.sigmoid(SWIGLU_ALPHA * glu_in)
        y_ref[...] = _dot(((up + 1.0) * glu).astype(BF16), w2_ref[0]) + b2_ref[0]

    @pl.when(i >= n_used)
    def _():
        y_ref[...] = jnp.zeros_like(y_ref)


def _moe_experts(x1, tile_expert, n_used, tok_of_slot, w1, b1, w2, b2):
    tm = MOE_TM
    n_tiles = tile_expert.shape[0]
    last = n_tiles - 1
    grid_spec = pltpu.PrefetchScalarGridSpec(
        num_scalar_prefetch=2,
        grid=(n_tiles,),
        in_specs=[
            pl.BlockSpec((tm,), lambda i, te, nu: (i,), memory_space=pltpu.SMEM),
            pl.BlockSpec((tm,), lambda i, te, nu: (jnp.minimum(i + 1, last),), memory_space=pltpu.SMEM),
            pl.BlockSpec(memory_space=pl.ANY),
            pl.BlockSpec((1, D_MODEL, 2 * D_FF), lambda i, te, nu: (te[i], 0, 0)),
            pl.BlockSpec((1, 1, 2 * D_FF), lambda i, te, nu: (te[i], 0, 0)),
            pl.BlockSpec((1, D_FF, D_MODEL), lambda i, te, nu: (te[i], 0, 0)),
            pl.BlockSpec((1, 1, D_MODEL), lambda i, te, nu: (te[i], 0, 0)),
        ],
        out_specs=pl.BlockSpec((tm, D_MODEL), lambda i, te, nu: (i, 0)),
        scratch_shapes=[pltpu.VMEM((2, tm, D_MODEL), F32), pltpu.SemaphoreType.DMA((2,))],
    )
    return pl.pallas_call(
        _moe_kernel,
        grid_spec=grid_spec,
        out_shape=jax.ShapeDtypeStruct((n_tiles * tm, D_MODEL), F32),
        compiler_params=_cparams(1),
        name="moe_experts",
    )(tile_expert, n_used, tok_of_slot, tok_of_slot, x1, w1, b1, w2, b2)


def _combine_kernel(pos_ref, posn_ref, y_hbm, x1_ref, tg_ref, g_ref, b_ref, o_ref, ybuf, sem):
    i = pl.program_id(0)
    n = pl.num_programs(0)
    tm = CMB_TM
    slot = i % 2

    def issue(idx_ref, s):
        def body(r, c):
            for kk in range(TOP_K):
                pltpu.make_async_copy(y_hbm.at[pl.ds(idx_ref[r * TOP_K + kk], 1)],
                                      ybuf.at[s, kk, pl.ds(r, 1)], sem.at[s]).start()
            return c
        lax.fori_loop(0, tm, body, 0)

    @pl.when(i == 0)
    def _():
        issue(pos_ref, 0)

    @pl.when(i + 1 < n)
    def _():
        issue(posn_ref, 1 - slot)

    pltpu.make_async_copy(ybuf.at[slot], ybuf.at[slot], sem.at[slot]).wait()
    tg = tg_ref[...]
    moe = tg[:, 0:1] * ybuf[slot, 0]
    for kk in range(1, TOP_K):
        moe = moe + tg[:, kk:kk + 1] * ybuf[slot, kk]
    o_ref[...] = _layer_norm_rows(ALPHA * x1_ref[...] + moe, g_ref[...], b_ref[...])


def _combine(pos, y_sorted, x1, tg, g, b):
    N = x1.shape[0]
    tm = CMB_TM
    n = N // tm
    row = lambda w: pl.BlockSpec((tm, w), lambda i: (i, 0))
    const = lambda a: pl.BlockSpec(a.shape, lambda i: (0,) * a.ndim)
    return pl.pallas_call(
        _combine_kernel,
        grid=(n,),
        in_specs=[pl.BlockSpec((tm * TOP_K,), lambda i: (i,), memory_space=pltpu.SMEM),
                  pl.BlockSpec((tm * TOP_K,), lambda i: (jnp.minimum(i + 1, n - 1),), memory_space=pltpu.SMEM),
                  pl.BlockSpec(memory_space=pl.ANY), row(D_MODEL), row(TOP_K), const(g), const(b)],
        out_specs=row(D_MODEL),
        out_shape=jax.ShapeDtypeStruct((N, D_MODEL), F32),
        scratch_shapes=[pltpu.VMEM((2, TOP_K, tm, D_MODEL), F32), pltpu.SemaphoreType.DMA((2,))],
        compiler_params=_cparams(1),
        name="moe_combine_ln",
    )(pos, pos, y_sorted, x1, tg, g, b)


def _routing_tables(top_i, n_tiles):
    tm = MOE_TM
    flat = top_i.reshape(-1)
    onehot = (flat[:, None] == jnp.arange(N_EXPERTS, dtype=I32)[None, :]).astype(I32)
    incl = jnp.cumsum(onehot, axis=0)
    rank = jnp.sum((incl - onehot) * onehot, axis=1)
    counts = incl[-1]
    padded = ((counts + tm - 1) // tm) * tm
    ends = jnp.cumsum(padded)
    offsets = ends - padded
    pos = offsets[flat] + rank
    tok_of_slot = jnp.zeros((n_tiles * tm,), I32).at[pos].set(jnp.arange(flat.shape[0], dtype=I32) // TOP_K)
    n_used = (ends[-1] // tm).astype(I32)
    tile_start = jnp.arange(n_tiles, dtype=I32) * tm
    tile_expert = jnp.sum((ends[None, :] <= tile_start[:, None]).astype(I32), axis=1)
    tile_expert = jnp.minimum(tile_expert, N_EXPERTS - 1)
    last_expert = tile_expert[jnp.maximum(n_used - 1, 0)]
    tile_expert = jnp.where(tile_start < ends[-1], tile_expert, last_expert)
    return pos.astype(I32), tok_of_slot, tile_expert, n_used.reshape(1)


def _rope_tables(T):
    inv = ROPE_THETA ** (-jnp.arange(0, HEAD_DIM, 2, dtype=F32) / HEAD_DIM)
    ang = jnp.arange(T, dtype=F32)[:, None] * inv[None, :]
    cos, sin = jnp.cos(ang), jnp.sin(ang)
    cos_t = jnp.tile(jnp.concatenate([cos, cos], axis=-1), (1, GROUP_HEADS))
    sin_t = jnp.tile(jnp.concatenate([-sin, sin], axis=-1), (1, GROUP_HEADS))
    return cos_t, sin_t


def _pad_w_in(w_in):
    base = 12 * GROUP_WIDTH + IDX_HEADS * IDX_DIM
    w = jnp.zeros((D_MODEL, IN_PAD), F32)
    w = w.at[:, :base + IDX_DIM].set(w_in[:, :base + IDX_DIM])
    w = w.at[:, base + 128:base + 128 + IDX_HEADS].set(w_in[:, base + IDX_DIM:])
    return w.astype(BF16)


def _layer(x2, B, T, cos_t, sin_t, tabs, w_in, ret_gn_g, ret_gn_b, conv_w, conv_b, rg_wx, rg_bx, rg_wa,
           rg_ba, rg_lambda, w_out, ln1_g, ln1_b, router_w, router_b, exp_w1, exp_b1, exp_w2, exp_b2,
           ln2_g, ln2_b):
    N = B * T
    r2 = lambda a: a.reshape(1, -1)
    (aq, ak, av, rq, rk, rv, rg, cx, cg, dq, dk, dv, dqi, dki, dwt) = _proj(x2, _pad_w_in(w_in), cos_t, sin_t, T)
    seq = lambda a: a.reshape(B, T, a.shape[-1])
    o_a = _moba(seq(aq), seq(ak), seq(av))
    o_r = _retention(seq(rq), seq(rk), seq(rv), seq(rg), r2(ret_gn_g), r2(ret_gn_b), tabs)
    o_c = _rglru(seq(cx), seq(cg), conv_w, r2(conv_b), _block_diag(rg_wx).astype(BF16), r2(rg_bx),
                 _block_diag(rg_wa).astype(BF16), r2(rg_ba), r2(rg_lambda))
    o_d = _dsa(seq(dq), seq(dk), seq(dv), seq(dqi), seq(dki), dwt)
    flat = lambda a: a.reshape(N, GROUP_WIDTH)
    x1, top_i, top_g = _outproj(flat(o_a), flat(o_r), flat(o_c), flat(o_d), x2, w_out.astype(BF16),
                                r2(ln1_g), r2(ln1_b), router_w.astype(BF16), r2(router_b))
    n_tiles = (N * TOP_K) // MOE_TM + N_EXPERTS
    pos, tok_of_slot, tile_expert, n_used = _routing_tables(top_i, n_tiles)
    y_sorted = _moe_experts(x1, tile_expert, n_used, tok_of_slot, exp_w1.astype(BF16),
                            exp_b1.reshape(N_EXPERTS, 1, -1), exp_w2.astype(BF16),
                            exp_b2.reshape(N_EXPERTS, 1, -1))
    return _combine(pos, y_sorted, x1, top_g, r2(ln2_g), r2(ln2_b))


def kernel(x, w_in, ret_gn_g, ret_gn_b, conv_w, conv_b, rg_wx, rg_bx, rg_wa, rg_ba, rg_lambda, w_out,
           ln1_g, ln1_b, router_w, router_b, exp_w1, exp_b1, exp_w2, exp_b2, ln2_g, ln2_b):
    B, T, D = x.shape
    cos_t, sin_t = _rope_tables(T)
    tabs = _ret_tables()
    x2 = x.reshape(B * T, D)
    for l in range(w_in.shape[0]):
        x2 = _layer(x2, B, T, cos_t, sin_t, tabs, w_in[l], ret_gn_g[l], ret_gn_b[l], conv_w[l], conv_b[l],
                    rg_wx[l], rg_bx[l], rg_wa[l], rg_ba[l], rg_lambda[l], w_out[l], ln1_g[l], ln1_b[l],
                    router_w[l], router_b[l], exp_w1[l], exp_b1[l], exp_w2[l], exp_b2[l], ln2_g[l], ln2_b[l])
    return x2.reshape(B, T, D)
```

```python
import functools

import numpy as np
import jax
import jax.numpy as jnp
from jax import lax
from jax.experimental import pallas as pl
from jax.experimental.pallas import tpu as pltpu

F32 = jnp.float32
BF16 = jnp.bfloat16
I32 = jnp.int32

D_MODEL = 1024
DEPTH = 2
HEAD_DIM = 64
GROUP_WIDTH = 256
GROUP_HEADS = 4
ROPE_THETA = 10000.0
Q_BLOCK = 128
MOBA_BLOCK = 256
MOBA_TOPK = 3
MOBA_MAX_BLOCKS = 16
RET_CHUNK = 128
RG_CONV = 4
RG_C = 8.0
IDX_HEADS = 8
IDX_DIM = 64
IDX_SCALE = (IDX_HEADS ** -0.5) * (IDX_DIM ** -0.5)
DSA_TOPK = 256
DSA_KEY_BLOCK = 512
N_EXPERTS = 32
TOP_K = 4
D_FF = 1024
SWIGLU_LIMIT = 7.0
SWIGLU_ALPHA = 1.702
ALPHA = (2 * DEPTH) ** 0.25
LN_EPS = 1e-5
IN_WIDTH = 12 * GROUP_WIDTH + IDX_HEADS * IDX_DIM + IDX_DIM + IDX_HEADS
IN_PAD = 15 * GROUP_WIDTH

NEG = -1e30
M_FLOOR = -1e29
BIG = 1e30
VMEM_LIMIT = 56 * 1024 * 1024
FOLD_ROWS = 64

PROJ_TM = 256
OUT_TM = 512
RG_TC = 256
MOE_TM = 256
DSP_TM = 256
CMB_TM = 256


def _cparams(ndims):
    return pltpu.CompilerParams(dimension_semantics=("arbitrary",) * ndims,
                                vmem_limit_bytes=VMEM_LIMIT)


def _dot(a, b, precision=None):
    return jnp.dot(a, b, preferred_element_type=F32, precision=precision)


def _dot_nt(a, b, precision=None):
    return lax.dot_general(a, b, (((1,), (1,)), ((), ())), preferred_element_type=F32,
                           precision=precision)


def _dot_tn(a, b):
    return lax.dot_general(a, b, (((0,), (0,)), ((), ())), preferred_element_type=F32)


def _head_stack(q):
    head = lax.shift_right_logical(lax.broadcasted_iota(I32, q.shape, 1), 6)
    qf = q.astype(F32)
    return jnp.concatenate([jnp.where(head == h, qf, 0.0) for h in range(GROUP_HEADS)],
                           axis=0).astype(q.dtype)


def _head_unstack(s, rows):
    head = lax.shift_right_logical(lax.broadcasted_iota(I32, (rows, GROUP_WIDTH), 1), 6)
    out = jnp.zeros((rows, GROUP_WIDTH), F32)
    for h in range(GROUP_HEADS):
        out = out + jnp.where(head == h, s[h * rows:(h + 1) * rows], 0.0)
    return out


def _fold_rows(x, op):
    return op(x.reshape(x.shape[0] // FOLD_ROWS, FOLD_ROWS, x.shape[1]), axis=0)


def _layer_norm_rows(y, g, b):
    mu = jnp.mean(y, axis=-1, keepdims=True)
    yc = y - mu
    var = jnp.mean(yc * yc, axis=-1, keepdims=True)
    return yc * lax.rsqrt(var + LN_EPS) * g + b


def _proj_kernel(x_ref, w_ref, cos_ref, sin_ref,
                 aq, ak, av, rq, rk, rv, rg, cx, cg, dq, dk, dv, dqi, dki, dwt):
    xb = x_ref[...].astype(BF16)
    cos = cos_ref[...]
    sin = sin_ref[...]
    first_half = (lax.broadcasted_iota(I32, cos.shape, 1) & (HEAD_DIM - 1)) < (HEAD_DIM // 2)

    def seg(i):
        return _dot(xb, w_ref[:, i * GROUP_WIDTH:(i + 1) * GROUP_WIDTH])

    def rope(p):
        rot = jnp.where(first_half, pltpu.roll(p, GROUP_WIDTH - HEAD_DIM // 2, 1),
                        pltpu.roll(p, HEAD_DIM // 2, 1))
        return p * cos + rot * sin

    aq[...] = rope(seg(0)).astype(BF16)
    ak[...] = rope(seg(1)).astype(BF16)
    av[...] = seg(2).astype(BF16)
    rq[...] = rope(seg(3)).astype(BF16)
    rk[...] = (rope(seg(4)) * (HEAD_DIM ** -0.5)).astype(BF16)
    rv[...] = seg(5).astype(BF16)
    rg[...] = seg(6)
    cx[...] = seg(7)
    cg[...] = seg(8)
    dq[...] = rope(seg(9)).astype(BF16)
    dk[...] = rope(seg(10)).astype(BF16)
    dv[...] = seg(11).astype(BF16)
    dqi[:, 0:GROUP_WIDTH] = rope(seg(12)).astype(BF16)
    dqi[:, GROUP_WIDTH:2 * GROUP_WIDTH] = rope(seg(13)).astype(BF16)
    last = seg(14)
    dki[...] = rope(last)[:, 0:IDX_DIM].astype(BF16)
    dwt[...] = last[:, 128:256].T[0:IDX_HEADS, :]


def _proj(x2, w_pad, cos_t, sin_t, T):
    N = x2.shape[0]
    tm = PROJ_TM
    tpb = T // tm
    row = lambda w: pl.BlockSpec((tm, w), lambda i: (i, 0))
    tab = pl.BlockSpec((tm, GROUP_WIDTH), lambda i: (i % tpb, 0))
    widths = [256] * 12 + [512, IDX_DIM]
    dtypes = [BF16, BF16, BF16, BF16, BF16, BF16, F32, F32, F32, BF16, BF16, BF16, BF16, BF16]
    return pl.pallas_call(
        _proj_kernel,
        grid=(N // tm,),
        in_specs=[row(D_MODEL), pl.BlockSpec((D_MODEL, IN_PAD), lambda i: (0, 0)), tab, tab],
        out_specs=[row(w) for w in widths] + [pl.BlockSpec((IDX_HEADS, tm), lambda i: (0, i))],
        out_shape=[jax.ShapeDtypeStruct((N, w), d) for w, d in zip(widths, dtypes)]
        + [jax.ShapeDtypeStruct((IDX_HEADS, N), F32)],
        compiler_params=_cparams(1),
        name="proj_rope",
    )(x2, w_pad, cos_t, sin_t)


def _moba_kernel(q_ref, k_ref, v_ref, o_ref, kmean_ref, sel_ref, m_ref, l_ref, acc_ref, *, n_blocks):
    j = pl.program_id(1)
    R = Q_BLOCK
    SR = GROUP_HEADS * R
    KB = MOBA_BLOCK

    @pl.when(j == 0)
    def _():
        kmean_ref[...] = jnp.zeros_like(kmean_ref)
        for n in range(n_blocks):
            kb = k_ref[0, n * KB:(n + 1) * KB, :].astype(F32)
            kmean_ref[n:n + 1, :] = jnp.mean(kb, axis=0, keepdims=True)

    own = j // (KB // R)
    q_raw = _head_stack(q_ref[0])
    q_stack = (q_raw.astype(F32) * (HEAD_DIM ** -0.5)).astype(BF16)

    gate = _dot_nt(kmean_ref[...], q_raw.astype(F32), precision=lax.Precision.HIGHEST)
    blk = lax.broadcasted_iota(I32, gate.shape, 0)
    past = blk < own
    g = jnp.where(past, gate, -jnp.inf)
    sel = jnp.zeros(gate.shape, F32)
    for _ in range(MOBA_TOPK):
        mx = jnp.max(g, axis=0, keepdims=True)
        first = jnp.min(jnp.where(g == mx, blk, MOBA_MAX_BLOCKS), axis=0, keepdims=True)
        pick = blk == first
        sel = jnp.where(pick & past, 1.0, sel)
        g = jnp.where(pick, -jnp.inf, g)
    sel_ref[...] = sel

    m_ref[...] = jnp.full(m_ref.shape, M_FLOOR, F32)
    l_ref[...] = jnp.zeros(l_ref.shape, F32)
    acc_ref[...] = jnp.zeros(acc_ref.shape, F32)

    def attend(kb, vb, bias):
        s = _dot_nt(kb, q_stack) + bias
        m_old = m_ref[...]
        m_new = jnp.maximum(m_old, jnp.max(s, axis=0, keepdims=True))
        alpha = jnp.exp(m_old - m_new)
        p = jnp.exp(s - m_new)
        l_ref[...] = alpha * l_ref[...] + jnp.sum(p, axis=0, keepdims=True)
        acc_ref[...] = alpha * acc_ref[...] + _dot_tn(vb, p.astype(BF16))
        m_ref[...] = m_new

    keypos = own * KB + lax.broadcasted_iota(I32, (KB, SR), 0)
    qpos = j * R + (lax.broadcasted_iota(I32, (KB, SR), 1) & (R - 1))
    st = pl.multiple_of(own * KB, KB)
    attend(k_ref[0, pl.ds(st, KB), :], v_ref[0, pl.ds(st, KB), :], jnp.where(keypos <= qpos, 0.0, NEG))

    def body(n, c):
        s0 = pl.multiple_of(n * KB, KB)
        bias = jnp.where(sel_ref[pl.ds(n, 1), :] > 0.5, 0.0, NEG)
        attend(k_ref[0, pl.ds(s0, KB), :], v_ref[0, pl.ds(s0, KB), :], bias)
        return c

    lax.fori_loop(0, own, body, 0)
    o_ref[0] = _head_unstack((acc_ref[...] / l_ref[...]).T, R).astype(o_ref.dtype)


def _moba(q, k, v):
    B, T, _ = q.shape
    n_blocks = T // MOBA_BLOCK
    assert T % MOBA_BLOCK == 0 and n_blocks <= MOBA_MAX_BLOCKS
    SR = GROUP_HEADS * Q_BLOCK
    return pl.pallas_call(
        functools.partial(_moba_kernel, n_blocks=n_blocks),
        grid=(B, T // Q_BLOCK),
        in_specs=[pl.BlockSpec((1, Q_BLOCK, GROUP_WIDTH), lambda b, j: (b, j, 0)),
                  pl.BlockSpec((1, T, GROUP_WIDTH), lambda b, j: (b, 0, 0)),
                  pl.BlockSpec((1, T, GROUP_WIDTH), lambda b, j: (b, 0, 0))],
        out_specs=pl.BlockSpec((1, Q_BLOCK, GROUP_WIDTH), lambda b, j: (b, j, 0)),
        out_shape=jax.ShapeDtypeStruct((B, T, GROUP_WIDTH), BF16),
        scratch_shapes=[pltpu.VMEM((MOBA_MAX_BLOCKS, GROUP_WIDTH), F32),
                        pltpu.VMEM((MOBA_MAX_BLOCKS, SR), F32),
                        pltpu.VMEM((1, SR), F32), pltpu.VMEM((1, SR), F32),
                        pltpu.VMEM((GROUP_WIDTH, SR), F32)],
        compiler_params=_cparams(2),
        name="moba_attention",
    )(q, k, v)


def _ret_kernel(q_ref, k_ref, v_ref, g_ref, dmask_ref, xi_ref, zeta_ref, gdec_ref, bd_ref, avg_ref,
                gng_ref, gnb_ref, o_ref, r_ref):
    j = pl.program_id(1)

    @pl.when(j == 0)
    def _():
        r_ref[...] = jnp.zeros_like(r_ref)

    C = RET_CHUNK
    q = q_ref[0]
    k = k_ref[0]
    v = v_ref[0]
    q_stack = _head_stack(q)
    inner = _dot_nt(q_stack, k) * dmask_ref[...]
    o = _head_unstack(_dot(inner.astype(BF16), v), C)
    R = r_ref[...]
    o = o + _dot(q, R.astype(BF16)) * xi_ref[...]
    kz = (k.astype(F32) * zeta_ref[...]).astype(BF16)
    r_ref[...] = gdec_ref[...] * R + bd_ref[...] * _dot_tn(kz, v)

    hp = lax.Precision.HIGHEST
    mu = _dot(o, avg_ref[...], precision=hp)
    oc = o - mu
    var = _dot(oc * oc, avg_ref[...], precision=hp)
    y = oc * lax.rsqrt(var + LN_EPS) * gng_ref[...] + gnb_ref[...]
    gte = g_ref[0]
    o_ref[0] = (y * (gte * jax.nn.sigmoid(gte))).astype(o_ref.dtype)


def _ret_tables():
    H, C, d = GROUP_HEADS, RET_CHUNK, HEAD_DIM
    log_g = np.log(1.0 - 2.0 ** (-5.0 - np.arange(H, dtype=np.float64)))
    n = np.arange(C, dtype=np.float64)
    diff = n[:, None] - n[None, :]
    dmask = np.where(diff >= 0, np.exp(log_g[:, None, None] * np.maximum(diff, 0.0)), 0.0)
    xi = np.exp(log_g[:, None] * (n + 1.0))
    zeta = np.exp(log_g[:, None] * (C - 1.0 - n))
    g_chunk = np.exp(log_g * C)
    head = np.arange(GROUP_WIDTH) // d
    bd = (head[:, None] == head[None, :]).astype(np.float64)
    to32 = lambda a: jnp.asarray(a, dtype=F32)
    return dict(dmask=to32(dmask.reshape(H * C, C)), xi=to32(xi.T[:, head]), zeta=to32(zeta.T[:, head]),
                gdec=to32(bd * g_chunk[head][:, None]), bd=to32(bd), avg=to32(bd / d))


def _retention(rq, rk, rv, rg, gn_g, gn_b, tabs):
    B, T, _ = rq.shape
    C = RET_CHUNK
    blk = pl.BlockSpec((1, C, GROUP_WIDTH), lambda b, j: (b, j, 0))
    const = lambda a: pl.BlockSpec(a.shape, lambda b, j: (0,) * a.ndim)
    consts = [tabs["dmask"], tabs["xi"], tabs["zeta"], tabs["gdec"], tabs["bd"], tabs["avg"], gn_g, gn_b]
    return pl.pallas_call(
        _ret_kernel,
        grid=(B, T // C),
        in_specs=[blk, blk, blk, blk] + [const(a) for a in consts],
        out_specs=blk,
        out_shape=jax.ShapeDtypeStruct((B, T, GROUP_WIDTH), BF16),
        scratch_shapes=[pltpu.VMEM((GROUP_WIDTH, GROUP_WIDTH), F32)],
        compiler_params=_cparams(2),
        name="retention",
    )(rq, rk, rv, rg, *consts)


def _rglru_kernel(x_ref, g_ref, cw_ref, cb_ref, wx_ref, bx_ref, wa_ref, ba_ref, lam_ref, o_ref,
                  xbuf, h_ref):
    j = pl.program_id(1)
    tc = RG_TC

    @pl.when(j == 0)
    def _():
        xbuf[0:8, :] = jnp.zeros((8, GROUP_WIDTH), F32)
        h_ref[...] = jnp.zeros_like(h_ref)

    xbuf[8:8 + tc, :] = x_ref[0]
    xc = cb_ref[...] + cw_ref[RG_CONV - 1:RG_CONV, :] * xbuf[8:8 + tc, :]
    for i in range(RG_CONV - 1):
        off = 8 - (RG_CONV - 1) + i
        xc = xc + cw_ref[i:i + 1, :] * xbuf[off:off + tc, :]
    xbuf[0:8, :] = xbuf[tc:tc + 8, :]

    xcb = xc.astype(BF16)
    gate_x = jax.nn.sigmoid(_dot(xcb, wx_ref[...]) + bx_ref[...])
    gate_a = jax.nn.sigmoid(_dot(xcb, wa_ref[...]) + ba_ref[...])
    lam = lam_ref[...]
    softplus_neg = jnp.maximum(-lam, 0.0) + jnp.log1p(jnp.exp(-jnp.abs(lam)))
    log_a = -RG_C * gate_a * softplus_neg
    a = jnp.exp(log_a)
    th = jnp.tanh(log_a)
    b = jnp.sqrt(-2.0 * th / (1.0 - th)) * (gate_x * xc)

    row = lax.broadcasted_iota(I32, (tc, GROUP_WIDTH), 0)
    d = 1
    while d < tc:
        keep = row >= d
        a_sh = jnp.where(keep, pltpu.roll(a, d, 0), 1.0)
        b_sh = jnp.where(keep, pltpu.roll(b, d, 0), 0.0)
        b = a * b_sh + b
        a = a * a_sh
        d *= 2
    h = b + a * h_ref[...]
    h_ref[...] = h[tc - 1:tc, :]

    xg = g_ref[0]
    gelu = 0.5 * xg * (1.0 + jnp.tanh(np.sqrt(2.0 / np.pi) * (xg + 0.044715 * xg * xg * xg)))
    o_ref[0] = (h * gelu).astype(o_ref.dtype)


def _block_diag(w):
    n, c, _ = w.shape
    eye = jnp.eye(n, dtype=w.dtype)
    return (eye[:, None, :, None] * w[:, :, None, :]).reshape(n * c, n * c)


def _rglru(cx, cg, conv_w, conv_b, wx, bx, wa, ba, lam):
    B, T, _ = cx.shape
    tc = RG_TC
    blk = pl.BlockSpec((1, tc, GROUP_WIDTH), lambda b, j: (b, j, 0))
    const = lambda a: pl.BlockSpec(a.shape, lambda b, j: (0,) * a.ndim)
    consts = [conv_w, conv_b, wx, bx, wa, ba, lam]
    return pl.pallas_call(
        _rglru_kernel,
        grid=(B, T // tc),
        in_specs=[blk, blk] + [const(a) for a in consts],
        out_specs=blk,
        out_shape=jax.ShapeDtypeStruct((B, T, GROUP_WIDTH), BF16),
        scratch_shapes=[pltpu.VMEM((tc + 8, GROUP_WIDTH), F32), pltpu.VMEM((1, GROUP_WIDTH), F32)],
        compiler_params=_cparams(2),
        name="rg_lru",
    )(cx, cg, *consts)


def _dsa_kernel(q_ref, k_ref, v_ref, qi_ref, ki_ref, wt_ref, o_ref,
                sc_ref, jcut_ref, m_ref, l_ref, acc_ref, *, n_sel, n_keys):
    j = pl.program_id(1)
    R = Q_BLOCK
    KB = DSA_KEY_BLOCK
    SR = GROUP_HEADS * R
    nkb = (j * R + R + KB - 1) // KB
    nsel = float(n_sel)

    rowk = lax.broadcasted_iota(I32, (KB, R), 0)
    qpos = j * R + lax.broadcasted_iota(I32, (KB, R), 1)

    qi = qi_ref[0]
    qi_stack = jnp.concatenate([qi[:, h * IDX_DIM:(h + 1) * IDX_DIM] for h in range(IDX_HEADS)], axis=0)
    wt = wt_ref[...]

    def score_body(kb, c):
        st = pl.multiple_of(kb * KB, KB)
        rel = _dot_nt(ki_ref[0, pl.ds(st, KB), :], qi_stack)
        sc = wt[0:1, :] * jnp.maximum(rel[:, 0:R], 0.0)
        for h in range(1, IDX_HEADS):
            sc = sc + wt[h:h + 1, :] * jnp.maximum(rel[:, h * R:(h + 1) * R], 0.0)
        sc_ref[pl.ds(st, KB), :] = jnp.where(st + rowk <= qpos, sc * IDX_SCALE, NEG)
        return c

    lax.fori_loop(0, nkb, score_body, 0)

    def blocks(fn, init):
        def body(kb, c):
            st = pl.multiple_of(kb * KB, KB)
            return fn(st, sc_ref[pl.ds(st, KB), :], c)
        return lax.fori_loop(0, nkb, body, init)

    def minmax(st, s, c):
        mn, mx = c
        return (jnp.minimum(mn, _fold_rows(jnp.where(s > 0.5 * NEG, s, BIG), jnp.min)),
                jnp.maximum(mx, _fold_rows(s, jnp.max)))

    mn8, mx8 = blocks(minmax, (jnp.full((FOLD_ROWS, R), BIG, F32), jnp.full((FOLD_ROWS, R), NEG, F32)))
    mn = jnp.min(mn8, axis=0, keepdims=True)
    mx = jnp.max(mx8, axis=0, keepdims=True)

    def count_ge(th):
        acc = blocks(lambda st, s, c: c + _fold_rows(jnp.where(s >= th, 1.0, 0.0), jnp.sum),
                     jnp.zeros((FOLD_ROWS, R), F32))
        return jnp.sum(acc, axis=0, keepdims=True)

    n_adm = (j * R + 1 + lax.broadcasted_iota(I32, (1, R), 1)).astype(F32)
    need = n_adm > nsel
    c_max = count_ge(mx)
    top_tie = need & (c_max >= nsel)
    lo0 = jnp.where(need, jnp.where(top_tie, mx, mn), 0.5 * NEG)
    cgt0 = jnp.where(top_tie, 0.0, c_max)
    act0 = jnp.where(need & jnp.logical_not(top_tie), 1.0, 0.0)
    tie0 = jnp.where(top_tie, 1.0, 0.0)

    def to_key(f):
        b = lax.bitcast_convert_type(f, I32)
        return b ^ (lax.shift_right_arithmetic(b, 31) & 0x7FFFFFFF)

    def from_key(kk):
        return lax.bitcast_convert_type(kk ^ (lax.shift_right_arithmetic(kk, 31) & 0x7FFFFFFF), F32)

    def bis_step(klo, khi, cgt, act, tie):
        on = act > 0.0
        kmid = lax.shift_right_arithmetic(klo, 1) + lax.shift_right_arithmetic(khi, 1) + (klo & khi & 1)
        stuck = kmid == klo
        cnt = count_ge(from_key(kmid))
        go = on & jnp.logical_not(stuck)
        up = go & (cnt >= nsel)
        dn = go & (cnt < nsel)
        return (jnp.where(up, kmid, klo), jnp.where(dn, kmid, khi), jnp.where(dn, cnt, cgt),
                jnp.where(go & (cnt != nsel), 1.0, 0.0), jnp.where(on & stuck, 1.0, tie))

    def bis_cond(c):
        return (c[1] > 0.0) & (c[0] < 40)

    def bis_body(c):
        st = bis_step(*bis_step(*c[2:]))
        return (c[0] + 2, jnp.max(st[3])) + st

    res = lax.while_loop(bis_cond, bis_body,
                         (jnp.int32(0), jnp.max(act0), to_key(lo0), to_key(mx), cgt0, act0, tie0))
    lo, cgt, tie = from_key(res[2]), res[4], res[6]

    jcut_ref[...] = jnp.full((1, R), float(n_keys), F32)

    @pl.when(jnp.max(tie) > 0.0)
    def _():
        want = nsel - cgt
        tied = tie > 0.0

        def jb(it, c):
            a, b = c
            mid = jnp.floor((a + b) * 0.5)
            hit8 = blocks(lambda st, s, cc: cc + _fold_rows(
                jnp.where((s == lo) & ((st + rowk).astype(F32) <= mid), 1.0, 0.0), jnp.sum),
                jnp.zeros((FOLD_ROWS, R), F32))
            ok = jnp.sum(hit8, axis=0, keepdims=True) >= want
            return jnp.where(ok, a, mid), jnp.where(ok, mid, b)

        _, b = lax.fori_loop(0, int(np.ceil(np.log2(n_keys))) + 1, jb,
                             (jnp.full((1, R), -1.0, F32), jnp.full((1, R), float(n_keys - 1), F32)))
        jcut_ref[...] = jnp.where(tied, b, float(n_keys))

    q_stack = _head_stack((q_ref[0].astype(F32) * (HEAD_DIM ** -0.5)).astype(BF16))
    m_ref[...] = jnp.full(m_ref.shape, M_FLOOR, F32)
    l_ref[...] = jnp.zeros(l_ref.shape, F32)
    acc_ref[...] = jnp.zeros(acc_ref.shape, F32)
    jcut = jcut_ref[...]

    def att_body(kb, c):
        st = pl.multiple_of(kb * KB, KB)
        sc = sc_ref[pl.ds(st, KB), :]
        keep = (sc > lo) | ((sc == lo) & ((st + rowk).astype(F32) <= jcut))
        bias = jnp.where(keep, 0.0, NEG)
        s = _dot_nt(k_ref[0, pl.ds(st, KB), :], q_stack) + jnp.concatenate([bias] * GROUP_HEADS, axis=1)
        m_old = m_ref[...]
        m_new = jnp.maximum(m_old, jnp.max(s, axis=0, keepdims=True))
        alpha = jnp.exp(m_old - m_new)
        p = jnp.exp(s - m_new)
        l_ref[...] = alpha * l_ref[...] + jnp.sum(p, axis=0, keepdims=True)
        acc_ref[...] = alpha * acc_ref[...] + _dot_tn(v_ref[0, pl.ds(st, KB), :], p.astype(BF16))
        m_ref[...] = m_new
        return c

    lax.fori_loop(0, nkb, att_body, 0)
    o_ref[0] = _head_unstack((acc_ref[...] / l_ref[...]).T, R).astype(o_ref.dtype)


def _dsa(q, k, v, qi, ki, wt):
    B, T, _ = q.shape
    KB = DSA_KEY_BLOCK
    assert T % KB == 0
    n_sel = min(DSA_TOPK, T // 4)
    SR = GROUP_HEADS * Q_BLOCK
    nq = T // Q_BLOCK
    qblk = lambda wd: pl.BlockSpec((1, Q_BLOCK, wd), lambda b, j: (b, j, 0))
    full = lambda wd: pl.BlockSpec((1, T, wd), lambda b, j: (b, 0, 0))
    return pl.pallas_call(
        functools.partial(_dsa_kernel, n_sel=n_sel, n_keys=T),
        grid=(B, nq),
        in_specs=[qblk(GROUP_WIDTH), full(GROUP_WIDTH), full(GROUP_WIDTH),
                  qblk(IDX_HEADS * IDX_DIM), full(IDX_DIM),
                  pl.BlockSpec((IDX_HEADS, Q_BLOCK), lambda b, j: (0, b * nq + j))],
        out_specs=qblk(GROUP_WIDTH),
        out_shape=jax.ShapeDtypeStruct((B, T, GROUP_WIDTH), BF16),
        scratch_shapes=[pltpu.VMEM((T, Q_BLOCK), F32), pltpu.VMEM((1, Q_BLOCK), F32),
                        pltpu.VMEM((1, SR), F32), pltpu.VMEM((1, SR), F32),
                        pltpu.VMEM((GROUP_WIDTH, SR), F32)],
        compiler_params=_cparams(2),
        name="dsa_attention",
    )(q, k, v, qi, ki, wt)


def _outproj_kernel(oa, orr, oc, od, x_ref, w_ref, g_ref, b_ref, rw_ref, rb_ref, ltri_ref,
                    x1_ref, ti_ref, tg_ref, rk_ref, cnt_ref, run_ref):
    GW = GROUP_WIDTH

    @pl.when(pl.program_id(0) == 0)
    def _():
        run_ref[...] = jnp.zeros_like(run_ref)

    acc = _dot(oa[...], w_ref[0:GW, :])
    acc = acc + _dot(orr[...], w_ref[GW:2 * GW, :])
    acc = acc + _dot(oc[...], w_ref[2 * GW:3 * GW, :])
    acc = acc + _dot(od[...], w_ref[3 * GW:4 * GW, :])
    x1 = _layer_norm_rows(ALPHA * x_ref[...] + acc, g_ref[...], b_ref[...])
    x1_ref[...] = x1

    logits = _dot(x1.astype(BF16), rw_ref[...]) + rb_ref[...]
    col = lax.broadcasted_iota(I32, logits.shape, 1)
    kcol = lax.broadcasted_iota(I32, ti_ref.shape, 1)
    g = logits
    ti = jnp.zeros(ti_ref.shape, I32)
    tv = jnp.zeros(tg_ref.shape, F32)
    picks = []
    for kk in range(TOP_K):
        mx = jnp.max(g, axis=1, keepdims=True)
        first = jnp.min(jnp.where(g == mx, col, N_EXPERTS), axis=1, keepdims=True)
        ti = jnp.where(kcol == kk, first, ti)
        tv = jnp.where(kcol == kk, mx, tv)
        picks.append(col == first)
        g = jnp.where(picks[-1], -jnp.inf, g)
    e = jnp.exp(tv - jnp.max(tv, axis=1, keepdims=True))
    ti_ref[...] = ti
    tg_ref[...] = e / jnp.sum(e, axis=1, keepdims=True)

    sel = jnp.where(picks[0] | picks[1] | picks[2] | picks[3], 1.0, 0.0)
    before = run_ref[...] + _dot(ltri_ref[...], sel.astype(BF16))
    rk = jnp.zeros(rk_ref.shape, F32)
    for kk in range(TOP_K):
        rk = jnp.where(kcol == kk, jnp.sum(jnp.where(picks[kk], before, 0.0), axis=1, keepdims=True), rk)
    rk_ref[...] = rk.astype(I32)
    run_ref[...] = run_ref[...] + jnp.sum(sel, axis=0, keepdims=True)
    cnt_ref[...] = run_ref[...]


def _outproj(oa, orr, oc, od, x2, w_out, g, b, rw, rb):
    N = x2.shape[0]
    tm = OUT_TM
    row = lambda w: pl.BlockSpec((tm, w), lambda i: (i, 0))
    const = lambda a: pl.BlockSpec(a.shape, lambda i: (0,) * a.ndim)
    ltri = jnp.asarray(np.tril(np.ones((tm, tm), np.float32), -1), dtype=BF16)
    return pl.pallas_call(
        _outproj_kernel,
        grid=(N // tm,),
        in_specs=[row(GROUP_WIDTH)] * 4 + [row(D_MODEL), const(w_out), const(g), const(b), const(rw), const(rb),
                                           const(ltri)],
        out_specs=[row(D_MODEL), row(TOP_K), row(TOP_K), row(TOP_K),
                   pl.BlockSpec((1, N_EXPERTS), lambda i: (0, 0))],
        out_shape=[jax.ShapeDtypeStruct((N, D_MODEL), F32), jax.ShapeDtypeStruct((N, TOP_K), I32),
                   jax.ShapeDtypeStruct((N, TOP_K), F32), jax.ShapeDtypeStruct((N, TOP_K), I32),
                   jax.ShapeDtypeStruct((1, N_EXPERTS), F32)],
        scratch_shapes=[pltpu.VMEM((1, N_EXPERTS), F32)],
        compiler_params=_cparams(1),
        name="outproj_ln_router",
    )(oa, orr, oc, od, x2, w_out, g, b, rw, rb, ltri)


def _dispatch_kernel(tv_ref, pos_ref, x_ref, xs_hbm, xbuf, sem, zsem, *, n_tiles):
    i = pl.program_id(0)
    n = pl.num_programs(0)
    tm = DSP_TM
    par = i % 2

    @pl.when(i == 0)
    def _():
        xbuf[1] = jnp.zeros((tm, D_MODEL), F32)

        def fill(t, c):
            @pl.when(tv_ref[t] < MOE_TM)
            def _():
                pltpu.make_async_copy(xbuf.at[1], xs_hbm.at[pl.ds(pl.multiple_of(t * MOE_TM, MOE_TM), MOE_TM)],
                                      zsem).start()
            return c

        def drain(t, c):
            @pl.when(tv_ref[t] < MOE_TM)
            def _():
                pltpu.make_async_copy(xbuf.at[1], xs_hbm.at[pl.ds(0, MOE_TM)], zsem).wait()
            return c

        lax.fori_loop(0, n_tiles, fill, 0)
        lax.fori_loop(0, n_tiles, drain, 0)

    def wait_step(p):
        for _ in range(TOP_K):
            pltpu.make_async_copy(xbuf.at[p], xbuf.at[p], sem.at[p]).wait()

    @pl.when(i >= 2)
    def _():
        wait_step(par)

    xbuf[par] = x_ref[...]

    def body(r, c):
        for kk in range(TOP_K):
            pltpu.make_async_copy(xbuf.at[par, pl.ds(r, 1)], xs_hbm.at[pl.ds(pos_ref[r * TOP_K + kk], 1)],
                                  sem.at[par]).start()
        return c

    lax.fori_loop(0, tm, body, 0, unroll=2)

    @pl.when(i == n - 1)
    def _():
        wait_step(1 - par)
        wait_step(par)


def _dispatch(pos, x1, tile_valid):
    N = x1.shape[0]
    tm = DSP_TM
    n_tiles = tile_valid.shape[0]
    assert N // tm >= 2 and tm == MOE_TM
    grid_spec = pltpu.PrefetchScalarGridSpec(
        num_scalar_prefetch=1,
        grid=(N // tm,),
        in_specs=[pl.BlockSpec((tm * TOP_K,), lambda i, tv: (i,), memory_space=pltpu.SMEM),
                  pl.BlockSpec((tm, D_MODEL), lambda i, tv: (i, 0))],
        out_specs=pl.BlockSpec(memory_space=pl.ANY),
        scratch_shapes=[pltpu.VMEM((2, tm, D_MODEL), F32), pltpu.SemaphoreType.DMA((2,)),
                        pltpu.SemaphoreType.DMA(())],
    )
    return pl.pallas_call(
        functools.partial(_dispatch_kernel, n_tiles=n_tiles),
        grid_spec=grid_spec,
        out_shape=jax.ShapeDtypeStruct((n_tiles * MOE_TM, D_MODEL), F32),
        compiler_params=_cparams(1),
        name="moe_dispatch",
    )(tile_valid, pos, x1)


def _moe_kernel(te_ref, nv_ref, x_ref, w1_ref, b1_ref, w2_ref, b2_ref, y_ref, w1b, w2b):
    i = pl.program_id(0)
    tm = MOE_TM
    n_valid = nv_ref[i]

    @pl.when((i == 0) | (te_ref[i] != te_ref[jnp.maximum(i - 1, 0)]))
    def _():
        step = 128
        for c in range(D_MODEL // step):
            w1b[c * step:(c + 1) * step, :] = w1_ref[0, 0, c * step:(c + 1) * step, :].astype(BF16)
        for c in range(D_FF // step):
            w2b[c * step:(c + 1) * step, :] = w2_ref[0, 0, c * step:(c + 1) * step, :].astype(BF16)

    @pl.when(n_valid > 0)
    def _():
        h = _dot(x_ref[...].astype(BF16), w1b[...]) + b1_ref[0]
        glu_in = jnp.minimum(h[:, :D_FF], SWIGLU_LIMIT)
        up = jnp.clip(h[:, D_FF:], -SWIGLU_LIMIT, SWIGLU_LIMIT)
        glu = glu_in * jax.nn.sigmoid(SWIGLU_ALPHA * glu_in)
        y_ref[...] = _dot(((up + 1.0) * glu).astype(BF16), w2b[...]) + b2_ref[0]

    @pl.when(n_valid == 0)
    def _():
        y_ref[...] = jnp.zeros_like(y_ref)


def _moe_experts(xs, tile_expert, tile_valid, w1, b1, w2, b2, layer):
    tm = MOE_TM
    n_tiles = tile_expert.shape[0]
    grid_spec = pltpu.PrefetchScalarGridSpec(
        num_scalar_prefetch=2,
        grid=(n_tiles,),
        in_specs=[
            pl.BlockSpec((tm, D_MODEL), lambda i, te, nv: (i, 0)),
            pl.BlockSpec((1, 1, D_MODEL, 2 * D_FF), lambda i, te, nv: (layer, te[i], 0, 0)),
            pl.BlockSpec((1, 1, 2 * D_FF), lambda i, te, nv: (te[i], 0, 0)),
            pl.BlockSpec((1, 1, D_FF, D_MODEL), lambda i, te, nv: (layer, te[i], 0, 0)),
            pl.BlockSpec((1, 1, D_MODEL), lambda i, te, nv: (te[i], 0, 0)),
        ],
        out_specs=pl.BlockSpec((tm, D_MODEL), lambda i, te, nv: (i, 0)),
        scratch_shapes=[pltpu.VMEM((D_MODEL, 2 * D_FF), BF16), pltpu.VMEM((D_FF, D_MODEL), BF16)],
    )
    return pl.pallas_call(
        _moe_kernel,
        grid_spec=grid_spec,
        out_shape=jax.ShapeDtypeStruct((n_tiles * tm, D_MODEL), F32),
        compiler_params=_cparams(1),
        name="moe_experts",
    )(tile_expert, tile_valid, xs, w1, b1, w2, b2)


def _combine_kernel(pos_ref, posn_ref, y_hbm, x1_ref, tg_ref, g_ref, b_ref, o_ref, ybuf, sem):
    i = pl.program_id(0)
    n = pl.num_programs(0)
    tm = CMB_TM
    slot = i % 2

    def issue(idx_ref, s):
        def body(r, c):
            for kk in range(TOP_K):
                pltpu.make_async_copy(y_hbm.at[pl.ds(idx_ref[r * TOP_K + kk], 1)],
                                      ybuf.at[s, kk, pl.ds(r, 1)], sem.at[s]).start()
            return c
        lax.fori_loop(0, tm, body, 0, unroll=2)

    @pl.when(i == 0)
    def _():
        issue(pos_ref, 0)

    @pl.when(i + 1 < n)
    def _():
        issue(posn_ref, 1 - slot)

    pltpu.make_async_copy(ybuf.at[slot], ybuf.at[slot], sem.at[slot]).wait()
    tg = tg_ref[...]
    moe = tg[:, 0:1] * ybuf[slot, 0]
    for kk in range(1, TOP_K):
        moe = moe + tg[:, kk:kk + 1] * ybuf[slot, kk]
    o_ref[...] = _layer_norm_rows(ALPHA * x1_ref[...] + moe, g_ref[...], b_ref[...])


def _combine(pos, y_sorted, x1, tg, g, b):
    N = x1.shape[0]
    tm = CMB_TM
    n = N // tm
    row = lambda w: pl.BlockSpec((tm, w), lambda i: (i, 0))
    const = lambda a: pl.BlockSpec(a.shape, lambda i: (0,) * a.ndim)
    return pl.pallas_call(
        _combine_kernel,
        grid=(n,),
        in_specs=[pl.BlockSpec((tm * TOP_K,), lambda i: (i,), memory_space=pltpu.SMEM),
                  pl.BlockSpec((tm * TOP_K,), lambda i: (jnp.minimum(i + 1, n - 1),), memory_space=pltpu.SMEM),
                  pl.BlockSpec(memory_space=pl.ANY), row(D_MODEL), row(TOP_K), const(g), const(b)],
        out_specs=row(D_MODEL),
        out_shape=jax.ShapeDtypeStruct((N, D_MODEL), F32),
        scratch_shapes=[pltpu.VMEM((2, TOP_K, tm, D_MODEL), F32), pltpu.SemaphoreType.DMA((2,))],
        compiler_params=_cparams(1),
        name="moe_combine_ln",
    )(pos, pos, y_sorted, x1, tg, g, b)


def _routing_tables(top_i, rank, counts_f, n_tiles):
    tm = MOE_TM
    counts = counts_f.reshape(-1).astype(I32)
    padded = ((counts + tm - 1) // tm) * tm
    ends = jnp.cumsum(padded)
    offsets = ends - padded
    onehot = top_i[:, :, None] == jnp.arange(N_EXPERTS, dtype=I32)[None, None, :]
    pos = jnp.sum(jnp.where(onehot, offsets[None, None, :], 0), axis=-1) + rank
    tile_start = jnp.arange(n_tiles, dtype=I32) * tm
    tile_expert = jnp.sum((ends[None, :] <= tile_start[:, None]).astype(I32), axis=1)
    tile_expert = jnp.minimum(tile_expert, N_EXPERTS - 1)
    n_used = ends[-1] // tm
    last_expert = tile_expert[jnp.maximum(n_used - 1, 0)]
    tile_expert = jnp.where(tile_start < ends[-1], tile_expert, last_expert)
    valid_end = (offsets + counts)[tile_expert]
    tile_valid = jnp.clip(valid_end - tile_start, 0, tm)
    return pos.reshape(-1).astype(I32), tile_expert.astype(I32), tile_valid.astype(I32)


def _rope_tables(T):
    inv = ROPE_THETA ** (-jnp.arange(0, HEAD_DIM, 2, dtype=F32) / HEAD_DIM)
    ang = jnp.arange(T, dtype=F32)[:, None] * inv[None, :]
    cos, sin = jnp.cos(ang), jnp.sin(ang)
    cos_t = jnp.tile(jnp.concatenate([cos, cos], axis=-1), (1, GROUP_HEADS))
    sin_t = jnp.tile(jnp.concatenate([-sin, sin], axis=-1), (1, GROUP_HEADS))
    return cos_t, sin_t


def _pad_w_in(w_in):
    base = 12 * GROUP_WIDTH + IDX_HEADS * IDX_DIM
    w = jnp.zeros((D_MODEL, IN_PAD), F32)
    w = w.at[:, :base + IDX_DIM].set(w_in[:, :base + IDX_DIM])
    w = w.at[:, base + 128:base + 128 + IDX_HEADS].set(w_in[:, base + IDX_DIM:])
    return w.astype(BF16)


def _layer(x2, B, T, cos_t, sin_t, tabs, w_in, ret_gn_g, ret_gn_b, conv_w, conv_b, rg_wx, rg_bx, rg_wa,
           rg_ba, rg_lambda, w_out, ln1_g, ln1_b, router_w, router_b, exp_w1, exp_b1, exp_w2, exp_b2,
           ln2_g, ln2_b, layer):
    N = B * T
    r2 = lambda a: a.reshape(1, -1)
    (aq, ak, av, rq, rk, rv, rg, cx, cg, dq, dk, dv, dqi, dki, dwt) = _proj(x2, _pad_w_in(w_in), cos_t, sin_t, T)
    seq = lambda a: a.reshape(B, T, a.shape[-1])
    o_a = _moba(seq(aq), seq(ak), seq(av))
    o_r = _retention(seq(rq), seq(rk), seq(rv), seq(rg), r2(ret_gn_g), r2(ret_gn_b), tabs)
    o_c = _rglru(seq(cx), seq(cg), conv_w, r2(conv_b), _block_diag(rg_wx).astype(BF16), r2(rg_bx),
                 _block_diag(rg_wa).astype(BF16), r2(rg_ba), r2(rg_lambda))
    o_d = _dsa(seq(dq), seq(dk), seq(dv), seq(dqi), seq(dki), dwt)
    flat = lambda a: a.reshape(N, GROUP_WIDTH)
    x1, top_i, top_g, rank, counts = _outproj(flat(o_a), flat(o_r), flat(o_c), flat(o_d), x2,
                                              w_out.astype(BF16), r2(ln1_g), r2(ln1_b),
                                              router_w.astype(BF16), r2(router_b))
    n_tiles = (N * TOP_K) // MOE_TM + N_EXPERTS
    pos, tile_expert, tile_valid = _routing_tables(top_i, rank, counts, n_tiles)
    xs = _dispatch(pos, x1, tile_valid)
    y_sorted = _moe_experts(xs, tile_expert, tile_valid, exp_w1, exp_b1.reshape(N_EXPERTS, 1, -1), exp_w2,
                            exp_b2.reshape(N_EXPERTS, 1, -1), layer)
    return _combine(pos, y_sorted, x1, top_g, r2(ln2_g), r2(ln2_b))


def kernel(x, w_in, ret_gn_g, ret_gn_b, conv_w, conv_b, rg_wx, rg_bx, rg_wa, rg_ba, rg_lambda, w_out,
           ln1_g, ln1_b, router_w, router_b, exp_w1, exp_b1, exp_w2, exp_b2, ln2_g, ln2_b):
    B, T, D = x.shape
    cos_t, sin_t = _rope_tables(T)
    tabs = _ret_tables()
    x2 = x.reshape(B * T, D)
    for l in range(w_in.shape[0]):
        x2 = _layer(x2, B, T, cos_t, sin_t, tabs, w_in[l], ret_gn_g[l], ret_gn_b[l], conv_w[l], conv_b[l],
                    rg_wx[l], rg_bx[l], rg_wa[l], rg_ba[l], rg_lambda[l], w_out[l], ln1_g[l], ln1_b[l],
                    router_w[l], router_b[l], exp_w1, exp_b1[l], exp_w2, exp_b2[l], ln2_g[l], ln2_b[l], l)
    return x2.reshape(B, T, D)
```

```python
import functools

import numpy as np
import jax
import jax.numpy as jnp
from jax import lax
from jax.experimental import pallas as pl
from jax.experimental.pallas import tpu as pltpu

F32 = jnp.float32
BF16 = jnp.bfloat16
I32 = jnp.int32

D_MODEL = 1024
DEPTH = 2
HEAD_DIM = 64
GROUP_WIDTH = 256
GROUP_HEADS = 4
ROPE_THETA = 10000.0
Q_BLOCK = 128
MOBA_BLOCK = 256
MOBA_TOPK = 3
MOBA_MAX_BLOCKS = 16
RET_CHUNK = 128
RG_CONV = 4
RG_C = 8.0
IDX_HEADS = 8
IDX_DIM = 64
IDX_SCALE = (IDX_HEADS ** -0.5) * (IDX_DIM ** -0.5)
DSA_TOPK = 256
DSA_KEY_BLOCK = 512
N_EXPERTS = 32
TOP_K = 4
D_FF = 1024
SWIGLU_LIMIT = 7.0
SWIGLU_ALPHA = 1.702
ALPHA = (2 * DEPTH) ** 0.25
LN_EPS = 1e-5
IN_WIDTH = 12 * GROUP_WIDTH + IDX_HEADS * IDX_DIM + IDX_DIM + IDX_HEADS
IN_PAD = 15 * GROUP_WIDTH

NEG = -1e30
M_FLOOR = -1e29
BIG = 1e30
VMEM_LIMIT = 56 * 1024 * 1024
FOLD_ROWS = 64

PROJ_TM = 256
OUT_TM = 512
RG_TC = 256
MOE_TM = 512
DSP_TM = 512
CMB_TM = 256


def _cparams(ndims):
    return pltpu.CompilerParams(dimension_semantics=("arbitrary",) * ndims,
                                vmem_limit_bytes=VMEM_LIMIT)


def _dot(a, b, precision=None):
    return jnp.dot(a, b, preferred_element_type=F32, precision=precision)


def _dot_nt(a, b, precision=None):
    return lax.dot_general(a, b, (((1,), (1,)), ((), ())), preferred_element_type=F32,
                           precision=precision)


def _dot_tn(a, b):
    return lax.dot_general(a, b, (((0,), (0,)), ((), ())), preferred_element_type=F32)


def _head_stack(q):
    head = lax.shift_right_logical(lax.broadcasted_iota(I32, q.shape, 1), 6)
    qf = q.astype(F32)
    return jnp.concatenate([jnp.where(head == h, qf, 0.0) for h in range(GROUP_HEADS)],
                           axis=0).astype(q.dtype)


def _head_unstack(s, rows):
    head = lax.shift_right_logical(lax.broadcasted_iota(I32, (rows, GROUP_WIDTH), 1), 6)
    out = jnp.zeros((rows, GROUP_WIDTH), F32)
    for h in range(GROUP_HEADS):
        out = out + jnp.where(head == h, s[h * rows:(h + 1) * rows], 0.0)
    return out


def _fold_rows(x, op):
    return op(x.reshape(x.shape[0] // FOLD_ROWS, FOLD_ROWS, x.shape[1]), axis=0)


def _layer_norm_rows(y, g, b):
    mu = jnp.mean(y, axis=-1, keepdims=True)
    yc = y - mu
    var = jnp.mean(yc * yc, axis=-1, keepdims=True)
    return yc * lax.rsqrt(var + LN_EPS) * g + b


def _proj_kernel(x_ref, w_ref, cos_ref, sin_ref,
                 aq, ak, av, rq, rk, rv, rg, cx, cg, dq, dk, dv, dqi, dki, dwt):
    xb = x_ref[...].astype(BF16)
    cos = cos_ref[...]
    sin = sin_ref[...]
    first_half = (lax.broadcasted_iota(I32, cos.shape, 1) & (HEAD_DIM - 1)) < (HEAD_DIM // 2)

    def seg(i):
        return _dot(xb, w_ref[:, i * GROUP_WIDTH:(i + 1) * GROUP_WIDTH])

    def rope(p):
        rot = jnp.where(first_half, pltpu.roll(p, GROUP_WIDTH - HEAD_DIM // 2, 1),
                        pltpu.roll(p, HEAD_DIM // 2, 1))
        return p * cos + rot * sin

    aq[...] = rope(seg(0)).astype(BF16)
    ak[...] = rope(seg(1)).astype(BF16)
    av[...] = seg(2).astype(BF16)
    rq[...] = rope(seg(3)).astype(BF16)
    rk[...] = (rope(seg(4)) * (HEAD_DIM ** -0.5)).astype(BF16)
    rv[...] = seg(5).astype(BF16)
    rg[...] = seg(6)
    cx[...] = seg(7)
    cg[...] = seg(8)
    dq[...] = rope(seg(9)).astype(BF16)
    dk[...] = rope(seg(10)).astype(BF16)
    dv[...] = seg(11).astype(BF16)
    dqi[:, 0:GROUP_WIDTH] = rope(seg(12)).astype(BF16)
    dqi[:, GROUP_WIDTH:2 * GROUP_WIDTH] = rope(seg(13)).astype(BF16)
    last = seg(14)
    dki[...] = rope(last)[:, 0:IDX_DIM].astype(BF16)
    dwt[...] = last[:, 128:256].T[0:IDX_HEADS, :]


def _proj(x2, w_pad, cos_t, sin_t, T):
    N = x2.shape[0]
    tm = PROJ_TM
    tpb = T // tm
    row = lambda w: pl.BlockSpec((tm, w), lambda i: (i, 0))
    tab = pl.BlockSpec((tm, GROUP_WIDTH), lambda i: (i % tpb, 0))
    widths = [256] * 12 + [512, IDX_DIM]
    dtypes = [BF16, BF16, BF16, BF16, BF16, BF16, F32, F32, F32, BF16, BF16, BF16, BF16, BF16]
    return pl.pallas_call(
        _proj_kernel,
        grid=(N // tm,),
        in_specs=[row(D_MODEL), pl.BlockSpec((D_MODEL, IN_PAD), lambda i: (0, 0)), tab, tab],
        out_specs=[row(w) for w in widths] + [pl.BlockSpec((IDX_HEADS, tm), lambda i: (0, i))],
        out_shape=[jax.ShapeDtypeStruct((N, w), d) for w, d in zip(widths, dtypes)]
        + [jax.ShapeDtypeStruct((IDX_HEADS, N), F32)],
        compiler_params=_cparams(1),
        name="proj_rope",
    )(x2, w_pad, cos_t, sin_t)


def _moba_kernel(q_ref, k_ref, v_ref, o_ref, kmean_ref, sel_ref, m_ref, l_ref, acc_ref, *, n_blocks):
    j = pl.program_id(1)
    R = Q_BLOCK
    SR = GROUP_HEADS * R
    KB = MOBA_BLOCK

    @pl.when(j == 0)
    def _():
        kmean_ref[...] = jnp.zeros_like(kmean_ref)
        for n in range(n_blocks):
            kb = k_ref[0, n * KB:(n + 1) * KB, :].astype(F32)
            kmean_ref[n:n + 1, :] = jnp.mean(kb, axis=0, keepdims=True)

    own = j // (KB // R)
    q_raw = _head_stack(q_ref[0])
    q_stack = (q_raw.astype(F32) * (HEAD_DIM ** -0.5)).astype(BF16)

    gate = _dot_nt(kmean_ref[...], q_raw.astype(F32), precision=lax.Precision.HIGHEST)
    blk = lax.broadcasted_iota(I32, gate.shape, 0)
    past = blk < own
    g = jnp.where(past, gate, -jnp.inf)
    sel = jnp.zeros(gate.shape, F32)
    for _ in range(MOBA_TOPK):
        mx = jnp.max(g, axis=0, keepdims=True)
        first = jnp.min(jnp.where(g == mx, blk, MOBA_MAX_BLOCKS), axis=0, keepdims=True)
        pick = blk == first
        sel = jnp.where(pick & past, 1.0, sel)
        g = jnp.where(pick, -jnp.inf, g)
    sel_ref[...] = sel

    m_ref[...] = jnp.full(m_ref.shape, M_FLOOR, F32)
    l_ref[...] = jnp.zeros(l_ref.shape, F32)
    acc_ref[...] = jnp.zeros(acc_ref.shape, F32)

    def attend(kb, vb, bias):
        s = _dot_nt(kb, q_stack) + bias
        m_old = m_ref[...]
        m_new = jnp.maximum(m_old, jnp.max(s, axis=0, keepdims=True))
        alpha = jnp.exp(m_old - m_new)
        p = jnp.exp(s - m_new)
        l_ref[...] = alpha * l_ref[...] + jnp.sum(p, axis=0, keepdims=True)
        acc_ref[...] = alpha * acc_ref[...] + _dot_tn(vb, p.astype(BF16))
        m_ref[...] = m_new

    def block_bias(n):
        return jnp.broadcast_to(jnp.where(sel_ref[pl.ds(n, 1), :] > 0.5, 0.0, NEG), (KB, SR))

    def attend_span(first_block, bias):
        st = pl.multiple_of(first_block * KB, KB)
        rows = bias.shape[0]
        attend(k_ref[0, pl.ds(st, rows), :], v_ref[0, pl.ds(st, rows), :], bias)

    keypos = own * KB + lax.broadcasted_iota(I32, (KB, SR), 0)
    qpos = j * R + (lax.broadcasted_iota(I32, (KB, SR), 1) & (R - 1))
    causal = jnp.where(keypos <= qpos, 0.0, NEG)
    odd = (own & 1) == 1

    @pl.when(odd)
    def _():
        attend_span(own - 1, jnp.concatenate([block_bias(own - 1), causal], axis=0))

    @pl.when(jnp.logical_not(odd))
    def _():
        attend_span(own, causal)

    def body(pair, c):
        attend_span(2 * pair, jnp.concatenate([block_bias(2 * pair), block_bias(2 * pair + 1)], axis=0))
        return c

    lax.fori_loop(0, own // 2, body, 0)
    o_ref[0] = _head_unstack((acc_ref[...] / l_ref[...]).T, R).astype(o_ref.dtype)


def _moba(q, k, v):
    B, T, _ = q.shape
    n_blocks = T // MOBA_BLOCK
    assert T % MOBA_BLOCK == 0 and n_blocks <= MOBA_MAX_BLOCKS
    SR = GROUP_HEADS * Q_BLOCK
    return pl.pallas_call(
        functools.partial(_moba_kernel, n_blocks=n_blocks),
        grid=(B, T // Q_BLOCK),
        in_specs=[pl.BlockSpec((1, Q_BLOCK, GROUP_WIDTH), lambda b, j: (b, j, 0)),
                  pl.BlockSpec((1, T, GROUP_WIDTH), lambda b, j: (b, 0, 0)),
                  pl.BlockSpec((1, T, GROUP_WIDTH), lambda b, j: (b, 0, 0))],
        out_specs=pl.BlockSpec((1, Q_BLOCK, GROUP_WIDTH), lambda b, j: (b, j, 0)),
        out_shape=jax.ShapeDtypeStruct((B, T, GROUP_WIDTH), BF16),
        scratch_shapes=[pltpu.VMEM((MOBA_MAX_BLOCKS, GROUP_WIDTH), F32),
                        pltpu.VMEM((MOBA_MAX_BLOCKS, SR), F32),
                        pltpu.VMEM((1, SR), F32), pltpu.VMEM((1, SR), F32),
                        pltpu.VMEM((GROUP_WIDTH, SR), F32)],
        compiler_params=_cparams(2),
        name="moba_attention",
    )(q, k, v)


def _ret_kernel(q_ref, k_ref, v_ref, g_ref, dmask_ref, xi_ref, zeta_ref, gdec_ref, bd_ref, avg_ref,
                gng_ref, gnb_ref, o_ref, r_ref):
    j = pl.program_id(1)

    @pl.when(j == 0)
    def _():
        r_ref[...] = jnp.zeros_like(r_ref)

    C = RET_CHUNK
    q = q_ref[0]
    k = k_ref[0]
    v = v_ref[0]
    q_stack = _head_stack(q)
    inner = _dot_nt(q_stack, k) * dmask_ref[...]
    o = _head_unstack(_dot(inner.astype(BF16), v), C)
    R = r_ref[...]
    o = o + _dot(q, R.astype(BF16)) * xi_ref[...]
    kz = (k.astype(F32) * zeta_ref[...]).astype(BF16)
    r_ref[...] = gdec_ref[...] * R + bd_ref[...] * _dot_tn(kz, v)

    hp = lax.Precision.HIGHEST
    mu = _dot(o, avg_ref[...], precision=hp)
    oc = o - mu
    var = _dot(oc * oc, avg_ref[...], precision=hp)
    y = oc * lax.rsqrt(var + LN_EPS) * gng_ref[...] + gnb_ref[...]
    gte = g_ref[0]
    o_ref[0] = (y * (gte * jax.nn.sigmoid(gte))).astype(o_ref.dtype)


def _ret_tables():
    H, C, d = GROUP_HEADS, RET_CHUNK, HEAD_DIM
    log_g = np.log(1.0 - 2.0 ** (-5.0 - np.arange(H, dtype=np.float64)))
    n = np.arange(C, dtype=np.float64)
    diff = n[:, None] - n[None, :]
    dmask = np.where(diff >= 0, np.exp(log_g[:, None, None] * np.maximum(diff, 0.0)), 0.0)
    xi = np.exp(log_g[:, None] * (n + 1.0))
    zeta = np.exp(log_g[:, None] * (C - 1.0 - n))
    g_chunk = np.exp(log_g * C)
    head = np.arange(GROUP_WIDTH) // d
    bd = (head[:, None] == head[None, :]).astype(np.float64)
    to32 = lambda a: jnp.asarray(a, dtype=F32)
    return dict(dmask=to32(dmask.reshape(H * C, C)), xi=to32(xi.T[:, head]), zeta=to32(zeta.T[:, head]),
                gdec=to32(bd * g_chunk[head][:, None]), bd=to32(bd), avg=to32(bd / d))


def _retention(rq, rk, rv, rg, gn_g, gn_b, tabs):
    B, T, _ = rq.shape
    C = RET_CHUNK
    blk = pl.BlockSpec((1, C, GROUP_WIDTH), lambda b, j: (b, j, 0))
    const = lambda a: pl.BlockSpec(a.shape, lambda b, j: (0,) * a.ndim)
    consts = [tabs["dmask"], tabs["xi"], tabs["zeta"], tabs["gdec"], tabs["bd"], tabs["avg"], gn_g, gn_b]
    return pl.pallas_call(
        _ret_kernel,
        grid=(B, T // C),
        in_specs=[blk, blk, blk, blk] + [const(a) for a in consts],
        out_specs=blk,
        out_shape=jax.ShapeDtypeStruct((B, T, GROUP_WIDTH), BF16),
        scratch_shapes=[pltpu.VMEM((GROUP_WIDTH, GROUP_WIDTH), F32)],
        compiler_params=_cparams(2),
        name="retention",
    )(rq, rk, rv, rg, *consts)


def _rglru_kernel(x_ref, g_ref, cw_ref, cb_ref, wx_ref, bx_ref, wa_ref, ba_ref, lam_ref, o_ref,
                  xbuf, h_ref):
    j = pl.program_id(1)
    tc = RG_TC

    @pl.when(j == 0)
    def _():
        xbuf[0:8, :] = jnp.zeros((8, GROUP_WIDTH), F32)
        h_ref[...] = jnp.zeros_like(h_ref)

    xbuf[8:8 + tc, :] = x_ref[0]
    xc = cb_ref[...] + cw_ref[RG_CONV - 1:RG_CONV, :] * xbuf[8:8 + tc, :]
    for i in range(RG_CONV - 1):
        off = 8 - (RG_CONV - 1) + i
        xc = xc + cw_ref[i:i + 1, :] * xbuf[off:off + tc, :]
    xbuf[0:8, :] = xbuf[tc:tc + 8, :]

    xcb = xc.astype(BF16)
    gate_x = jax.nn.sigmoid(_dot(xcb, wx_ref[...]) + bx_ref[...])
    gate_a = jax.nn.sigmoid(_dot(xcb, wa_ref[...]) + ba_ref[...])
    lam = lam_ref[...]
    softplus_neg = jnp.maximum(-lam, 0.0) + jnp.log1p(jnp.exp(-jnp.abs(lam)))
    log_a = -RG_C * gate_a * softplus_neg
    a = jnp.exp(log_a)
    th = jnp.tanh(log_a)
    b = jnp.sqrt(-2.0 * th / (1.0 - th)) * (gate_x * xc)

    row = lax.broadcasted_iota(I32, (tc, GROUP_WIDTH), 0)
    d = 1
    while d < tc:
        keep = row >= d
        a_sh = jnp.where(keep, pltpu.roll(a, d, 0), 1.0)
        b_sh = jnp.where(keep, pltpu.roll(b, d, 0), 0.0)
        b = a * b_sh + b
        a = a * a_sh
        d *= 2
    h = b + a * h_ref[...]
    h_ref[...] = h[tc - 1:tc, :]

    xg = g_ref[0]
    gelu = 0.5 * xg * (1.0 + jnp.tanh(np.sqrt(2.0 / np.pi) * (xg + 0.044715 * xg * xg * xg)))
    o_ref[0] = (h * gelu).astype(o_ref.dtype)


def _block_diag(w):
    n, c, _ = w.shape
    eye = jnp.eye(n, dtype=w.dtype)
    return (eye[:, None, :, None] * w[:, :, None, :]).reshape(n * c, n * c)


def _rglru(cx, cg, conv_w, conv_b, wx, bx, wa, ba, lam):
    B, T, _ = cx.shape
    tc = RG_TC
    blk = pl.BlockSpec((1, tc, GROUP_WIDTH), lambda b, j: (b, j, 0))
    const = lambda a: pl.BlockSpec(a.shape, lambda b, j: (0,) * a.ndim)
    consts = [conv_w, conv_b, wx, bx, wa, ba, lam]
    return pl.pallas_call(
        _rglru_kernel,
        grid=(B, T // tc),
        in_specs=[blk, blk] + [const(a) for a in consts],
        out_specs=blk,
        out_shape=jax.ShapeDtypeStruct((B, T, GROUP_WIDTH), BF16),
        scratch_shapes=[pltpu.VMEM((tc + 8, GROUP_WIDTH), F32), pltpu.VMEM((1, GROUP_WIDTH), F32)],
        compiler_params=_cparams(2),
        name="rg_lru",
    )(cx, cg, *consts)


def _dsa_kernel(q_ref, k_ref, v_ref, qi_ref, ki_ref, wt_ref, o_ref,
                sc_ref, jcut_ref, m_ref, l_ref, acc_ref, *, n_sel, n_keys):
    j = pl.program_id(1)
    R = Q_BLOCK
    KB = DSA_KEY_BLOCK
    SR = GROUP_HEADS * R
    nkb = (j * R + R + KB - 1) // KB
    nsel = float(n_sel)

    rowk = lax.broadcasted_iota(I32, (KB, R), 0)
    qpos = j * R + lax.broadcasted_iota(I32, (KB, R), 1)

    qi = qi_ref[0]
    qi_stack = jnp.concatenate([qi[:, h * IDX_DIM:(h + 1) * IDX_DIM] for h in range(IDX_HEADS)], axis=0)
    wt = wt_ref[...]

    def score_body(kb, c):
        st = pl.multiple_of(kb * KB, KB)
        rel = _dot_nt(ki_ref[0, pl.ds(st, KB), :], qi_stack)
        sc = wt[0:1, :] * jnp.maximum(rel[:, 0:R], 0.0)
        for h in range(1, IDX_HEADS):
            sc = sc + wt[h:h + 1, :] * jnp.maximum(rel[:, h * R:(h + 1) * R], 0.0)
        sc_ref[pl.ds(st, KB), :] = jnp.where(st + rowk <= qpos, sc * IDX_SCALE, NEG)
        return c

    lax.fori_loop(0, nkb, score_body, 0)

    def blocks(fn, init):
        def body(kb, c):
            st = pl.multiple_of(kb * KB, KB)
            return fn(st, sc_ref[pl.ds(st, KB), :], c)
        return lax.fori_loop(0, nkb, body, init)

    def minmax(st, s, c):
        mn, mx = c
        return (jnp.minimum(mn, _fold_rows(jnp.where(s > 0.5 * NEG, s, BIG), jnp.min)),
                jnp.maximum(mx, _fold_rows(s, jnp.max)))

    mn8, mx8 = blocks(minmax, (jnp.full((FOLD_ROWS, R), BIG, F32), jnp.full((FOLD_ROWS, R), NEG, F32)))
    mn = jnp.min(mn8, axis=0, keepdims=True)
    mx = jnp.max(mx8, axis=0, keepdims=True)

    def count_ge(th):
        acc = blocks(lambda st, s, c: c + _fold_rows(jnp.where(s >= th, 1.0, 0.0), jnp.sum),
                     jnp.zeros((FOLD_ROWS, R), F32))
        return jnp.sum(acc, axis=0, keepdims=True)

    n_adm = (j * R + 1 + lax.broadcasted_iota(I32, (1, R), 1)).astype(F32)
    need = n_adm > nsel
    c_max = count_ge(mx)
    top_tie = need & (c_max >= nsel)
    lo0 = jnp.where(need, jnp.where(top_tie, mx, mn), 0.5 * NEG)
    cgt0 = jnp.where(top_tie, 0.0, c_max)
    act0 = jnp.where(need & jnp.logical_not(top_tie), 1.0, 0.0)
    tie0 = jnp.where(top_tie, 1.0, 0.0)

    def to_key(f):
        b = lax.bitcast_convert_type(f, I32)
        return b ^ (lax.shift_right_arithmetic(b, 31) & 0x7FFFFFFF)

    def from_key(kk):
        return lax.bitcast_convert_type(kk ^ (lax.shift_right_arithmetic(kk, 31) & 0x7FFFFFFF), F32)

    def bis_step(klo, khi, cgt, act, tie):
        on = act > 0.0
        kmid = lax.shift_right_arithmetic(klo, 1) + lax.shift_right_arithmetic(khi, 1) + (klo & khi & 1)
        stuck = kmid == klo
        cnt = count_ge(from_key(kmid))
        go = on & jnp.logical_not(stuck)
        up = go & (cnt >= nsel)
        dn = go & (cnt < nsel)
        return (jnp.where(up, kmid, klo), jnp.where(dn, kmid, khi), jnp.where(dn, cnt, cgt),
                jnp.where(go & (cnt != nsel), 1.0, 0.0), jnp.where(on & stuck, 1.0, tie))

    def bis_cond(c):
        return (c[1] > 0.0) & (c[0] < 40)

    def bis_body(c):
        flag = jnp.max(c[5])
        st = bis_step(*bis_step(*c[2:]))
        return (c[0] + 2, flag) + st

    res = lax.while_loop(bis_cond, bis_body,
                         (jnp.int32(0), jnp.max(act0), to_key(lo0), to_key(mx), cgt0, act0, tie0))
    lo, cgt, tie = from_key(res[2]), res[4], res[6]

    jcut_ref[...] = jnp.full((1, R), float(n_keys), F32)

    @pl.when(jnp.max(tie) > 0.0)
    def _():
        want = nsel - cgt
        tied = tie > 0.0

        def jb(it, c):
            a, b = c
            mid = jnp.floor((a + b) * 0.5)
            hit8 = blocks(lambda st, s, cc: cc + _fold_rows(
                jnp.where((s == lo) & ((st + rowk).astype(F32) <= mid), 1.0, 0.0), jnp.sum),
                jnp.zeros((FOLD_ROWS, R), F32))
            ok = jnp.sum(hit8, axis=0, keepdims=True) >= want
            return jnp.where(ok, a, mid), jnp.where(ok, mid, b)

        _, b = lax.fori_loop(0, int(np.ceil(np.log2(n_keys))) + 1, jb,
                             (jnp.full((1, R), -1.0, F32), jnp.full((1, R), float(n_keys - 1), F32)))
        jcut_ref[...] = jnp.where(tied, b, float(n_keys))

    q_stack = _head_stack((q_ref[0].astype(F32) * (HEAD_DIM ** -0.5)).astype(BF16))
    m_ref[...] = jnp.full(m_ref.shape, M_FLOOR, F32)
    l_ref[...] = jnp.zeros(l_ref.shape, F32)
    acc_ref[...] = jnp.zeros(acc_ref.shape, F32)
    jcut = jcut_ref[...]

    def att_body(kb, c):
        st = pl.multiple_of(kb * KB, KB)
        sc = sc_ref[pl.ds(st, KB), :]
        keep = (sc > lo) | ((sc == lo) & ((st + rowk).astype(F32) <= jcut))
        bias = jnp.where(keep, 0.0, NEG)
        s = _dot_nt(k_ref[0, pl.ds(st, KB), :], q_stack) + jnp.concatenate([bias] * GROUP_HEADS, axis=1)
        m_old = m_ref[...]
        m_new = jnp.maximum(m_old, jnp.max(s, axis=0, keepdims=True))
        alpha = jnp.exp(m_old - m_new)
        p = jnp.exp(s - m_new)
        l_ref[...] = alpha * l_ref[...] + jnp.sum(p, axis=0, keepdims=True)
        acc_ref[...] = alpha * acc_ref[...] + _dot_tn(v_ref[0, pl.ds(st, KB), :], p.astype(BF16))
        m_ref[...] = m_new
        return c

    lax.fori_loop(0, nkb, att_body, 0)
    o_ref[0] = _head_unstack((acc_ref[...] / l_ref[...]).T, R).astype(o_ref.dtype)


def _dsa(q, k, v, qi, ki, wt):
    B, T, _ = q.shape
    KB = DSA_KEY_BLOCK
    assert T % KB == 0
    n_sel = min(DSA_TOPK, T // 4)
    SR = GROUP_HEADS * Q_BLOCK
    nq = T // Q_BLOCK
    qblk = lambda wd: pl.BlockSpec((1, Q_BLOCK, wd), lambda b, j: (b, j, 0))
    full = lambda wd: pl.BlockSpec((1, T, wd), lambda b, j: (b, 0, 0))
    return pl.pallas_call(
        functools.partial(_dsa_kernel, n_sel=n_sel, n_keys=T),
        grid=(B, nq),
        in_specs=[qblk(GROUP_WIDTH), full(GROUP_WIDTH), full(GROUP_WIDTH),
                  qblk(IDX_HEADS * IDX_DIM), full(IDX_DIM),
                  pl.BlockSpec((IDX_HEADS, Q_BLOCK), lambda b, j: (0, b * nq + j))],
        out_specs=qblk(GROUP_WIDTH),
        out_shape=jax.ShapeDtypeStruct((B, T, GROUP_WIDTH), BF16),
        scratch_shapes=[pltpu.VMEM((T, Q_BLOCK), F32), pltpu.VMEM((1, Q_BLOCK), F32),
                        pltpu.VMEM((1, SR), F32), pltpu.VMEM((1, SR), F32),
                        pltpu.VMEM((GROUP_WIDTH, SR), F32)],
        compiler_params=_cparams(2),
        name="dsa_attention",
    )(q, k, v, qi, ki, wt)


def _outproj_kernel(oa, orr, oc, od, x_ref, w_ref, g_ref, b_ref, rw_ref, rb_ref, ltri_ref,
                    x1_ref, ti_ref, tg_ref, rk_ref, cnt_ref, run_ref):
    GW = GROUP_WIDTH

    @pl.when(pl.program_id(0) == 0)
    def _():
        run_ref[...] = jnp.zeros_like(run_ref)

    acc = _dot(oa[...], w_ref[0:GW, :])
    acc = acc + _dot(orr[...], w_ref[GW:2 * GW, :])
    acc = acc + _dot(oc[...], w_ref[2 * GW:3 * GW, :])
    acc = acc + _dot(od[...], w_ref[3 * GW:4 * GW, :])
    x1 = _layer_norm_rows(ALPHA * x_ref[...] + acc, g_ref[...], b_ref[...])
    x1_ref[...] = x1

    logits = _dot(x1.astype(BF16), rw_ref[...]) + rb_ref[...]
    col = lax.broadcasted_iota(I32, logits.shape, 1)
    kcol = lax.broadcasted_iota(I32, ti_ref.shape, 1)
    g = logits
    ti = jnp.zeros(ti_ref.shape, I32)
    tv = jnp.zeros(tg_ref.shape, F32)
    picks = []
    for kk in range(TOP_K):
        mx = jnp.max(g, axis=1, keepdims=True)
        first = jnp.min(jnp.where(g == mx, col, N_EXPERTS), axis=1, keepdims=True)
        ti = jnp.where(kcol == kk, first, ti)
        tv = jnp.where(kcol == kk, mx, tv)
        picks.append(col == first)
        g = jnp.where(picks[-1], -jnp.inf, g)
    e = jnp.exp(tv - jnp.max(tv, axis=1, keepdims=True))
    ti_ref[...] = ti
    tg_ref[...] = e / jnp.sum(e, axis=1, keepdims=True)

    sel = jnp.where(picks[0] | picks[1] | picks[2] | picks[3], 1.0, 0.0)
    before = run_ref[...] + _dot(ltri_ref[...], sel.astype(BF16))
    rk = jnp.zeros(rk_ref.shape, F32)
    for kk in range(TOP_K):
        rk = jnp.where(kcol == kk, jnp.sum(jnp.where(picks[kk], before, 0.0), axis=1, keepdims=True), rk)
    rk_ref[...] = rk.astype(I32)
    run_ref[...] = run_ref[...] + jnp.sum(sel, axis=0, keepdims=True)
    cnt_ref[...] = run_ref[...]


def _outproj(oa, orr, oc, od, x2, w_out, g, b, rw, rb):
    N = x2.shape[0]
    tm = OUT_TM
    row = lambda w: pl.BlockSpec((tm, w), lambda i: (i, 0))
    const = lambda a: pl.BlockSpec(a.shape, lambda i: (0,) * a.ndim)
    ltri = jnp.asarray(np.tril(np.ones((tm, tm), np.float32), -1), dtype=BF16)
    return pl.pallas_call(
        _outproj_kernel,
        grid=(N // tm,),
        in_specs=[row(GROUP_WIDTH)] * 4 + [row(D_MODEL), const(w_out), const(g), const(b), const(rw), const(rb),
                                           const(ltri)],
        out_specs=[row(D_MODEL), row(TOP_K), row(TOP_K), row(TOP_K),
                   pl.BlockSpec((1, N_EXPERTS), lambda i: (0, 0))],
        out_shape=[jax.ShapeDtypeStruct((N, D_MODEL), F32), jax.ShapeDtypeStruct((N, TOP_K), I32),
                   jax.ShapeDtypeStruct((N, TOP_K), F32), jax.ShapeDtypeStruct((N, TOP_K), I32),
                   jax.ShapeDtypeStruct((1, N_EXPERTS), F32)],
        scratch_shapes=[pltpu.VMEM((1, N_EXPERTS), F32)],
        compiler_params=_cparams(1),
        name="outproj_ln_router",
    )(oa, orr, oc, od, x2, w_out, g, b, rw, rb, ltri)


def _dispatch_kernel(tv_ref, pos_ref, x_ref, xs_hbm, xbuf, sem, zsem, *, n_tiles):
    i = pl.program_id(0)
    n = pl.num_programs(0)
    tm = DSP_TM
    par = i % 2

    @pl.when(i == 0)
    def _():
        xbuf[1] = jnp.zeros((tm, D_MODEL), F32)

        def fill(t, c):
            @pl.when(tv_ref[t] < MOE_TM)
            def _():
                pltpu.make_async_copy(xbuf.at[1], xs_hbm.at[pl.ds(pl.multiple_of(t * MOE_TM, MOE_TM), MOE_TM)],
                                      zsem).start()
            return c

        def drain(t, c):
            @pl.when(tv_ref[t] < MOE_TM)
            def _():
                pltpu.make_async_copy(xbuf.at[1], xs_hbm.at[pl.ds(0, MOE_TM)], zsem).wait()
            return c

        lax.fori_loop(0, n_tiles, fill, 0)
        lax.fori_loop(0, n_tiles, drain, 0)

    def wait_step(p):
        for _ in range(TOP_K):
            pltpu.make_async_copy(xbuf.at[p], xbuf.at[p], sem.at[p]).wait()

    @pl.when(i >= 2)
    def _():
        wait_step(par)

    xbuf[par] = x_ref[...]

    def body(r, c):
        for kk in range(TOP_K):
            pltpu.make_async_copy(xbuf.at[par, pl.ds(r, 1)], xs_hbm.at[pl.ds(pos_ref[r * TOP_K + kk], 1)],
                                  sem.at[par]).start()
        return c

    lax.fori_loop(0, tm, body, 0, unroll=4)

    @pl.when(i == n - 1)
    def _():
        wait_step(1 - par)
        wait_step(par)


def _dispatch(pos, x1, tile_valid):
    N = x1.shape[0]
    tm = DSP_TM
    n_tiles = tile_valid.shape[0]
    assert N // tm >= 2 and tm == MOE_TM
    grid_spec = pltpu.PrefetchScalarGridSpec(
        num_scalar_prefetch=1,
        grid=(N // tm,),
        in_specs=[pl.BlockSpec((tm * TOP_K,), lambda i, tv: (i,), memory_space=pltpu.SMEM),
                  pl.BlockSpec((tm, D_MODEL), lambda i, tv: (i, 0))],
        out_specs=pl.BlockSpec(memory_space=pl.ANY),
        scratch_shapes=[pltpu.VMEM((2, tm, D_MODEL), F32), pltpu.SemaphoreType.DMA((2,)),
                        pltpu.SemaphoreType.DMA(())],
    )
    return pl.pallas_call(
        functools.partial(_dispatch_kernel, n_tiles=n_tiles),
        grid_spec=grid_spec,
        out_shape=jax.ShapeDtypeStruct((n_tiles * MOE_TM, D_MODEL), F32),
        compiler_params=_cparams(1),
        name="moe_dispatch",
    )(tile_valid, pos, x1)


def _moe_kernel(te_ref, nv_ref, x_ref, w1_ref, b1_ref, w2_ref, b2_ref, y_ref, w1b, w2b):
    i = pl.program_id(0)
    tm = MOE_TM
    n_valid = nv_ref[i]

    @pl.when((i == 0) | (te_ref[i] != te_ref[jnp.maximum(i - 1, 0)]))
    def _():
        step = 128
        for c in range(D_MODEL // step):
            w1b[c * step:(c + 1) * step, :] = w1_ref[0, 0, c * step:(c + 1) * step, :].astype(BF16)
        for c in range(D_FF // step):
            w2b[c * step:(c + 1) * step, :] = w2_ref[0, 0, c * step:(c + 1) * step, :].astype(BF16)

    @pl.when(n_valid > 0)
    def _():
        h = _dot(x_ref[...].astype(BF16), w1b[...]) + b1_ref[0]
        glu_in = jnp.minimum(h[:, :D_FF], SWIGLU_LIMIT)
        up = jnp.clip(h[:, D_FF:], -SWIGLU_LIMIT, SWIGLU_LIMIT)
        glu = glu_in * jax.nn.sigmoid(SWIGLU_ALPHA * glu_in)
        y_ref[...] = _dot(((up + 1.0) * glu).astype(BF16), w2b[...]) + b2_ref[0]

    @pl.when(n_valid == 0)
    def _():
        y_ref[...] = jnp.zeros_like(y_ref)


def _moe_experts(xs, tile_expert, tile_valid, w1, b1, w2, b2, layer):
    tm = MOE_TM
    n_tiles = tile_expert.shape[0]
    grid_spec = pltpu.PrefetchScalarGridSpec(
        num_scalar_prefetch=2,
        grid=(n_tiles,),
        in_specs=[
            pl.BlockSpec((tm, D_MODEL), lambda i, te, nv: (i, 0)),
            pl.BlockSpec((1, 1, D_MODEL, 2 * D_FF), lambda i, te, nv: (layer, te[i], 0, 0)),
            pl.BlockSpec((1, 1, 2 * D_FF), lambda i, te, nv: (te[i], 0, 0)),
            pl.BlockSpec((1, 1, D_FF, D_MODEL), lambda i, te, nv: (layer, te[i], 0, 0)),
            pl.BlockSpec((1, 1, D_MODEL), lambda i, te, nv: (te[i], 0, 0)),
        ],
        out_specs=pl.BlockSpec((tm, D_MODEL), lambda i, te, nv: (i, 0)),
        scratch_shapes=[pltpu.VMEM((D_MODEL, 2 * D_FF), BF16), pltpu.VMEM((D_FF, D_MODEL), BF16)],
    )
    return pl.pallas_call(
        _moe_kernel,
        grid_spec=grid_spec,
        out_shape=jax.ShapeDtypeStruct((n_tiles * tm, D_MODEL), F32),
        compiler_params=_cparams(1),
        name="moe_experts",
    )(tile_expert, tile_valid, xs, w1, b1, w2, b2)


def _combine_kernel(pos_ref, posn_ref, y_hbm, x1_ref, tg_ref, g_ref, b_ref, o_ref, ybuf, sem):
    i = pl.program_id(0)
    n = pl.num_programs(0)
    tm = CMB_TM
    slot = i % 2

    def issue(idx_ref, s):
        def body(r, c):
            for kk in range(TOP_K):
                pltpu.make_async_copy(y_hbm.at[pl.ds(idx_ref[r * TOP_K + kk], 1)],
                                      ybuf.at[s, kk, pl.ds(r, 1)], sem.at[s]).start()
            return c
        lax.fori_loop(0, tm, body, 0, unroll=4)

    @pl.when(i == 0)
    def _():
        issue(pos_ref, 0)

    @pl.when(i + 1 < n)
    def _():
        issue(posn_ref, 1 - slot)

    pltpu.make_async_copy(ybuf.at[slot], ybuf.at[slot], sem.at[slot]).wait()
    tg = tg_ref[...]
    moe = tg[:, 0:1] * ybuf[slot, 0]
    for kk in range(1, TOP_K):
        moe = moe + tg[:, kk:kk + 1] * ybuf[slot, kk]
    o_ref[...] = _layer_norm_rows(ALPHA * x1_ref[...] + moe, g_ref[...], b_ref[...])


def _combine(pos, y_sorted, x1, tg, g, b):
    N = x1.shape[0]
    tm = CMB_TM
    n = N // tm
    row = lambda w: pl.BlockSpec((tm, w), lambda i: (i, 0))
    const = lambda a: pl.BlockSpec(a.shape, lambda i: (0,) * a.ndim)
    return pl.pallas_call(
        _combine_kernel,
        grid=(n,),
        in_specs=[pl.BlockSpec((tm * TOP_K,), lambda i: (i,), memory_space=pltpu.SMEM),
                  pl.BlockSpec((tm * TOP_K,), lambda i: (jnp.minimum(i + 1, n - 1),), memory_space=pltpu.SMEM),
                  pl.BlockSpec(memory_space=pl.ANY), row(D_MODEL), row(TOP_K), const(g), const(b)],
        out_specs=row(D_MODEL),
        out_shape=jax.ShapeDtypeStruct((N, D_MODEL), F32),
        scratch_shapes=[pltpu.VMEM((2, TOP_K, tm, D_MODEL), F32), pltpu.SemaphoreType.DMA((2,))],
        compiler_params=_cparams(1),
        name="moe_combine_ln",
    )(pos, pos, y_sorted, x1, tg, g, b)


def _routing_tables(top_i, rank, counts_f, n_tiles):
    tm = MOE_TM
    counts = counts_f.reshape(-1).astype(I32)
    padded = ((counts + tm - 1) // tm) * tm
    ends = jnp.cumsum(padded)
    offsets = ends - padded
    onehot = top_i[:, :, None] == jnp.arange(N_EXPERTS, dtype=I32)[None, None, :]
    pos = jnp.sum(jnp.where(onehot, offsets[None, None, :], 0), axis=-1) + rank
    tile_start = jnp.arange(n_tiles, dtype=I32) * tm
    tile_expert = jnp.sum((ends[None, :] <= tile_start[:, None]).astype(I32), axis=1)
    tile_expert = jnp.minimum(tile_expert, N_EXPERTS - 1)
    n_used = ends[-1] // tm
    last_expert = tile_expert[jnp.maximum(n_used - 1, 0)]
    tile_expert = jnp.where(tile_start < ends[-1], tile_expert, last_expert)
    valid_end = (offsets + counts)[tile_expert]
    tile_valid = jnp.clip(valid_end - tile_start, 0, tm)
    return pos.reshape(-1).astype(I32), tile_expert.astype(I32), tile_valid.astype(I32)


def _rope_tables(T):
    inv = ROPE_THETA ** (-jnp.arange(0, HEAD_DIM, 2, dtype=F32) / HEAD_DIM)
    ang = jnp.arange(T, dtype=F32)[:, None] * inv[None, :]
    cos, sin = jnp.cos(ang), jnp.sin(ang)
    cos_t = jnp.tile(jnp.concatenate([cos, cos], axis=-1), (1, GROUP_HEADS))
    sin_t = jnp.tile(jnp.concatenate([-sin, sin], axis=-1), (1, GROUP_HEADS))
    return cos_t, sin_t


def _pad_w_in(w_in):
    base = 12 * GROUP_WIDTH + IDX_HEADS * IDX_DIM
    w = jnp.zeros((D_MODEL, IN_PAD), F32)
    w = w.at[:, :base + IDX_DIM].set(w_in[:, :base + IDX_DIM])
    w = w.at[:, base + 128:base + 128 + IDX_HEADS].set(w_in[:, base + IDX_DIM:])
    return w.astype(BF16)


def _layer(x2, B, T, cos_t, sin_t, tabs, w_in, ret_gn_g, ret_gn_b, conv_w, conv_b, rg_wx, rg_bx, rg_wa,
           rg_ba, rg_lambda, w_out, ln1_g, ln1_b, router_w, router_b, exp_w1, exp_b1, exp_w2, exp_b2,
           ln2_g, ln2_b, layer):
    N = B * T
    r2 = lambda a: a.reshape(1, -1)
    (aq, ak, av, rq, rk, rv, rg, cx, cg, dq, dk, dv, dqi, dki, dwt) = _proj(x2, _pad_w_in(w_in), cos_t, sin_t, T)
    seq = lambda a: a.reshape(B, T, a.shape[-1])
    o_a = _moba(seq(aq), seq(ak), seq(av))
    o_r = _retention(seq(rq), seq(rk), seq(rv), seq(rg), r2(ret_gn_g), r2(ret_gn_b), tabs)
    o_c = _rglru(seq(cx), seq(cg), conv_w, r2(conv_b), _block_diag(rg_wx).astype(BF16), r2(rg_bx),
                 _block_diag(rg_wa).astype(BF16), r2(rg_ba), r2(rg_lambda))
    o_d = _dsa(seq(dq), seq(dk), seq(dv), seq(dqi), seq(dki), dwt)
    flat = lambda a: a.reshape(N, GROUP_WIDTH)
    x1, top_i, top_g, rank, counts = _outproj(flat(o_a), flat(o_r), flat(o_c), flat(o_d), x2,
                                              w_out.astype(BF16), r2(ln1_g), r2(ln1_b),
                                              router_w.astype(BF16), r2(router_b))
    n_tiles = (N * TOP_K) // MOE_TM + N_EXPERTS
    pos, tile_expert, tile_valid = _routing_tables(top_i, rank, counts, n_tiles)
    xs = _dispatch(pos, x1, tile_valid)
    y_sorted = _moe_experts(xs, tile_expert, tile_valid, exp_w1, exp_b1.reshape(N_EXPERTS, 1, -1), exp_w2,
                            exp_b2.reshape(N_EXPERTS, 1, -1), layer)
    return _combine(pos, y_sorted, x1, top_g, r2(ln2_g), r2(ln2_b))


def kernel(x, w_in, ret_gn_g, ret_gn_b, conv_w, conv_b, rg_wx, rg_bx, rg_wa, rg_ba, rg_lambda, w_out,
           ln1_g, ln1_b, router_w, router_b, exp_w1, exp_b1, exp_w2, exp_b2, ln2_g, ln2_b):
    B, T, D = x.shape
    cos_t, sin_t = _rope_tables(T)
    tabs = _ret_tables()
    x2 = x.reshape(B * T, D)
    for l in range(w_in.shape[0]):
        x2 = _layer(x2, B, T, cos_t, sin_t, tabs, w_in[l], ret_gn_g[l], ret_gn_b[l], conv_w[l], conv_b[l],
                    rg_wx[l], rg_bx[l], rg_wa[l], rg_ba[l], rg_lambda[l], w_out[l], ln1_g[l], ln1_b[l],
                    router_w[l], router_b[l], exp_w1, exp_b1[l], exp_w2, exp_b2[l], ln2_g[l], ln2_b[l], l)
    return x2.reshape(B, T, D)
```

```python
import functools

import numpy as np
import jax
import jax.numpy as jnp
from jax import lax
from jax.experimental import pallas as pl
from jax.experimental.pallas import tpu as pltpu

F32 = jnp.float32
BF16 = jnp.bfloat16
I32 = jnp.int32

D_MODEL = 1024
DEPTH = 2
HEAD_DIM = 64
GROUP_WIDTH = 256
GROUP_HEADS = 4
ROPE_THETA = 10000.0
Q_BLOCK = 128
MOBA_BLOCK = 256
MOBA_TOPK = 3
MOBA_MAX_BLOCKS = 16
RET_CHUNK = 128
RG_CONV = 4
RG_C = 8.0
IDX_HEADS = 8
IDX_DIM = 64
IDX_SCALE = (IDX_HEADS ** -0.5) * (IDX_DIM ** -0.5)
DSA_TOPK = 256
DSA_Q = 256
DSA_KEY_BLOCK = 512
N_EXPERTS = 32
TOP_K = 4
D_FF = 1024
SWIGLU_LIMIT = 7.0
SWIGLU_ALPHA = 1.702
ALPHA = (2 * DEPTH) ** 0.25
LN_EPS = 1e-5
IN_WIDTH = 12 * GROUP_WIDTH + IDX_HEADS * IDX_DIM + IDX_DIM + IDX_HEADS
IN_PAD = 15 * GROUP_WIDTH

NEG = -1e30
M_FLOOR = -1e29
BIG = 1e30
VMEM_LIMIT = 56 * 1024 * 1024
FOLD_ROWS = 32

PROJ_TM = 256
OUT_TM = 512
RG_TC = 256
MOE_TM = 512
DSP_TM = 512
CMB_TM = 256


def _cparams(ndims):
    return pltpu.CompilerParams(dimension_semantics=("arbitrary",) * ndims,
                                vmem_limit_bytes=VMEM_LIMIT)


def _dot(a, b, precision=None):
    return jnp.dot(a, b, preferred_element_type=F32, precision=precision)


def _dot_nt(a, b, precision=None):
    return lax.dot_general(a, b, (((1,), (1,)), ((), ())), preferred_element_type=F32,
                           precision=precision)


def _dot_tn(a, b):
    return lax.dot_general(a, b, (((0,), (0,)), ((), ())), preferred_element_type=F32)


def _head_stack(q):
    head = lax.shift_right_logical(lax.broadcasted_iota(I32, q.shape, 1), 6)
    qf = q.astype(F32)
    return jnp.concatenate([jnp.where(head == h, qf, 0.0) for h in range(GROUP_HEADS)],
                           axis=0).astype(q.dtype)


def _head_unstack(s, rows):
    head = lax.shift_right_logical(lax.broadcasted_iota(I32, (rows, GROUP_WIDTH), 1), 6)
    out = jnp.zeros((rows, GROUP_WIDTH), F32)
    for h in range(GROUP_HEADS):
        out = out + jnp.where(head == h, s[h * rows:(h + 1) * rows], 0.0)
    return out


def _fold_rows(x, op):
    return op(x.reshape(x.shape[0] // FOLD_ROWS, FOLD_ROWS, x.shape[1]), axis=0)


def _layer_norm_rows(y, g, b):
    mu = jnp.mean(y, axis=-1, keepdims=True)
    yc = y - mu
    var = jnp.mean(yc * yc, axis=-1, keepdims=True)
    return yc * lax.rsqrt(var + LN_EPS) * g + b


def _proj_kernel(x_ref, w_ref, cos_ref, sin_ref,
                 aq, ak, av, rq, rk, rv, rg, cx, cg, dq, dk, dv, dqi, dki, dwt):
    xb = x_ref[...].astype(BF16)
    cos = cos_ref[...]
    sin = sin_ref[...]
    first_half = (lax.broadcasted_iota(I32, cos.shape, 1) & (HEAD_DIM - 1)) < (HEAD_DIM // 2)

    def seg(i):
        return _dot(xb, w_ref[:, i * GROUP_WIDTH:(i + 1) * GROUP_WIDTH])

    def rope(p):
        rot = jnp.where(first_half, pltpu.roll(p, GROUP_WIDTH - HEAD_DIM // 2, 1),
                        pltpu.roll(p, HEAD_DIM // 2, 1))
        return p * cos + rot * sin

    aq[...] = rope(seg(0)).astype(BF16)
    ak[...] = rope(seg(1)).astype(BF16)
    av[...] = seg(2).astype(BF16)
    rq[...] = rope(seg(3)).astype(BF16)
    rk[...] = (rope(seg(4)) * (HEAD_DIM ** -0.5)).astype(BF16)
    rv[...] = seg(5).astype(BF16)
    rg[...] = seg(6)
    cx[...] = seg(7)
    cg[...] = seg(8)
    dq[...] = rope(seg(9)).astype(BF16)
    dk[...] = rope(seg(10)).astype(BF16)
    dv[...] = seg(11).astype(BF16)
    dqi[:, 0:GROUP_WIDTH] = rope(seg(12)).astype(BF16)
    dqi[:, GROUP_WIDTH:2 * GROUP_WIDTH] = rope(seg(13)).astype(BF16)
    last = seg(14)
    dki[...] = rope(last)[:, 0:IDX_DIM].astype(BF16)
    dwt[...] = last[:, 128:256].T[0:IDX_HEADS, :]


def _proj(x2, w_pad, cos_t, sin_t, T):
    N = x2.shape[0]
    tm = PROJ_TM
    tpb = T // tm
    row = lambda w: pl.BlockSpec((tm, w), lambda i: (i, 0))
    tab = pl.BlockSpec((tm, GROUP_WIDTH), lambda i: (i % tpb, 0))
    widths = [256] * 12 + [512, IDX_DIM]
    dtypes = [BF16, BF16, BF16, BF16, BF16, BF16, F32, F32, F32, BF16, BF16, BF16, BF16, BF16]
    return pl.pallas_call(
        _proj_kernel,
        grid=(N // tm,),
        in_specs=[row(D_MODEL), pl.BlockSpec((D_MODEL, IN_PAD), lambda i: (0, 0)), tab, tab],
        out_specs=[row(w) for w in widths] + [pl.BlockSpec((IDX_HEADS, tm), lambda i: (0, i))],
        out_shape=[jax.ShapeDtypeStruct((N, w), d) for w, d in zip(widths, dtypes)]
        + [jax.ShapeDtypeStruct((IDX_HEADS, N), F32)],
        compiler_params=_cparams(1),
        name="proj_rope",
    )(x2, w_pad, cos_t, sin_t)


def _moba_kernel(q_ref, k_ref, v_ref, o_ref, kmean_ref, sel_ref, m_ref, l_ref, acc_ref, *, n_blocks):
    j = pl.program_id(1)
    R = Q_BLOCK
    SR = GROUP_HEADS * R
    KB = MOBA_BLOCK

    @pl.when(j == 0)
    def _():
        kmean_ref[...] = jnp.zeros_like(kmean_ref)
        for n in range(n_blocks):
            kb = k_ref[0, n * KB:(n + 1) * KB, :].astype(F32)
            kmean_ref[n:n + 1, :] = jnp.mean(kb, axis=0, keepdims=True)

    own = j // (KB // R)
    q_raw = _head_stack(q_ref[0])
    q_stack = (q_raw.astype(F32) * (HEAD_DIM ** -0.5)).astype(BF16)

    gate = _dot_nt(kmean_ref[...], q_raw.astype(F32), precision=lax.Precision.HIGHEST)
    blk = lax.broadcasted_iota(I32, gate.shape, 0)
    past = blk < own
    g = jnp.where(past, gate, -jnp.inf)
    sel = jnp.zeros(gate.shape, F32)
    for _ in range(MOBA_TOPK):
        mx = jnp.max(g, axis=0, keepdims=True)
        first = jnp.min(jnp.where(g == mx, blk, MOBA_MAX_BLOCKS), axis=0, keepdims=True)
        pick = blk == first
        sel = jnp.where(pick & past, 1.0, sel)
        g = jnp.where(pick, -jnp.inf, g)
    sel_ref[...] = sel

    m_ref[...] = jnp.full(m_ref.shape, M_FLOOR, F32)
    l_ref[...] = jnp.zeros(l_ref.shape, F32)
    acc_ref[...] = jnp.zeros(acc_ref.shape, F32)

    def attend(kb, vb, bias):
        s = _dot_nt(kb, q_stack) + bias
        m_old = m_ref[...]
        m_new = jnp.maximum(m_old, jnp.max(s, axis=0, keepdims=True))
        alpha = jnp.exp(m_old - m_new)
        p = jnp.exp(s - m_new)
        l_ref[...] = alpha * l_ref[...] + jnp.sum(p, axis=0, keepdims=True)
        acc_ref[...] = alpha * acc_ref[...] + _dot_tn(vb, p.astype(BF16))
        m_ref[...] = m_new

    def block_bias(n):
        return jnp.broadcast_to(jnp.where(sel_ref[pl.ds(n, 1), :] > 0.5, 0.0, NEG), (KB, SR))

    def attend_span(first_block, bias):
        st = pl.multiple_of(first_block * KB, KB)
        rows = bias.shape[0]
        attend(k_ref[0, pl.ds(st, rows), :], v_ref[0, pl.ds(st, rows), :], bias)

    keypos = own * KB + lax.broadcasted_iota(I32, (KB, SR), 0)
    qpos = j * R + (lax.broadcasted_iota(I32, (KB, SR), 1) & (R - 1))
    causal = jnp.where(keypos <= qpos, 0.0, NEG)
    odd = (own & 1) == 1

    @pl.when(odd)
    def _():
        attend_span(own - 1, jnp.concatenate([block_bias(own - 1), causal], axis=0))

    @pl.when(jnp.logical_not(odd))
    def _():
        attend_span(own, causal)

    def body(pair, c):
        attend_span(2 * pair, jnp.concatenate([block_bias(2 * pair), block_bias(2 * pair + 1)], axis=0))
        return c

    lax.fori_loop(0, own // 2, body, 0)
    o_ref[0] = _head_unstack((acc_ref[...] / l_ref[...]).T, R).astype(o_ref.dtype)


def _moba(q, k, v):
    B, T, _ = q.shape
    n_blocks = T // MOBA_BLOCK
    assert T % MOBA_BLOCK == 0 and n_blocks <= MOBA_MAX_BLOCKS
    SR = GROUP_HEADS * Q_BLOCK
    return pl.pallas_call(
        functools.partial(_moba_kernel, n_blocks=n_blocks),
        grid=(B, T // Q_BLOCK),
        in_specs=[pl.BlockSpec((1, Q_BLOCK, GROUP_WIDTH), lambda b, j: (b, j, 0)),
                  pl.BlockSpec((1, T, GROUP_WIDTH), lambda b, j: (b, 0, 0)),
                  pl.BlockSpec((1, T, GROUP_WIDTH), lambda b, j: (b, 0, 0))],
        out_specs=pl.BlockSpec((1, Q_BLOCK, GROUP_WIDTH), lambda b, j: (b, j, 0)),
        out_shape=jax.ShapeDtypeStruct((B, T, GROUP_WIDTH), BF16),
        scratch_shapes=[pltpu.VMEM((MOBA_MAX_BLOCKS, GROUP_WIDTH), F32),
                        pltpu.VMEM((MOBA_MAX_BLOCKS, SR), F32),
                        pltpu.VMEM((1, SR), F32), pltpu.VMEM((1, SR), F32),
                        pltpu.VMEM((GROUP_WIDTH, SR), F32)],
        compiler_params=_cparams(2),
        name="moba_attention",
    )(q, k, v)


def _ret_kernel(q_ref, k_ref, v_ref, g_ref, dmask_ref, xi_ref, zeta_ref, gdec_ref, bd_ref, avg_ref,
                gng_ref, gnb_ref, o_ref, r_ref):
    j = pl.program_id(0)

    @pl.when(j == 0)
    def _():
        r_ref[...] = jnp.zeros_like(r_ref)

    C = RET_CHUNK
    hp = lax.Precision.HIGHEST
    for b in range(q_ref.shape[0]):
        q = q_ref[b]
        k = k_ref[b]
        v = v_ref[b]
        q_stack = _head_stack(q)
        inner = _dot_nt(q_stack, k) * dmask_ref[...]
        o = _head_unstack(_dot(inner.astype(BF16), v), C)
        R = r_ref[b]
        o = o + _dot(q, R.astype(BF16)) * xi_ref[...]
        kz = (k.astype(F32) * zeta_ref[...]).astype(BF16)
        r_ref[b] = gdec_ref[...] * R + bd_ref[...] * _dot_tn(kz, v)

        mu = _dot(o, avg_ref[...], precision=hp)
        oc = o - mu
        var = _dot(oc * oc, avg_ref[...], precision=hp)
        y = oc * lax.rsqrt(var + LN_EPS) * gng_ref[...] + gnb_ref[...]
        gte = g_ref[b]
        o_ref[b] = (y * (gte * jax.nn.sigmoid(gte))).astype(o_ref.dtype)


def _ret_tables():
    H, C, d = GROUP_HEADS, RET_CHUNK, HEAD_DIM
    log_g = np.log(1.0 - 2.0 ** (-5.0 - np.arange(H, dtype=np.float64)))
    n = np.arange(C, dtype=np.float64)
    diff = n[:, None] - n[None, :]
    dmask = np.where(diff >= 0, np.exp(log_g[:, None, None] * np.maximum(diff, 0.0)), 0.0)
    xi = np.exp(log_g[:, None] * (n + 1.0))
    zeta = np.exp(log_g[:, None] * (C - 1.0 - n))
    g_chunk = np.exp(log_g * C)
    head = np.arange(GROUP_WIDTH) // d
    bd = (head[:, None] == head[None, :]).astype(np.float64)
    to32 = lambda a: jnp.asarray(a, dtype=F32)
    return dict(dmask=to32(dmask.reshape(H * C, C)), xi=to32(xi.T[:, head]), zeta=to32(zeta.T[:, head]),
                gdec=to32(bd * g_chunk[head][:, None]), bd=to32(bd), avg=to32(bd / d))


def _retention(rq, rk, rv, rg, gn_g, gn_b, tabs):
    B, T, _ = rq.shape
    C = RET_CHUNK
    blk = pl.BlockSpec((B, C, GROUP_WIDTH), lambda j: (0, j, 0))
    const = lambda a: pl.BlockSpec(a.shape, lambda j: (0,) * a.ndim)
    consts = [tabs["dmask"], tabs["xi"], tabs["zeta"], tabs["gdec"], tabs["bd"], tabs["avg"], gn_g, gn_b]
    return pl.pallas_call(
        _ret_kernel,
        grid=(T // C,),
        in_specs=[blk, blk, blk, blk] + [const(a) for a in consts],
        out_specs=blk,
        out_shape=jax.ShapeDtypeStruct((B, T, GROUP_WIDTH), BF16),
        scratch_shapes=[pltpu.VMEM((B, GROUP_WIDTH, GROUP_WIDTH), F32)],
        compiler_params=_cparams(1),
        name="retention",
    )(rq, rk, rv, rg, *consts)


def _rglru_kernel(x_ref, g_ref, cw_ref, cb_ref, wx_ref, bx_ref, wa_ref, ba_ref, lam_ref, o_ref,
                  xbuf, h_ref):
    j = pl.program_id(1)
    tc = RG_TC

    @pl.when(j == 0)
    def _():
        xbuf[0:8, :] = jnp.zeros((8, GROUP_WIDTH), F32)
        h_ref[...] = jnp.zeros_like(h_ref)

    xbuf[8:8 + tc, :] = x_ref[0]
    xc = cb_ref[...] + cw_ref[RG_CONV - 1:RG_CONV, :] * xbuf[8:8 + tc, :]
    for i in range(RG_CONV - 1):
        off = 8 - (RG_CONV - 1) + i
        xc = xc + cw_ref[i:i + 1, :] * xbuf[off:off + tc, :]
    xbuf[0:8, :] = xbuf[tc:tc + 8, :]

    xcb = xc.astype(BF16)
    gate_x = jax.nn.sigmoid(_dot(xcb, wx_ref[...]) + bx_ref[...])
    gate_a = jax.nn.sigmoid(_dot(xcb, wa_ref[...]) + ba_ref[...])
    lam = lam_ref[...]
    softplus_neg = jnp.maximum(-lam, 0.0) + jnp.log1p(jnp.exp(-jnp.abs(lam)))
    log_a = -RG_C * gate_a * softplus_neg
    a = jnp.exp(log_a)
    th = jnp.tanh(log_a)
    b = jnp.sqrt(-2.0 * th / (1.0 - th)) * (gate_x * xc)

    row = lax.broadcasted_iota(I32, (tc, GROUP_WIDTH), 0)
    d = 1
    while d < tc:
        keep = row >= d
        a_sh = jnp.where(keep, pltpu.roll(a, d, 0), 1.0)
        b_sh = jnp.where(keep, pltpu.roll(b, d, 0), 0.0)
        b = a * b_sh + b
        a = a * a_sh
        d *= 2
    h = b + a * h_ref[...]
    h_ref[...] = h[tc - 1:tc, :]

    xg = g_ref[0]
    gelu = 0.5 * xg * (1.0 + jnp.tanh(np.sqrt(2.0 / np.pi) * (xg + 0.044715 * xg * xg * xg)))
    o_ref[0] = (h * gelu).astype(o_ref.dtype)


def _block_diag(w):
    n, c, _ = w.shape
    eye = jnp.eye(n, dtype=w.dtype)
    return (eye[:, None, :, None] * w[:, :, None, :]).reshape(n * c, n * c)


def _rglru(cx, cg, conv_w, conv_b, wx, bx, wa, ba, lam):
    B, T, _ = cx.shape
    tc = RG_TC
    blk = pl.BlockSpec((1, tc, GROUP_WIDTH), lambda b, j: (b, j, 0))
    const = lambda a: pl.BlockSpec(a.shape, lambda b, j: (0,) * a.ndim)
    consts = [conv_w, conv_b, wx, bx, wa, ba, lam]
    return pl.pallas_call(
        _rglru_kernel,
        grid=(B, T // tc),
        in_specs=[blk, blk] + [const(a) for a in consts],
        out_specs=blk,
        out_shape=jax.ShapeDtypeStruct((B, T, GROUP_WIDTH), BF16),
        scratch_shapes=[pltpu.VMEM((tc + 8, GROUP_WIDTH), F32), pltpu.VMEM((1, GROUP_WIDTH), F32)],
        compiler_params=_cparams(2),
        name="rg_lru",
    )(cx, cg, *consts)


def _dsa_kernel(q_ref, k_ref, v_ref, qi_ref, ki_ref, wt_ref, o_ref,
                sc_ref, jcut_ref, m_ref, l_ref, acc_ref, *, n_sel, n_keys):
    j = pl.program_id(1)
    R = DSA_Q
    KB = DSA_KEY_BLOCK
    SR = GROUP_HEADS * R
    nkb = (j * R + R + KB - 1) // KB
    nsel = float(n_sel)

    rowk = lax.broadcasted_iota(I32, (KB, R), 0)
    qpos = j * R + lax.broadcasted_iota(I32, (KB, R), 1)

    qi = qi_ref[0]
    qi_stack = jnp.concatenate([qi[:, h * IDX_DIM:(h + 1) * IDX_DIM] for h in range(IDX_HEADS)], axis=0)
    wt = wt_ref[...]

    def score_body(kb, c):
        st = pl.multiple_of(kb * KB, KB)
        rel = _dot_nt(ki_ref[0, pl.ds(st, KB), :], qi_stack)
        sc = wt[0:1, :] * jnp.maximum(rel[:, 0:R], 0.0)
        for h in range(1, IDX_HEADS):
            sc = sc + wt[h:h + 1, :] * jnp.maximum(rel[:, h * R:(h + 1) * R], 0.0)
        sc_ref[pl.ds(st, KB), :] = jnp.where(st + rowk <= qpos, sc * IDX_SCALE, NEG)
        return c

    lax.fori_loop(0, nkb, score_body, 0)

    def blocks(fn, init):
        def body(kb, c):
            st = pl.multiple_of(kb * KB, KB)
            return fn(st, sc_ref[pl.ds(st, KB), :], c)
        return lax.fori_loop(0, nkb, body, init)

    def minmax(st, s, c):
        mn, mx = c
        return (jnp.minimum(mn, _fold_rows(jnp.where(s > 0.5 * NEG, s, BIG), jnp.min)),
                jnp.maximum(mx, _fold_rows(s, jnp.max)))

    mn8, mx8 = blocks(minmax, (jnp.full((FOLD_ROWS, R), BIG, F32), jnp.full((FOLD_ROWS, R), NEG, F32)))
    mn = jnp.min(mn8, axis=0, keepdims=True)
    mx = jnp.max(mx8, axis=0, keepdims=True)

    def count_ge(th):
        acc = blocks(lambda st, s, c: c + _fold_rows(jnp.where(s >= th, 1.0, 0.0), jnp.sum),
                     jnp.zeros((FOLD_ROWS, R), F32))
        return jnp.sum(acc, axis=0, keepdims=True)

    n_adm = (j * R + 1 + lax.broadcasted_iota(I32, (1, R), 1)).astype(F32)
    need = n_adm > nsel
    c_max = count_ge(mx)
    top_tie = need & (c_max >= nsel)
    lo0 = jnp.where(need, jnp.where(top_tie, mx, mn), 0.5 * NEG)
    cgt0 = jnp.where(top_tie, 0.0, c_max)
    act0 = jnp.where(need & jnp.logical_not(top_tie), 1.0, 0.0)
    tie0 = jnp.where(top_tie, 1.0, 0.0)

    def to_key(f):
        b = lax.bitcast_convert_type(f, I32)
        return b ^ (lax.shift_right_arithmetic(b, 31) & 0x7FFFFFFF)

    def from_key(kk):
        return lax.bitcast_convert_type(kk ^ (lax.shift_right_arithmetic(kk, 31) & 0x7FFFFFFF), F32)

    def bis_step(klo, khi, cgt, act, tie):
        on = act > 0.0
        kmid = lax.shift_right_arithmetic(klo, 1) + lax.shift_right_arithmetic(khi, 1) + (klo & khi & 1)
        stuck = kmid == klo
        cnt = count_ge(from_key(kmid))
        go = on & jnp.logical_not(stuck)
        up = go & (cnt >= nsel)
        dn = go & (cnt < nsel)
        return (jnp.where(up, kmid, klo), jnp.where(dn, kmid, khi), jnp.where(dn, cnt, cgt),
                jnp.where(go & (cnt != nsel), 1.0, 0.0), jnp.where(on & stuck, 1.0, tie))

    def bis_cond(c):
        return (c[1] > 0.0) & (c[0] < 40)

    def bis_body(c):
        flag = jnp.max(c[5])
        st = bis_step(*bis_step(*c[2:]))
        return (c[0] + 2, flag) + st

    res = lax.while_loop(bis_cond, bis_body,
                         (jnp.int32(0), jnp.max(act0), to_key(lo0), to_key(mx), cgt0, act0, tie0))
    lo, cgt, tie = from_key(res[2]), res[4], res[6]

    jcut_ref[...] = jnp.full((1, R), float(n_keys), F32)

    @pl.when(jnp.max(tie) > 0.0)
    def _():
        want = nsel - cgt
        tied = tie > 0.0

        def jb(it, c):
            a, b = c
            mid = jnp.floor((a + b) * 0.5)
            hit8 = blocks(lambda st, s, cc: cc + _fold_rows(
                jnp.where((s == lo) & ((st + rowk).astype(F32) <= mid), 1.0, 0.0), jnp.sum),
                jnp.zeros((FOLD_ROWS, R), F32))
            ok = jnp.sum(hit8, axis=0, keepdims=True) >= want
            return jnp.where(ok, a, mid), jnp.where(ok, mid, b)

        _, b = lax.fori_loop(0, int(np.ceil(np.log2(n_keys))) + 1, jb,
                             (jnp.full((1, R), -1.0, F32), jnp.full((1, R), float(n_keys - 1), F32)))
        jcut_ref[...] = jnp.where(tied, b, float(n_keys))

    q_stack = _head_stack((q_ref[0].astype(F32) * (HEAD_DIM ** -0.5)).astype(BF16))
    m_ref[...] = jnp.full(m_ref.shape, M_FLOOR, F32)
    l_ref[...] = jnp.zeros(l_ref.shape, F32)
    acc_ref[...] = jnp.zeros(acc_ref.shape, F32)
    jcut = jcut_ref[...]

    def att_body(kb, c):
        st = pl.multiple_of(kb * KB, KB)
        sc = sc_ref[pl.ds(st, KB), :]
        keep = (sc > lo) | ((sc == lo) & ((st + rowk).astype(F32) <= jcut))
        bias = jnp.where(keep, 0.0, NEG)
        s = _dot_nt(k_ref[0, pl.ds(st, KB), :], q_stack) + jnp.concatenate([bias] * GROUP_HEADS, axis=1)
        m_old = m_ref[...]
        m_new = jnp.maximum(m_old, jnp.max(s, axis=0, keepdims=True))
        alpha = jnp.exp(m_old - m_new)
        p = jnp.exp(s - m_new)
        l_ref[...] = alpha * l_ref[...] + jnp.sum(p, axis=0, keepdims=True)
        acc_ref[...] = alpha * acc_ref[...] + _dot_tn(v_ref[0, pl.ds(st, KB), :], p.astype(BF16))
        m_ref[...] = m_new
        return c

    lax.fori_loop(0, nkb, att_body, 0)
    o_ref[0] = _head_unstack((acc_ref[...] / l_ref[...]).T, R).astype(o_ref.dtype)


def _dsa(q, k, v, qi, ki, wt):
    B, T, _ = q.shape
    KB = DSA_KEY_BLOCK
    assert T % KB == 0
    n_sel = min(DSA_TOPK, T // 4)
    R = DSA_Q
    assert T % R == 0
    SR = GROUP_HEADS * R
    nq = T // R
    qblk = lambda wd: pl.BlockSpec((1, R, wd), lambda b, j: (b, j, 0))
    full = lambda wd: pl.BlockSpec((1, T, wd), lambda b, j: (b, 0, 0))
    return pl.pallas_call(
        functools.partial(_dsa_kernel, n_sel=n_sel, n_keys=T),
        grid=(B, nq),
        in_specs=[qblk(GROUP_WIDTH), full(GROUP_WIDTH), full(GROUP_WIDTH),
                  qblk(IDX_HEADS * IDX_DIM), full(IDX_DIM),
                  pl.BlockSpec((IDX_HEADS, R), lambda b, j: (0, b * nq + j))],
        out_specs=qblk(GROUP_WIDTH),
        out_shape=jax.ShapeDtypeStruct((B, T, GROUP_WIDTH), BF16),
        scratch_shapes=[pltpu.VMEM((T, R), F32), pltpu.VMEM((1, R), F32),
                        pltpu.VMEM((1, SR), F32), pltpu.VMEM((1, SR), F32),
                        pltpu.VMEM((GROUP_WIDTH, SR), F32)],
        compiler_params=_cparams(2),
        name="dsa_attention",
    )(q, k, v, qi, ki, wt)


def _outproj_kernel(oa, orr, oc, od, x_ref, w_ref, g_ref, b_ref, rw_ref, rb_ref, ltri_ref,
                    x1_ref, ti_ref, tg_ref, rk_ref, cnt_ref, run_ref):
    GW = GROUP_WIDTH

    @pl.when(pl.program_id(0) == 0)
    def _():
        run_ref[...] = jnp.zeros_like(run_ref)

    acc = _dot(oa[...], w_ref[0:GW, :])
    acc = acc + _dot(orr[...], w_ref[GW:2 * GW, :])
    acc = acc + _dot(oc[...], w_ref[2 * GW:3 * GW, :])
    acc = acc + _dot(od[...], w_ref[3 * GW:4 * GW, :])
    x1 = _layer_norm_rows(ALPHA * x_ref[...] + acc, g_ref[...], b_ref[...])
    x1_ref[...] = x1

    logits = _dot(x1.astype(BF16), rw_ref[...]) + rb_ref[...]
    col = lax.broadcasted_iota(I32, logits.shape, 1)
    kcol = lax.broadcasted_iota(I32, ti_ref.shape, 1)
    g = logits
    ti = jnp.zeros(ti_ref.shape, I32)
    tv = jnp.zeros(tg_ref.shape, F32)
    picks = []
    for kk in range(TOP_K):
        mx = jnp.max(g, axis=1, keepdims=True)
        first = jnp.min(jnp.where(g == mx, col, N_EXPERTS), axis=1, keepdims=True)
        ti = jnp.where(kcol == kk, first, ti)
        tv = jnp.where(kcol == kk, mx, tv)
        picks.append(col == first)
        g = jnp.where(picks[-1], -jnp.inf, g)
    e = jnp.exp(tv - jnp.max(tv, axis=1, keepdims=True))
    ti_ref[...] = ti
    tg_ref[...] = e / jnp.sum(e, axis=1, keepdims=True)

    sel = jnp.where(picks[0] | picks[1] | picks[2] | picks[3], 1.0, 0.0)
    before = run_ref[...] + _dot(ltri_ref[...], sel.astype(BF16))
    rk = jnp.zeros(rk_ref.shape, F32)
    for kk in range(TOP_K):
        rk = jnp.where(kcol == kk, jnp.sum(jnp.where(picks[kk], before, 0.0), axis=1, keepdims=True), rk)
    rk_ref[...] = rk.astype(I32)
    run_ref[...] = run_ref[...] + jnp.sum(sel, axis=0, keepdims=True)
    cnt_ref[...] = run_ref[...]


def _outproj(oa, orr, oc, od, x2, w_out, g, b, rw, rb):
    N = x2.shape[0]
    tm = OUT_TM
    row = lambda w: pl.BlockSpec((tm, w), lambda i: (i, 0))
    const = lambda a: pl.BlockSpec(a.shape, lambda i: (0,) * a.ndim)
    ltri = jnp.asarray(np.tril(np.ones((tm, tm), np.float32), -1), dtype=BF16)
    return pl.pallas_call(
        _outproj_kernel,
        grid=(N // tm,),
        in_specs=[row(GROUP_WIDTH)] * 4 + [row(D_MODEL), const(w_out), const(g), const(b), const(rw), const(rb),
                                           const(ltri)],
        out_specs=[row(D_MODEL), row(TOP_K), row(TOP_K), row(TOP_K),
                   pl.BlockSpec((1, N_EXPERTS), lambda i: (0, 0))],
        out_shape=[jax.ShapeDtypeStruct((N, D_MODEL), F32), jax.ShapeDtypeStruct((N, TOP_K), I32),
                   jax.ShapeDtypeStruct((N, TOP_K), F32), jax.ShapeDtypeStruct((N, TOP_K), I32),
                   jax.ShapeDtypeStruct((1, N_EXPERTS), F32)],
        scratch_shapes=[pltpu.VMEM((1, N_EXPERTS), F32)],
        compiler_params=_cparams(1),
        name="outproj_ln_router",
    )(oa, orr, oc, od, x2, w_out, g, b, rw, rb, ltri)


def _dispatch_kernel(tv_ref, pos_ref, x_ref, xs_hbm, xbuf, sem, zsem, *, n_tiles):
    i = pl.program_id(0)
    n = pl.num_programs(0)
    tm = DSP_TM
    par = i % 2

    @pl.when(i == 0)
    def _():
        xbuf[1] = jnp.zeros((tm, D_MODEL), F32)

        def fill(t, c):
            @pl.when(tv_ref[t] < MOE_TM)
            def _():
                pltpu.make_async_copy(xbuf.at[1], xs_hbm.at[pl.ds(pl.multiple_of(t * MOE_TM, MOE_TM), MOE_TM)],
                                      zsem).start()
            return c

        def drain(t, c):
            @pl.when(tv_ref[t] < MOE_TM)
            def _():
                pltpu.make_async_copy(xbuf.at[1], xs_hbm.at[pl.ds(0, MOE_TM)], zsem).wait()
            return c

        lax.fori_loop(0, n_tiles, fill, 0)
        lax.fori_loop(0, n_tiles, drain, 0)

    def wait_step(p):
        for _ in range(TOP_K):
            pltpu.make_async_copy(xbuf.at[p], xbuf.at[p], sem.at[p]).wait()

    @pl.when(i >= 2)
    def _():
        wait_step(par)

    xbuf[par] = x_ref[...]

    def body(r, c):
        for kk in range(TOP_K):
            pltpu.make_async_copy(xbuf.at[par, pl.ds(r, 1)], xs_hbm.at[pl.ds(pos_ref[r * TOP_K + kk], 1)],
                                  sem.at[par]).start()
        return c

    lax.fori_loop(0, tm, body, 0, unroll=4)

    @pl.when(i == n - 1)
    def _():
        wait_step(1 - par)
        wait_step(par)


def _dispatch(pos, x1, tile_valid):
    N = x1.shape[0]
    tm = DSP_TM
    n_tiles = tile_valid.shape[0]
    assert N // tm >= 2 and tm == MOE_TM
    grid_spec = pltpu.PrefetchScalarGridSpec(
        num_scalar_prefetch=1,
        grid=(N // tm,),
        in_specs=[pl.BlockSpec((tm * TOP_K,), lambda i, tv: (i,), memory_space=pltpu.SMEM),
                  pl.BlockSpec((tm, D_MODEL), lambda i, tv: (i, 0))],
        out_specs=pl.BlockSpec(memory_space=pl.ANY),
        scratch_shapes=[pltpu.VMEM((2, tm, D_MODEL), F32), pltpu.SemaphoreType.DMA((2,)),
                        pltpu.SemaphoreType.DMA(())],
    )
    return pl.pallas_call(
        functools.partial(_dispatch_kernel, n_tiles=n_tiles),
        grid_spec=grid_spec,
        out_shape=jax.ShapeDtypeStruct((n_tiles * MOE_TM, D_MODEL), F32),
        compiler_params=_cparams(1),
        name="moe_dispatch",
    )(tile_valid, pos, x1)


def _moe_kernel(te_ref, nv_ref, x_ref, w1_ref, b1_ref, w2_ref, b2_ref, y_ref, w1b, w2b):
    i = pl.program_id(0)
    tm = MOE_TM
    n_valid = nv_ref[i]

    @pl.when((i == 0) | (te_ref[i] != te_ref[jnp.maximum(i - 1, 0)]))
    def _():
        step = 128
        for c in range(D_MODEL // step):
            w1b[c * step:(c + 1) * step, :] = w1_ref[0, 0, c * step:(c + 1) * step, :].astype(BF16)
        for c in range(D_FF // step):
            w2b[c * step:(c + 1) * step, :] = w2_ref[0, 0, c * step:(c + 1) * step, :].astype(BF16)

    @pl.when(n_valid > 0)
    def _():
        h = _dot(x_ref[...].astype(BF16), w1b[...]) + b1_ref[0]
        glu_in = jnp.minimum(h[:, :D_FF], SWIGLU_LIMIT)
        up = jnp.clip(h[:, D_FF:], -SWIGLU_LIMIT, SWIGLU_LIMIT)
        glu = glu_in * jax.nn.sigmoid(SWIGLU_ALPHA * glu_in)
        y_ref[...] = _dot(((up + 1.0) * glu).astype(BF16), w2b[...]) + b2_ref[0]

    @pl.when(n_valid == 0)
    def _():
        y_ref[...] = jnp.zeros_like(y_ref)


def _moe_experts(xs, tile_expert, tile_valid, w1, b1, w2, b2, layer):
    tm = MOE_TM
    n_tiles = tile_expert.shape[0]
    grid_spec = pltpu.PrefetchScalarGridSpec(
        num_scalar_prefetch=2,
        grid=(n_tiles,),
        in_specs=[
            pl.BlockSpec((tm, D_MODEL), lambda i, te, nv: (i, 0)),
            pl.BlockSpec((1, 1, D_MODEL, 2 * D_FF), lambda i, te, nv: (layer, te[i], 0, 0)),
            pl.BlockSpec((1, 1, 2 * D_FF), lambda i, te, nv: (te[i], 0, 0)),
            pl.BlockSpec((1, 1, D_FF, D_MODEL), lambda i, te, nv: (layer, te[i], 0, 0)),
            pl.BlockSpec((1, 1, D_MODEL), lambda i, te, nv: (te[i], 0, 0)),
        ],
        out_specs=pl.BlockSpec((tm, D_MODEL), lambda i, te, nv: (i, 0)),
        scratch_shapes=[pltpu.VMEM((D_MODEL, 2 * D_FF), BF16), pltpu.VMEM((D_FF, D_MODEL), BF16)],
    )
    return pl.pallas_call(
        _moe_kernel,
        grid_spec=grid_spec,
        out_shape=jax.ShapeDtypeStruct((n_tiles * tm, D_MODEL), F32),
        compiler_params=_cparams(1),
        name="moe_experts",
    )(tile_expert, tile_valid, xs, w1, b1, w2, b2)


def _combine_kernel(pos_ref, posn_ref, y_hbm, x1_ref, tg_ref, g_ref, b_ref, o_ref, ybuf, sem):
    i = pl.program_id(0)
    n = pl.num_programs(0)
    tm = CMB_TM
    slot = i % 2

    def issue(idx_ref, s):
        def body(r, c):
            for kk in range(TOP_K):
                pltpu.make_async_copy(y_hbm.at[pl.ds(idx_ref[r * TOP_K + kk], 1)],
                                      ybuf.at[s, kk, pl.ds(r, 1)], sem.at[s]).start()
            return c
        lax.fori_loop(0, tm, body, 0, unroll=4)

    @pl.when(i == 0)
    def _():
        issue(pos_ref, 0)

    @pl.when(i + 1 < n)
    def _():
        issue(posn_ref, 1 - slot)

    pltpu.make_async_copy(ybuf.at[slot], ybuf.at[slot], sem.at[slot]).wait()
    tg = tg_ref[...]
    moe = tg[:, 0:1] * ybuf[slot, 0]
    for kk in range(1, TOP_K):
        moe = moe + tg[:, kk:kk + 1] * ybuf[slot, kk]
    o_ref[...] = _layer_norm_rows(ALPHA * x1_ref[...] + moe, g_ref[...], b_ref[...])


def _combine(pos, y_sorted, x1, tg, g, b):
    N = x1.shape[0]
    tm = CMB_TM
    n = N // tm
    row = lambda w: pl.BlockSpec((tm, w), lambda i: (i, 0))
    const = lambda a: pl.BlockSpec(a.shape, lambda i: (0,) * a.ndim)
    return pl.pallas_call(
        _combine_kernel,
        grid=(n,),
        in_specs=[pl.BlockSpec((tm * TOP_K,), lambda i: (i,), memory_space=pltpu.SMEM),
                  pl.BlockSpec((tm * TOP_K,), lambda i: (jnp.minimum(i + 1, n - 1),), memory_space=pltpu.SMEM),
                  pl.BlockSpec(memory_space=pl.ANY), row(D_MODEL), row(TOP_K), const(g), const(b)],
        out_specs=row(D_MODEL),
        out_shape=jax.ShapeDtypeStruct((N, D_MODEL), F32),
        scratch_shapes=[pltpu.VMEM((2, TOP_K, tm, D_MODEL), F32), pltpu.SemaphoreType.DMA((2,))],
        compiler_params=_cparams(1),
        name="moe_combine_ln",
    )(pos, pos, y_sorted, x1, tg, g, b)


def _routing_tables(top_i, rank, counts_f, n_tiles):
    tm = MOE_TM
    counts = counts_f.reshape(-1).astype(I32)
    padded = ((counts + tm - 1) // tm) * tm
    ends = jnp.cumsum(padded)
    offsets = ends - padded
    onehot = top_i[:, :, None] == jnp.arange(N_EXPERTS, dtype=I32)[None, None, :]
    pos = jnp.sum(jnp.where(onehot, offsets[None, None, :], 0), axis=-1) + rank
    tile_start = jnp.arange(n_tiles, dtype=I32) * tm
    tile_expert = jnp.sum((ends[None, :] <= tile_start[:, None]).astype(I32), axis=1)
    tile_expert = jnp.minimum(tile_expert, N_EXPERTS - 1)
    n_used = ends[-1] // tm
    last_expert = tile_expert[jnp.maximum(n_used - 1, 0)]
    tile_expert = jnp.where(tile_start < ends[-1], tile_expert, last_expert)
    valid_end = (offsets + counts)[tile_expert]
    tile_valid = jnp.clip(valid_end - tile_start, 0, tm)
    return pos.reshape(-1).astype(I32), tile_expert.astype(I32), tile_valid.astype(I32)


def _rope_tables(T):
    inv = ROPE_THETA ** (-jnp.arange(0, HEAD_DIM, 2, dtype=F32) / HEAD_DIM)
    ang = jnp.arange(T, dtype=F32)[:, None] * inv[None, :]
    cos, sin = jnp.cos(ang), jnp.sin(ang)
    cos_t = jnp.tile(jnp.concatenate([cos, cos], axis=-1), (1, GROUP_HEADS))
    sin_t = jnp.tile(jnp.concatenate([-sin, sin], axis=-1), (1, GROUP_HEADS))
    return cos_t, sin_t


def _pad_w_in(w_in):
    base = 12 * GROUP_WIDTH + IDX_HEADS * IDX_DIM
    w = jnp.zeros((D_MODEL, IN_PAD), F32)
    w = w.at[:, :base + IDX_DIM].set(w_in[:, :base + IDX_DIM])
    w = w.at[:, base + 128:base + 128 + IDX_HEADS].set(w_in[:, base + IDX_DIM:])
    return w.astype(BF16)


def _layer(x2, B, T, cos_t, sin_t, tabs, w_in, ret_gn_g, ret_gn_b, conv_w, conv_b, rg_wx, rg_bx, rg_wa,
           rg_ba, rg_lambda, w_out, ln1_g, ln1_b, router_w, router_b, exp_w1, exp_b1, exp_w2, exp_b2,
           ln2_g, ln2_b, layer):
    N = B * T
    r2 = lambda a: a.reshape(1, -1)
    (aq, ak, av, rq, rk, rv, rg, cx, cg, dq, dk, dv, dqi, dki, dwt) = _proj(x2, _pad_w_in(w_in), cos_t, sin_t, T)
    seq = lambda a: a.reshape(B, T, a.shape[-1])
    o_a = _moba(seq(aq), seq(ak), seq(av))
    o_r = _retention(seq(rq), seq(rk), seq(rv), seq(rg), r2(ret_gn_g), r2(ret_gn_b), tabs)
    o_c = _rglru(seq(cx), seq(cg), conv_w, r2(conv_b), _block_diag(rg_wx).astype(BF16), r2(rg_bx),
                 _block_diag(rg_wa).astype(BF16), r2(rg_ba), r2(rg_lambda))
    o_d = _dsa(seq(dq), seq(dk), seq(dv), seq(dqi), seq(dki), dwt)
    flat = lambda a: a.reshape(N, GROUP_WIDTH)
    x1, top_i, top_g, rank, counts = _outproj(flat(o_a), flat(o_r), flat(o_c), flat(o_d), x2,
                                              w_out.astype(BF16), r2(ln1_g), r2(ln1_b),
                                              router_w.astype(BF16), r2(router_b))
    n_tiles = (N * TOP_K) // MOE_TM + N_EXPERTS
    pos, tile_expert, tile_valid = _routing_tables(top_i, rank, counts, n_tiles)
    xs = _dispatch(pos, x1, tile_valid)
    y_sorted = _moe_experts(xs, tile_expert, tile_valid, exp_w1, exp_b1.reshape(N_EXPERTS, 1, -1), exp_w2,
                            exp_b2.reshape(N_EXPERTS, 1, -1), layer)
    return _combine(pos, y_sorted, x1, top_g, r2(ln2_g), r2(ln2_b))


def kernel(x, w_in, ret_gn_g, ret_gn_b, conv_w, conv_b, rg_wx, rg_bx, rg_wa, rg_ba, rg_lambda, w_out,
           ln1_g, ln1_b, router_w, router_b, exp_w1, exp_b1, exp_w2, exp_b2, ln2_g, ln2_b):
    B, T, D = x.shape
    cos_t, sin_t = _rope_tables(T)
    tabs = _ret_tables()
    x2 = x.reshape(B * T, D)
    for l in range(w_in.shape[0]):
        x2 = _layer(x2, B, T, cos_t, sin_t, tabs, w_in[l], ret_gn_g[l], ret_gn_b[l], conv_w[l], conv_b[l],
                    rg_wx[l], rg_bx[l], rg_wa[l], rg_ba[l], rg_lambda[l], w_out[l], ln1_g[l], ln1_b[l],
                    router_w[l], router_b[l], exp_w1, exp_b1[l], exp_w2, exp_b2[l], ln2_g[l], ln2_b[l], l)
    return x2.reshape(B, T, D)
```

```python
import functools

import numpy as np
import jax
import jax.numpy as jnp
from jax import lax
from jax.experimental import pallas as pl
from jax.experimental.pallas import tpu as pltpu

F32 = jnp.float32
BF16 = jnp.bfloat16
I32 = jnp.int32

D_MODEL = 1024
DEPTH = 2
HEAD_DIM = 64
GROUP_WIDTH = 256
GROUP_HEADS = 4
ROPE_THETA = 10000.0
Q_BLOCK = 128
MOBA_BLOCK = 256
MOBA_TOPK = 3
MOBA_MAX_BLOCKS = 16
RET_CHUNK = 128
RG_CONV = 4
RG_C = 8.0
IDX_HEADS = 8
IDX_DIM = 64
IDX_SCALE = (IDX_HEADS ** -0.5) * (IDX_DIM ** -0.5)
DSA_TOPK = 256
DSA_Q = 256
DSA_KEY_BLOCK = 512
DSA_ATT_BLOCK = 512
N_EXPERTS = 32
TOP_K = 4
D_FF = 1024
SWIGLU_LIMIT = 7.0
SWIGLU_ALPHA = 1.702
ALPHA = (2 * DEPTH) ** 0.25
LN_EPS = 1e-5
IN_WIDTH = 12 * GROUP_WIDTH + IDX_HEADS * IDX_DIM + IDX_DIM + IDX_HEADS
IN_PAD = 15 * GROUP_WIDTH

NEG = -1e30
M_FLOOR = -1e29
BIG = 1e30
VMEM_LIMIT = 56 * 1024 * 1024
FOLD_ROWS = 32

PROJ_TM = 256
OUT_TM = 512
RG_TC = 256
MOE_TM = 512
DSP_TM = 512
CMB_TM = 256


def _cparams(ndims):
    return pltpu.CompilerParams(dimension_semantics=("arbitrary",) * ndims,
                                vmem_limit_bytes=VMEM_LIMIT)


def _dot(a, b, precision=None):
    return jnp.dot(a, b, preferred_element_type=F32, precision=precision)


def _dot_nt(a, b, precision=None):
    return lax.dot_general(a, b, (((1,), (1,)), ((), ())), preferred_element_type=F32,
                           precision=precision)


def _dot_tn(a, b):
    return lax.dot_general(a, b, (((0,), (0,)), ((), ())), preferred_element_type=F32)


def _head_stack(q):
    head = lax.shift_right_logical(lax.broadcasted_iota(I32, q.shape, 1), 6)
    qf = q.astype(F32)
    return jnp.concatenate([jnp.where(head == h, qf, 0.0) for h in range(GROUP_HEADS)],
                           axis=0).astype(q.dtype)


def _head_unstack(s, rows):
    head = lax.shift_right_logical(lax.broadcasted_iota(I32, (rows, GROUP_WIDTH), 1), 6)
    out = jnp.zeros((rows, GROUP_WIDTH), F32)
    for h in range(GROUP_HEADS):
        out = out + jnp.where(head == h, s[h * rows:(h + 1) * rows], 0.0)
    return out


def _fold_rows(x, op):
    return op(x.reshape(x.shape[0] // FOLD_ROWS, FOLD_ROWS, x.shape[1]), axis=0)


def _layer_norm_rows(y, g, b):
    mu = jnp.mean(y, axis=-1, keepdims=True)
    yc = y - mu
    var = jnp.mean(yc * yc, axis=-1, keepdims=True)
    return yc * lax.rsqrt(var + LN_EPS) * g + b


def _proj_kernel(x_ref, w_ref, cos_ref, sin_ref,
                 aq, ak, av, rq, rk, rv, rg, cx, cg, dq, dk, dv, dqi, dki, dwt):
    xb = x_ref[...].astype(BF16)
    cos = cos_ref[...]
    sin = sin_ref[...]
    first_half = (lax.broadcasted_iota(I32, cos.shape, 1) & (HEAD_DIM - 1)) < (HEAD_DIM // 2)

    def seg(i):
        return _dot(xb, w_ref[:, i * GROUP_WIDTH:(i + 1) * GROUP_WIDTH])

    def rope(p):
        rot = jnp.where(first_half, pltpu.roll(p, GROUP_WIDTH - HEAD_DIM // 2, 1),
                        pltpu.roll(p, HEAD_DIM // 2, 1))
        return p * cos + rot * sin

    aq[...] = rope(seg(0)).astype(BF16)
    ak[...] = rope(seg(1)).astype(BF16)
    av[...] = seg(2).astype(BF16)
    rq[...] = rope(seg(3)).astype(BF16)
    rk[...] = (rope(seg(4)) * (HEAD_DIM ** -0.5)).astype(BF16)
    rv[...] = seg(5).astype(BF16)
    rg[...] = seg(6)
    cx[...] = seg(7)
    cg[...] = seg(8)
    dq[...] = rope(seg(9)).astype(BF16)
    dk[...] = rope(seg(10)).astype(BF16)
    dv[...] = seg(11).astype(BF16)
    dqi[:, 0:GROUP_WIDTH] = rope(seg(12)).astype(BF16)
    dqi[:, GROUP_WIDTH:2 * GROUP_WIDTH] = rope(seg(13)).astype(BF16)
    last = seg(14)
    dki[...] = rope(last)[:, 0:IDX_DIM].astype(BF16)
    dwt[...] = last[:, 128:256].T[0:IDX_HEADS, :]


def _proj(x2, w_pad, cos_t, sin_t, T):
    N = x2.shape[0]
    tm = PROJ_TM
    tpb = T // tm
    row = lambda w: pl.BlockSpec((tm, w), lambda i: (i, 0))
    tab = pl.BlockSpec((tm, GROUP_WIDTH), lambda i: (i % tpb, 0))
    widths = [256] * 12 + [512, IDX_DIM]
    dtypes = [BF16, BF16, BF16, BF16, BF16, BF16, F32, F32, F32, BF16, BF16, BF16, BF16, BF16]
    return pl.pallas_call(
        _proj_kernel,
        grid=(N // tm,),
        in_specs=[row(D_MODEL), pl.BlockSpec((D_MODEL, IN_PAD), lambda i: (0, 0)), tab, tab],
        out_specs=[row(w) for w in widths] + [pl.BlockSpec((IDX_HEADS, tm), lambda i: (0, i))],
        out_shape=[jax.ShapeDtypeStruct((N, w), d) for w, d in zip(widths, dtypes)]
        + [jax.ShapeDtypeStruct((IDX_HEADS, N), F32)],
        compiler_params=_cparams(1),
        name="proj_rope",
    )(x2, w_pad, cos_t, sin_t)


def _moba_kernel(q_ref, k_ref, v_ref, o_ref, kmean_ref, sel_ref, qk_ref, m_ref, l_ref, acc_ref, *, n_blocks):
    j = pl.program_id(1)
    R = Q_BLOCK
    SR = GROUP_HEADS * R
    KB = MOBA_BLOCK

    @pl.when(j == 0)
    def _():
        kmean_ref[...] = jnp.zeros_like(kmean_ref)
        for n in range(n_blocks):
            kb = k_ref[0, n * KB:(n + 1) * KB, :].astype(F32)
            kmean_ref[n:n + 1, :] = jnp.mean(kb, axis=0, keepdims=True)

    own = j // (KB // R)
    q_raw = _head_stack(q_ref[0])
    q_stack = (q_raw.astype(F32) * (HEAD_DIM ** -0.5)).astype(BF16)

    gate = _dot_nt(kmean_ref[...], q_raw.astype(F32), precision=lax.Precision.HIGHEST)
    blk = lax.broadcasted_iota(I32, gate.shape, 0)
    past = blk < own
    g = jnp.where(past, gate, -jnp.inf)
    sel = jnp.zeros(gate.shape, F32)
    for _ in range(MOBA_TOPK):
        mx = jnp.max(g, axis=0, keepdims=True)
        first = jnp.min(jnp.where(g == mx, blk, MOBA_MAX_BLOCKS), axis=0, keepdims=True)
        pick = blk == first
        sel = jnp.where(pick & past, 1.0, sel)
        g = jnp.where(pick, -jnp.inf, g)
    sel_ref[...] = sel

    m_ref[...] = jnp.full(m_ref.shape, M_FLOOR, F32)
    l_ref[...] = jnp.zeros(l_ref.shape, F32)
    acc_ref[...] = jnp.zeros(acc_ref.shape, F32)

    def softmax_pv(s_raw, vb, bias):
        s = s_raw + bias
        m_old = m_ref[...]
        m_new = jnp.maximum(m_old, jnp.max(s, axis=0, keepdims=True))
        alpha = jnp.exp(m_old - m_new)
        p = jnp.exp(s - m_new)
        l_ref[...] = alpha * l_ref[...] + jnp.sum(p, axis=0, keepdims=True)
        acc_ref[...] = alpha * acc_ref[...] + _dot_tn(vb, p.astype(BF16))
        m_ref[...] = m_new

    def block_bias(n):
        return jnp.broadcast_to(jnp.where(sel_ref[pl.ds(n, 1), :] > 0.5, 0.0, NEG), (KB, SR))

    def qk(first_block, rows):
        st = pl.multiple_of(first_block * KB, KB)
        return _dot_nt(k_ref[0, pl.ds(st, rows), :], q_stack)

    def attend_span(first_block, bias, s_raw=None):
        st = pl.multiple_of(first_block * KB, KB)
        rows = bias.shape[0]
        s_raw = qk(first_block, rows) if s_raw is None else s_raw
        softmax_pv(s_raw, v_ref[0, pl.ds(st, rows), :], bias)

    keypos = own * KB + lax.broadcasted_iota(I32, (KB, SR), 0)
    qpos = j * R + (lax.broadcasted_iota(I32, (KB, SR), 1) & (R - 1))
    causal = jnp.where(keypos <= qpos, 0.0, NEG)
    odd = (own & 1) == 1

    @pl.when(odd)
    def _():
        attend_span(own - 1, jnp.concatenate([block_bias(own - 1), causal], axis=0))

    @pl.when(jnp.logical_not(odd))
    def _():
        attend_span(own, causal)

    n_pairs = own // 2

    @pl.when(n_pairs > 0)
    def _():
        qk_ref[...] = qk(0, 2 * KB)

    def body(pair, c):
        s_raw = qk_ref[...]
        qk_ref[...] = qk(2 * jnp.minimum(pair + 1, n_pairs - 1), 2 * KB)
        attend_span(2 * pair, jnp.concatenate([block_bias(2 * pair), block_bias(2 * pair + 1)], axis=0), s_raw)
        return c

    lax.fori_loop(0, n_pairs, body, 0)
    o_ref[0] = _head_unstack((acc_ref[...] / l_ref[...]).T, R).astype(o_ref.dtype)


def _moba(q, k, v):
    B, T, _ = q.shape
    n_blocks = T // MOBA_BLOCK
    assert T % MOBA_BLOCK == 0 and n_blocks <= MOBA_MAX_BLOCKS
    SR = GROUP_HEADS * Q_BLOCK
    return pl.pallas_call(
        functools.partial(_moba_kernel, n_blocks=n_blocks),
        grid=(B, T // Q_BLOCK),
        in_specs=[pl.BlockSpec((1, Q_BLOCK, GROUP_WIDTH), lambda b, j: (b, j, 0)),
                  pl.BlockSpec((1, T, GROUP_WIDTH), lambda b, j: (b, 0, 0)),
                  pl.BlockSpec((1, T, GROUP_WIDTH), lambda b, j: (b, 0, 0))],
        out_specs=pl.BlockSpec((1, Q_BLOCK, GROUP_WIDTH), lambda b, j: (b, j, 0)),
        out_shape=jax.ShapeDtypeStruct((B, T, GROUP_WIDTH), BF16),
        scratch_shapes=[pltpu.VMEM((MOBA_MAX_BLOCKS, GROUP_WIDTH), F32),
                        pltpu.VMEM((MOBA_MAX_BLOCKS, SR), F32),
                        pltpu.VMEM((2 * MOBA_BLOCK, SR), F32),
                        pltpu.VMEM((1, SR), F32), pltpu.VMEM((1, SR), F32),
                        pltpu.VMEM((GROUP_WIDTH, SR), F32)],
        compiler_params=_cparams(2),
        name="moba_attention",
    )(q, k, v)


def _ret_kernel(q_ref, k_ref, v_ref, g_ref, dmask_ref, xi_ref, zeta_ref, gdec_ref, bd_ref, avg_ref,
                gng_ref, gnb_ref, o_ref, r_ref):
    j = pl.program_id(0)

    @pl.when(j == 0)
    def _():
        r_ref[...] = jnp.zeros_like(r_ref)

    C = RET_CHUNK
    hp = lax.Precision.HIGHEST
    for b in range(q_ref.shape[0]):
        q = q_ref[b]
        k = k_ref[b]
        v = v_ref[b]
        q_stack = _head_stack(q)
        inner = _dot_nt(q_stack, k) * dmask_ref[...]
        o = _head_unstack(_dot(inner.astype(BF16), v), C)
        R = r_ref[b]
        o = o + _dot(q, R.astype(BF16)) * xi_ref[...]
        kz = (k.astype(F32) * zeta_ref[...]).astype(BF16)
        r_ref[b] = gdec_ref[...] * R + bd_ref[...] * _dot_tn(kz, v)

        mu = _dot(o, avg_ref[...], precision=hp)
        oc = o - mu
        var = _dot(oc * oc, avg_ref[...], precision=hp)
        y = oc * lax.rsqrt(var + LN_EPS) * gng_ref[...] + gnb_ref[...]
        gte = g_ref[b]
        o_ref[b] = (y * (gte * jax.nn.sigmoid(gte))).astype(o_ref.dtype)


def _ret_tables():
    H, C, d = GROUP_HEADS, RET_CHUNK, HEAD_DIM
    log_g = np.log(1.0 - 2.0 ** (-5.0 - np.arange(H, dtype=np.float64)))
    n = np.arange(C, dtype=np.float64)
    diff = n[:, None] - n[None, :]
    dmask = np.where(diff >= 0, np.exp(log_g[:, None, None] * np.maximum(diff, 0.0)), 0.0)
    xi = np.exp(log_g[:, None] * (n + 1.0))
    zeta = np.exp(log_g[:, None] * (C - 1.0 - n))
    g_chunk = np.exp(log_g * C)
    head = np.arange(GROUP_WIDTH) // d
    bd = (head[:, None] == head[None, :]).astype(np.float64)
    to32 = lambda a: jnp.asarray(a, dtype=F32)
    return dict(dmask=to32(dmask.reshape(H * C, C)), xi=to32(xi.T[:, head]), zeta=to32(zeta.T[:, head]),
                gdec=to32(bd * g_chunk[head][:, None]), bd=to32(bd), avg=to32(bd / d))


def _retention(rq, rk, rv, rg, gn_g, gn_b, tabs):
    B, T, _ = rq.shape
    C = RET_CHUNK
    blk = pl.BlockSpec((B, C, GROUP_WIDTH), lambda j: (0, j, 0))
    const = lambda a: pl.BlockSpec(a.shape, lambda j: (0,) * a.ndim)
    consts = [tabs["dmask"], tabs["xi"], tabs["zeta"], tabs["gdec"], tabs["bd"], tabs["avg"], gn_g, gn_b]
    return pl.pallas_call(
        _ret_kernel,
        grid=(T // C,),
        in_specs=[blk, blk, blk, blk] + [const(a) for a in consts],
        out_specs=blk,
        out_shape=jax.ShapeDtypeStruct((B, T, GROUP_WIDTH), BF16),
        scratch_shapes=[pltpu.VMEM((B, GROUP_WIDTH, GROUP_WIDTH), F32)],
        compiler_params=_cparams(1),
        name="retention",
    )(rq, rk, rv, rg, *consts)


def _rglru_kernel(x_ref, g_ref, cw_ref, cb_ref, wx_ref, bx_ref, wa_ref, ba_ref, lam_ref, o_ref,
                  xbuf, h_ref):
    j = pl.program_id(1)
    tc = RG_TC

    @pl.when(j == 0)
    def _():
        xbuf[0:8, :] = jnp.zeros((8, GROUP_WIDTH), F32)
        h_ref[...] = jnp.zeros_like(h_ref)

    xbuf[8:8 + tc, :] = x_ref[0]
    xc = cb_ref[...] + cw_ref[RG_CONV - 1:RG_CONV, :] * xbuf[8:8 + tc, :]
    for i in range(RG_CONV - 1):
        off = 8 - (RG_CONV - 1) + i
        xc = xc + cw_ref[i:i + 1, :] * xbuf[off:off + tc, :]
    xbuf[0:8, :] = xbuf[tc:tc + 8, :]

    xcb = xc.astype(BF16)
    gate_x = jax.nn.sigmoid(_dot(xcb, wx_ref[...]) + bx_ref[...])
    gate_a = jax.nn.sigmoid(_dot(xcb, wa_ref[...]) + ba_ref[...])
    lam = lam_ref[...]
    softplus_neg = jnp.maximum(-lam, 0.0) + jnp.log1p(jnp.exp(-jnp.abs(lam)))
    log_a = -RG_C * gate_a * softplus_neg
    a = jnp.exp(log_a)
    th = jnp.tanh(log_a)
    b = jnp.sqrt(-2.0 * th / (1.0 - th)) * (gate_x * xc)

    row = lax.broadcasted_iota(I32, (tc, GROUP_WIDTH), 0)
    d = 1
    while d < tc:
        keep = row >= d
        a_sh = jnp.where(keep, pltpu.roll(a, d, 0), 1.0)
        b_sh = jnp.where(keep, pltpu.roll(b, d, 0), 0.0)
        b = a * b_sh + b
        a = a * a_sh
        d *= 2
    h = b + a * h_ref[...]
    h_ref[...] = h[tc - 1:tc, :]

    xg = g_ref[0]
    gelu = 0.5 * xg * (1.0 + jnp.tanh(np.sqrt(2.0 / np.pi) * (xg + 0.044715 * xg * xg * xg)))
    o_ref[0] = (h * gelu).astype(o_ref.dtype)


def _block_diag(w):
    n, c, _ = w.shape
    eye = jnp.eye(n, dtype=w.dtype)
    return (eye[:, None, :, None] * w[:, :, None, :]).reshape(n * c, n * c)


def _rglru(cx, cg, conv_w, conv_b, wx, bx, wa, ba, lam):
    B, T, _ = cx.shape
    tc = RG_TC
    blk = pl.BlockSpec((1, tc, GROUP_WIDTH), lambda b, j: (b, j, 0))
    const = lambda a: pl.BlockSpec(a.shape, lambda b, j: (0,) * a.ndim)
    consts = [conv_w, conv_b, wx, bx, wa, ba, lam]
    return pl.pallas_call(
        _rglru_kernel,
        grid=(B, T // tc),
        in_specs=[blk, blk] + [const(a) for a in consts],
        out_specs=blk,
        out_shape=jax.ShapeDtypeStruct((B, T, GROUP_WIDTH), BF16),
        scratch_shapes=[pltpu.VMEM((tc + 8, GROUP_WIDTH), F32), pltpu.VMEM((1, GROUP_WIDTH), F32)],
        compiler_params=_cparams(2),
        name="rg_lru",
    )(cx, cg, *consts)


def _dsa_kernel(q_ref, k_ref, v_ref, qi_ref, ki_ref, wt_ref, o_ref,
                sc_ref, jcut_ref, qk_ref, m_ref, l_ref, acc_ref, *, n_sel, n_keys):
    j = pl.program_id(1)
    R = DSA_Q
    KB = DSA_KEY_BLOCK
    SR = GROUP_HEADS * R
    nkb = (j * R + R + KB - 1) // KB
    nsel = float(n_sel)

    rowk = lax.broadcasted_iota(I32, (KB, R), 0)
    qpos = j * R + lax.broadcasted_iota(I32, (KB, R), 1)

    qi = qi_ref[0]
    qi_stack = jnp.concatenate([qi[:, h * IDX_DIM:(h + 1) * IDX_DIM] for h in range(IDX_HEADS)], axis=0)
    wt = wt_ref[...]

    def score_body(kb, c):
        st = pl.multiple_of(kb * KB, KB)
        rel = _dot_nt(ki_ref[0, pl.ds(st, KB), :], qi_stack)
        sc = wt[0:1, :] * jnp.maximum(rel[:, 0:R], 0.0)
        for h in range(1, IDX_HEADS):
            sc = sc + wt[h:h + 1, :] * jnp.maximum(rel[:, h * R:(h + 1) * R], 0.0)
        sc_ref[pl.ds(st, KB), :] = jnp.where(st + rowk <= qpos, sc * IDX_SCALE, NEG)
        return c

    lax.fori_loop(0, nkb, score_body, 0)

    def blocks(fn, init):
        def body(kb, c):
            st = pl.multiple_of(kb * KB, KB)
            return fn(st, sc_ref[pl.ds(st, KB), :], c)
        return lax.fori_loop(0, nkb, body, init)

    def minmax(st, s, c):
        mn, mx = c
        return (jnp.minimum(mn, _fold_rows(jnp.where(s > 0.5 * NEG, s, BIG), jnp.min)),
                jnp.maximum(mx, _fold_rows(s, jnp.max)))

    mn8, mx8 = blocks(minmax, (jnp.full((FOLD_ROWS, R), BIG, F32), jnp.full((FOLD_ROWS, R), NEG, F32)))
    mn = jnp.min(mn8, axis=0, keepdims=True)
    mx = jnp.max(mx8, axis=0, keepdims=True)

    def count_ge(th):
        acc = blocks(lambda st, s, c: c + _fold_rows(jnp.where(s >= th, 1.0, 0.0), jnp.sum),
                     jnp.zeros((FOLD_ROWS, R), F32))
        return jnp.sum(acc, axis=0, keepdims=True)

    n_adm = (j * R + 1 + lax.broadcasted_iota(I32, (1, R), 1)).astype(F32)
    need = n_adm > nsel
    c_max = count_ge(mx)
    top_tie = need & (c_max >= nsel)
    lo0 = jnp.where(need, jnp.where(top_tie, mx, mn), 0.5 * NEG)
    cgt0 = jnp.where(top_tie, 0.0, c_max)
    act0 = jnp.where(need & jnp.logical_not(top_tie), 1.0, 0.0)
    tie0 = jnp.where(top_tie, 1.0, 0.0)

    def to_key(f):
        b = lax.bitcast_convert_type(f, I32)
        return b ^ (lax.shift_right_arithmetic(b, 31) & 0x7FFFFFFF)

    def from_key(kk):
        return lax.bitcast_convert_type(kk ^ (lax.shift_right_arithmetic(kk, 31) & 0x7FFFFFFF), F32)

    def bis_step(klo, khi, cgt, act, tie):
        on = act > 0.0
        kmid = lax.shift_right_arithmetic(klo, 1) + lax.shift_right_arithmetic(khi, 1) + (klo & khi & 1)
        stuck = kmid == klo
        cnt = count_ge(from_key(kmid))
        go = on & jnp.logical_not(stuck)
        up = go & (cnt >= nsel)
        dn = go & (cnt < nsel)
        return (jnp.where(up, kmid, klo), jnp.where(dn, kmid, khi), jnp.where(dn, cnt, cgt),
                jnp.where(go & (cnt != nsel), 1.0, 0.0), jnp.where(on & stuck, 1.0, tie))

    def bis_cond(c):
        return (c[1] > 0.0) & (c[0] < 40)

    def bis_body(c):
        flag = jnp.max(c[5])
        st = bis_step(*bis_step(*c[2:]))
        return (c[0] + 2, flag) + st

    res = lax.while_loop(bis_cond, bis_body,
                         (jnp.int32(0), jnp.max(act0), to_key(lo0), to_key(mx), cgt0, act0, tie0))
    lo, cgt, tie = from_key(res[2]), res[4], res[6]

    jcut_ref[...] = jnp.full((1, R), float(n_keys), F32)

    @pl.when(jnp.max(tie) > 0.0)
    def _():
        want = nsel - cgt
        tied = tie > 0.0

        def jb(it, c):
            a, b = c
            mid = jnp.floor((a + b) * 0.5)
            hit8 = blocks(lambda st, s, cc: cc + _fold_rows(
                jnp.where((s == lo) & ((st + rowk).astype(F32) <= mid), 1.0, 0.0), jnp.sum),
                jnp.zeros((FOLD_ROWS, R), F32))
            ok = jnp.sum(hit8, axis=0, keepdims=True) >= want
            return jnp.where(ok, a, mid), jnp.where(ok, mid, b)

        _, b = lax.fori_loop(0, int(np.ceil(np.log2(n_keys))) + 1, jb,
                             (jnp.full((1, R), -1.0, F32), jnp.full((1, R), float(n_keys - 1), F32)))
        jcut_ref[...] = jnp.where(tied, b, float(n_keys))

    q_stack = _head_stack((q_ref[0].astype(F32) * (HEAD_DIM ** -0.5)).astype(BF16))
    m_ref[...] = jnp.full(m_ref.shape, M_FLOOR, F32)
    l_ref[...] = jnp.zeros(l_ref.shape, F32)
    acc_ref[...] = jnp.zeros(acc_ref.shape, F32)
    jcut = jcut_ref[...]
    AB = DSA_ATT_BLOCK
    nab = (j * R + R + AB - 1) // AB
    rowf = lax.broadcasted_iota(I32, (AB, R), 0).astype(F32)

    def qk(kb):
        st = pl.multiple_of(kb * AB, AB)
        return _dot_nt(k_ref[0, pl.ds(st, AB), :], q_stack)

    qk_ref[...] = qk(0)

    def att_body(kb, c):
        st = pl.multiple_of(kb * AB, AB)
        s_raw = qk_ref[...]
        qk_ref[...] = qk(jnp.minimum(kb + 1, nab - 1))
        sc = sc_ref[pl.ds(st, AB), :]
        keep = (sc > lo) | ((sc == lo) & (rowf <= jcut - st.astype(F32)))
        bias = jnp.where(keep, 0.0, NEG)
        s = s_raw + jnp.concatenate([bias] * GROUP_HEADS, axis=1)
        m_old = m_ref[...]
        m_new = jnp.maximum(m_old, jnp.max(s, axis=0, keepdims=True))
        alpha = jnp.exp(m_old - m_new)
        p = jnp.exp(s - m_new)
        l_ref[...] = alpha * l_ref[...] + jnp.sum(p, axis=0, keepdims=True)
        acc_ref[...] = alpha * acc_ref[...] + _dot_tn(v_ref[0, pl.ds(st, AB), :], p.astype(BF16))
        m_ref[...] = m_new
        return c

    lax.fori_loop(0, nab, att_body, 0)
    o_ref[0] = _head_unstack((acc_ref[...] / l_ref[...]).T, R).astype(o_ref.dtype)


def _dsa(q, k, v, qi, ki, wt):
    B, T, _ = q.shape
    KB = DSA_KEY_BLOCK
    assert T % KB == 0
    n_sel = min(DSA_TOPK, T // 4)
    R = DSA_Q
    assert T % R == 0
    SR = GROUP_HEADS * R
    nq = T // R
    qblk = lambda wd: pl.BlockSpec((1, R, wd), lambda b, j: (b, j, 0))
    full = lambda wd: pl.BlockSpec((1, T, wd), lambda b, j: (b, 0, 0))
    return pl.pallas_call(
        functools.partial(_dsa_kernel, n_sel=n_sel, n_keys=T),
        grid=(B, nq),
        in_specs=[qblk(GROUP_WIDTH), full(GROUP_WIDTH), full(GROUP_WIDTH),
                  qblk(IDX_HEADS * IDX_DIM), full(IDX_DIM),
                  pl.BlockSpec((IDX_HEADS, R), lambda b, j: (0, b * nq + j))],
        out_specs=qblk(GROUP_WIDTH),
        out_shape=jax.ShapeDtypeStruct((B, T, GROUP_WIDTH), BF16),
        scratch_shapes=[pltpu.VMEM((T, R), F32), pltpu.VMEM((1, R), F32), pltpu.VMEM((DSA_ATT_BLOCK, SR), F32),
                        pltpu.VMEM((1, SR), F32), pltpu.VMEM((1, SR), F32),
                        pltpu.VMEM((GROUP_WIDTH, SR), F32)],
        compiler_params=_cparams(2),
        name="dsa_attention",
    )(q, k, v, qi, ki, wt)


def _outproj_kernel(oa, orr, oc, od, x_ref, w_ref, g_ref, b_ref, rw_ref, rb_ref, ltri_ref,
                    x1_ref, ti_ref, tg_ref, rk_ref, cnt_ref, run_ref):
    GW = GROUP_WIDTH

    @pl.when(pl.program_id(0) == 0)
    def _():
        run_ref[...] = jnp.zeros_like(run_ref)

    acc = _dot(oa[...], w_ref[0:GW, :])
    acc = acc + _dot(orr[...], w_ref[GW:2 * GW, :])
    acc = acc + _dot(oc[...], w_ref[2 * GW:3 * GW, :])
    acc = acc + _dot(od[...], w_ref[3 * GW:4 * GW, :])
    x1 = _layer_norm_rows(ALPHA * x_ref[...] + acc, g_ref[...], b_ref[...])
    x1_ref[...] = x1

    logits = _dot(x1.astype(BF16), rw_ref[...]) + rb_ref[...]
    col = lax.broadcasted_iota(I32, logits.shape, 1)
    kcol = lax.broadcasted_iota(I32, ti_ref.shape, 1)
    g = logits
    ti = jnp.zeros(ti_ref.shape, I32)
    tv = jnp.zeros(tg_ref.shape, F32)
    picks = []
    for kk in range(TOP_K):
        mx = jnp.max(g, axis=1, keepdims=True)
        first = jnp.min(jnp.where(g == mx, col, N_EXPERTS), axis=1, keepdims=True)
        ti = jnp.where(kcol == kk, first, ti)
        tv = jnp.where(kcol == kk, mx, tv)
        picks.append(col == first)
        g = jnp.where(picks[-1], -jnp.inf, g)
    e = jnp.exp(tv - jnp.max(tv, axis=1, keepdims=True))
    ti_ref[...] = ti
    tg_ref[...] = e / jnp.sum(e, axis=1, keepdims=True)

    sel = jnp.where(picks[0] | picks[1] | picks[2] | picks[3], 1.0, 0.0)
    before = run_ref[...] + _dot(ltri_ref[...], sel.astype(BF16))
    rk = jnp.zeros(rk_ref.shape, F32)
    for kk in range(TOP_K):
        rk = jnp.where(kcol == kk, jnp.sum(jnp.where(picks[kk], before, 0.0), axis=1, keepdims=True), rk)
    rk_ref[...] = rk.astype(I32)
    run_ref[...] = run_ref[...] + jnp.sum(sel, axis=0, keepdims=True)
    cnt_ref[...] = run_ref[...]


def _outproj(oa, orr, oc, od, x2, w_out, g, b, rw, rb):
    N = x2.shape[0]
    tm = OUT_TM
    row = lambda w: pl.BlockSpec((tm, w), lambda i: (i, 0))
    const = lambda a: pl.BlockSpec(a.shape, lambda i: (0,) * a.ndim)
    ltri = jnp.asarray(np.tril(np.ones((tm, tm), np.float32), -1), dtype=BF16)
    return pl.pallas_call(
        _outproj_kernel,
        grid=(N // tm,),
        in_specs=[row(GROUP_WIDTH)] * 4 + [row(D_MODEL), const(w_out), const(g), const(b), const(rw), const(rb),
                                           const(ltri)],
        out_specs=[row(D_MODEL), row(TOP_K), row(TOP_K), row(TOP_K),
                   pl.BlockSpec((1, N_EXPERTS), lambda i: (0, 0))],
        out_shape=[jax.ShapeDtypeStruct((N, D_MODEL), F32), jax.ShapeDtypeStruct((N, TOP_K), I32),
                   jax.ShapeDtypeStruct((N, TOP_K), F32), jax.ShapeDtypeStruct((N, TOP_K), I32),
                   jax.ShapeDtypeStruct((1, N_EXPERTS), F32)],
        scratch_shapes=[pltpu.VMEM((1, N_EXPERTS), F32)],
        compiler_params=_cparams(1),
        name="outproj_ln_router",
    )(oa, orr, oc, od, x2, w_out, g, b, rw, rb, ltri)


def _dispatch_kernel(tv_ref, pos_ref, x_ref, xs_hbm, xbuf, sem, zsem, *, n_tiles):
    i = pl.program_id(0)
    n = pl.num_programs(0)
    tm = DSP_TM
    par = i % 2

    @pl.when(i == 0)
    def _():
        xbuf[1] = jnp.zeros((tm, D_MODEL), F32)

        def fill(t, c):
            @pl.when(tv_ref[t] < MOE_TM)
            def _():
                pltpu.make_async_copy(xbuf.at[1], xs_hbm.at[pl.ds(pl.multiple_of(t * MOE_TM, MOE_TM), MOE_TM)],
                                      zsem).start()
            return c

        def drain(t, c):
            @pl.when(tv_ref[t] < MOE_TM)
            def _():
                pltpu.make_async_copy(xbuf.at[1], xs_hbm.at[pl.ds(0, MOE_TM)], zsem).wait()
            return c

        lax.fori_loop(0, n_tiles, fill, 0)
        lax.fori_loop(0, n_tiles, drain, 0)

    def wait_step(p):
        for _ in range(TOP_K):
            pltpu.make_async_copy(xbuf.at[p], xbuf.at[p], sem.at[p]).wait()

    @pl.when(i >= 2)
    def _():
        wait_step(par)

    xbuf[par] = x_ref[...]

    def body(r, c):
        for kk in range(TOP_K):
            pltpu.make_async_copy(xbuf.at[par, pl.ds(r, 1)], xs_hbm.at[pl.ds(pos_ref[r * TOP_K + kk], 1)],
                                  sem.at[par]).start()
        return c

    lax.fori_loop(0, tm, body, 0, unroll=4)

    @pl.when(i == n - 1)
    def _():
        wait_step(1 - par)
        wait_step(par)


def _dispatch(pos, x1, tile_valid):
    N = x1.shape[0]
    tm = DSP_TM
    n_tiles = tile_valid.shape[0]
    assert N // tm >= 2 and tm == MOE_TM
    grid_spec = pltpu.PrefetchScalarGridSpec(
        num_scalar_prefetch=1,
        grid=(N // tm,),
        in_specs=[pl.BlockSpec((tm * TOP_K,), lambda i, tv: (i,), memory_space=pltpu.SMEM),
                  pl.BlockSpec((tm, D_MODEL), lambda i, tv: (i, 0))],
        out_specs=pl.BlockSpec(memory_space=pl.ANY),
        scratch_shapes=[pltpu.VMEM((2, tm, D_MODEL), F32), pltpu.SemaphoreType.DMA((2,)),
                        pltpu.SemaphoreType.DMA(())],
    )
    return pl.pallas_call(
        functools.partial(_dispatch_kernel, n_tiles=n_tiles),
        grid_spec=grid_spec,
        out_shape=jax.ShapeDtypeStruct((n_tiles * MOE_TM, D_MODEL), F32),
        compiler_params=_cparams(1),
        name="moe_dispatch",
    )(tile_valid, pos, x1)


def _moe_kernel(te_ref, nv_ref, x_ref, w1_ref, b1_ref, w2_ref, b2_ref, y_ref, w1b, w2b):
    i = pl.program_id(0)
    tm = MOE_TM
    n_valid = nv_ref[i]

    @pl.when((i == 0) | (te_ref[i] != te_ref[jnp.maximum(i - 1, 0)]))
    def _():
        step = 128
        for c in range(D_MODEL // step):
            w1b[c * step:(c + 1) * step, :] = w1_ref[0, 0, c * step:(c + 1) * step, :].astype(BF16)
        for c in range(D_FF // step):
            w2b[c * step:(c + 1) * step, :] = w2_ref[0, 0, c * step:(c + 1) * step, :].astype(BF16)

    @pl.when(n_valid > 0)
    def _():
        h = _dot(x_ref[...].astype(BF16), w1b[...]) + b1_ref[0]
        glu_in = jnp.minimum(h[:, :D_FF], SWIGLU_LIMIT)
        up = jnp.clip(h[:, D_FF:], -SWIGLU_LIMIT, SWIGLU_LIMIT)
        glu = glu_in * jax.nn.sigmoid(SWIGLU_ALPHA * glu_in)
        y_ref[...] = _dot(((up + 1.0) * glu).astype(BF16), w2b[...]) + b2_ref[0]

    @pl.when(n_valid == 0)
    def _():
        y_ref[...] = jnp.zeros_like(y_ref)


def _moe_experts(xs, tile_expert, tile_valid, w1, b1, w2, b2, layer):
    tm = MOE_TM
    n_tiles = tile_expert.shape[0]
    grid_spec = pltpu.PrefetchScalarGridSpec(
        num_scalar_prefetch=2,
        grid=(n_tiles,),
        in_specs=[
            pl.BlockSpec((tm, D_MODEL), lambda i, te, nv: (i, 0)),
            pl.BlockSpec((1, 1, D_MODEL, 2 * D_FF), lambda i, te, nv: (layer, te[i], 0, 0)),
            pl.BlockSpec((1, 1, 2 * D_FF), lambda i, te, nv: (te[i], 0, 0)),
            pl.BlockSpec((1, 1, D_FF, D_MODEL), lambda i, te, nv: (layer, te[i], 0, 0)),
            pl.BlockSpec((1, 1, D_MODEL), lambda i, te, nv: (te[i], 0, 0)),
        ],
        out_specs=pl.BlockSpec((tm, D_MODEL), lambda i, te, nv: (i, 0)),
        scratch_shapes=[pltpu.VMEM((D_MODEL, 2 * D_FF), BF16), pltpu.VMEM((D_FF, D_MODEL), BF16)],
    )
    return pl.pallas_call(
        _moe_kernel,
        grid_spec=grid_spec,
        out_shape=jax.ShapeDtypeStruct((n_tiles * tm, D_MODEL), F32),
        compiler_params=_cparams(1),
        name="moe_experts",
    )(tile_expert, tile_valid, xs, w1, b1, w2, b2)


def _combine_kernel(pos_ref, posn_ref, y_hbm, x1_ref, tg_ref, g_ref, b_ref, o_ref, ybuf, sem):
    i = pl.program_id(0)
    n = pl.num_programs(0)
    tm = CMB_TM
    slot = i % 2

    def issue(idx_ref, s):
        def body(r, c):
            for kk in range(TOP_K):
                pltpu.make_async_copy(y_hbm.at[pl.ds(idx_ref[r * TOP_K + kk], 1)],
                                      ybuf.at[s, kk, pl.ds(r, 1)], sem.at[s]).start()
            return c
        lax.fori_loop(0, tm, body, 0, unroll=4)

    @pl.when(i == 0)
    def _():
        issue(pos_ref, 0)

    @pl.when(i + 1 < n)
    def _():
        issue(posn_ref, 1 - slot)

    pltpu.make_async_copy(ybuf.at[slot], ybuf.at[slot], sem.at[slot]).wait()
    tg = tg_ref[...]
    moe = tg[:, 0:1] * ybuf[slot, 0]
    for kk in range(1, TOP_K):
        moe = moe + tg[:, kk:kk + 1] * ybuf[slot, kk]
    o_ref[...] = _layer_norm_rows(ALPHA * x1_ref[...] + moe, g_ref[...], b_ref[...])


def _combine(pos, y_sorted, x1, tg, g, b):
    N = x1.shape[0]
    tm = CMB_TM
    n = N // tm
    row = lambda w: pl.BlockSpec((tm, w), lambda i: (i, 0))
    const = lambda a: pl.BlockSpec(a.shape, lambda i: (0,) * a.ndim)
    return pl.pallas_call(
        _combine_kernel,
        grid=(n,),
        in_specs=[pl.BlockSpec((tm * TOP_K,), lambda i: (i,), memory_space=pltpu.SMEM),
                  pl.BlockSpec((tm * TOP_K,), lambda i: (jnp.minimum(i + 1, n - 1),), memory_space=pltpu.SMEM),
                  pl.BlockSpec(memory_space=pl.ANY), row(D_MODEL), row(TOP_K), const(g), const(b)],
        out_specs=row(D_MODEL),
        out_shape=jax.ShapeDtypeStruct((N, D_MODEL), F32),
        scratch_shapes=[pltpu.VMEM((2, TOP_K, tm, D_MODEL), F32), pltpu.SemaphoreType.DMA((2,))],
        compiler_params=_cparams(1),
        name="moe_combine_ln",
    )(pos, pos, y_sorted, x1, tg, g, b)


def _routing_tables(top_i, rank, counts_f, n_tiles):
    tm = MOE_TM
    counts = counts_f.reshape(-1).astype(I32)
    padded = ((counts + tm - 1) // tm) * tm
    ends = jnp.cumsum(padded)
    offsets = ends - padded
    onehot = top_i[:, :, None] == jnp.arange(N_EXPERTS, dtype=I32)[None, None, :]
    pos = jnp.sum(jnp.where(onehot, offsets[None, None, :], 0), axis=-1) + rank
    tile_start = jnp.arange(n_tiles, dtype=I32) * tm
    tile_expert = jnp.sum((ends[None, :] <= tile_start[:, None]).astype(I32), axis=1)
    tile_expert = jnp.minimum(tile_expert, N_EXPERTS - 1)
    n_used = ends[-1] // tm
    last_expert = tile_expert[jnp.maximum(n_used - 1, 0)]
    tile_expert = jnp.where(tile_start < ends[-1], tile_expert, last_expert)
    valid_end = (offsets + counts)[tile_expert]
    tile_valid = jnp.clip(valid_end - tile_start, 0, tm)
    return pos.reshape(-1).astype(I32), tile_expert.astype(I32), tile_valid.astype(I32)


def _rope_tables(T):
    inv = ROPE_THETA ** (-jnp.arange(0, HEAD_DIM, 2, dtype=F32) / HEAD_DIM)
    ang = jnp.arange(T, dtype=F32)[:, None] * inv[None, :]
    cos, sin = jnp.cos(ang), jnp.sin(ang)
    cos_t = jnp.tile(jnp.concatenate([cos, cos], axis=-1), (1, GROUP_HEADS))
    sin_t = jnp.tile(jnp.concatenate([-sin, sin], axis=-1), (1, GROUP_HEADS))
    return cos_t, sin_t


def _pad_w_in(w_in):
    base = 12 * GROUP_WIDTH + IDX_HEADS * IDX_DIM
    w = jnp.zeros((D_MODEL, IN_PAD), F32)
    w = w.at[:, :base + IDX_DIM].set(w_in[:, :base + IDX_DIM])
    w = w.at[:, base + 128:base + 128 + IDX_HEADS].set(w_in[:, base + IDX_DIM:])
    return w.astype(BF16)


def _layer(x2, B, T, cos_t, sin_t, tabs, w_in, ret_gn_g, ret_gn_b, conv_w, conv_b, rg_wx, rg_bx, rg_wa,
           rg_ba, rg_lambda, w_out, ln1_g, ln1_b, router_w, router_b, exp_w1, exp_b1, exp_w2, exp_b2,
           ln2_g, ln2_b, layer):
    N = B * T
    r2 = lambda a: a.reshape(1, -1)
    (aq, ak, av, rq, rk, rv, rg, cx, cg, dq, dk, dv, dqi, dki, dwt) = _proj(x2, _pad_w_in(w_in), cos_t, sin_t, T)
    seq = lambda a: a.reshape(B, T, a.shape[-1])
    o_a = _moba(seq(aq), seq(ak), seq(av))
    o_r = _retention(seq(rq), seq(rk), seq(rv), seq(rg), r2(ret_gn_g), r2(ret_gn_b), tabs)
    o_c = _rglru(seq(cx), seq(cg), conv_w, r2(conv_b), _block_diag(rg_wx).astype(BF16), r2(rg_bx),
                 _block_diag(rg_wa).astype(BF16), r2(rg_ba), r2(rg_lambda))
    o_d = _dsa(seq(dq), seq(dk), seq(dv), seq(dqi), seq(dki), dwt)
    flat = lambda a: a.reshape(N, GROUP_WIDTH)
    x1, top_i, top_g, rank, counts = _outproj(flat(o_a), flat(o_r), flat(o_c), flat(o_d), x2,
                                              w_out.astype(BF16), r2(ln1_g), r2(ln1_b),
                                              router_w.astype(BF16), r2(router_b))
    n_tiles = (N * TOP_K) // MOE_TM + N_EXPERTS
    pos, tile_expert, tile_valid = _routing_tables(top_i, rank, counts, n_tiles)
    xs = _dispatch(pos, x1, tile_valid)
    y_sorted = _moe_experts(xs, tile_expert, tile_valid, exp_w1, exp_b1.reshape(N_EXPERTS, 1, -1), exp_w2,
                            exp_b2.reshape(N_EXPERTS, 1, -1), layer)
    return _combine(pos, y_sorted, x1, top_g, r2(ln2_g), r2(ln2_b))


def kernel(x, w_in, ret_gn_g, ret_gn_b, conv_w, conv_b, rg_wx, rg_bx, rg_wa, rg_ba, rg_lambda, w_out,
           ln1_g, ln1_b, router_w, router_b, exp_w1, exp_b1, exp_w2, exp_b2, ln2_g, ln2_b):
    B, T, D = x.shape
    cos_t, sin_t = _rope_tables(T)
    tabs = _ret_tables()
    x2 = x.reshape(B * T, D)
    for l in range(w_in.shape[0]):
        x2 = _layer(x2, B, T, cos_t, sin_t, tabs, w_in[l], ret_gn_g[l], ret_gn_b[l], conv_w[l], conv_b[l],
                    rg_wx[l], rg_bx[l], rg_wa[l], rg_ba[l], rg_lambda[l], w_out[l], ln1_g[l], ln1_b[l],
                    router_w[l], router_b[l], exp_w1, exp_b1[l], exp_w2, exp_b2[l], ln2_g[l], ln2_b[l], l)
    return x2.reshape(B, T, D)
```

```python
import functools

import numpy as np
import jax
import jax.numpy as jnp
from jax import lax
from jax.experimental import pallas as pl
from jax.experimental.pallas import tpu as pltpu

F32 = jnp.float32
BF16 = jnp.bfloat16
I32 = jnp.int32

D_MODEL = 1024
DEPTH = 2
HEAD_DIM = 64
GROUP_WIDTH = 256
GROUP_HEADS = 4
ROPE_THETA = 10000.0
Q_BLOCK = 128
MOBA_BLOCK = 256
MOBA_TOPK = 3
MOBA_MAX_BLOCKS = 16
RET_CHUNK = 128
RG_CONV = 4
RG_C = 8.0
IDX_HEADS = 8
IDX_DIM = 64
IDX_SCALE = (IDX_HEADS ** -0.5) * (IDX_DIM ** -0.5)
DSA_TOPK = 256
DSA_Q = 256
DSA_KEY_BLOCK = 512
DSA_ATT_BLOCK = 512
N_EXPERTS = 32
TOP_K = 4
D_FF = 1024
SWIGLU_LIMIT = 7.0
SWIGLU_ALPHA = 1.702
ALPHA = (2 * DEPTH) ** 0.25
LN_EPS = 1e-5
IN_WIDTH = 12 * GROUP_WIDTH + IDX_HEADS * IDX_DIM + IDX_DIM + IDX_HEADS
IN_PAD = 15 * GROUP_WIDTH

LANES = 128
SUBLANES = 8
HEAD_SHIFT = HEAD_DIM.bit_length() - 1
KEY_MAGNITUDE_BITS = 0x7FFFFFFF
BISECT_MAX_STEPS = 36
NEG = -1e30
M_FLOOR = -1e29
BIG = 1e30
VMEM_LIMIT = 56 * 1024 * 1024
FOLD_ROWS = 32

PROJ_TM = 256
OUT_TM = 512
RG_TC = 256
MOE_TM = 512
DSP_TM = 512
CMB_TM = 256


def _cparams(ndims):
    return pltpu.CompilerParams(dimension_semantics=("arbitrary",) * ndims,
                                vmem_limit_bytes=VMEM_LIMIT)


def _dot(a, b, precision=None):
    return jnp.dot(a, b, preferred_element_type=F32, precision=precision)


def _dot_nt(a, b, precision=None):
    return lax.dot_general(a, b, (((1,), (1,)), ((), ())), preferred_element_type=F32,
                           precision=precision)


def _dot_tn(a, b):
    return lax.dot_general(a, b, (((0,), (0,)), ((), ())), preferred_element_type=F32)


def _head_stack(q):
    head = lax.shift_right_logical(lax.broadcasted_iota(I32, q.shape, 1), HEAD_SHIFT)
    qf = q.astype(F32)
    return jnp.concatenate([jnp.where(head == h, qf, 0.0) for h in range(GROUP_HEADS)],
                           axis=0).astype(q.dtype)


def _head_unstack(s, rows):
    head = lax.shift_right_logical(lax.broadcasted_iota(I32, (rows, GROUP_WIDTH), 1), HEAD_SHIFT)
    out = jnp.zeros((rows, GROUP_WIDTH), F32)
    for h in range(GROUP_HEADS):
        out = out + jnp.where(head == h, s[h * rows:(h + 1) * rows], 0.0)
    return out


def _fold_rows(x, op):
    return op(x.reshape(x.shape[0] // FOLD_ROWS, FOLD_ROWS, x.shape[1]), axis=0)


def _layer_norm_rows(y, g, b):
    mu = jnp.mean(y, axis=-1, keepdims=True)
    yc = y - mu
    var = jnp.mean(yc * yc, axis=-1, keepdims=True)
    return yc * lax.rsqrt(var + LN_EPS) * g + b


def _proj_kernel(x_ref, w_ref, cos_ref, sin_ref,
                 aq, ak, av, rq, rk, rv, rg, cx, cg, dq, dk, dv, dqi, dki, dwt):
    xb = x_ref[...].astype(BF16)
    cos = cos_ref[...]
    sin = sin_ref[...]
    first_half = (lax.broadcasted_iota(I32, cos.shape, 1) & (HEAD_DIM - 1)) < (HEAD_DIM // 2)

    def seg(i):
        return _dot(xb, w_ref[:, i * GROUP_WIDTH:(i + 1) * GROUP_WIDTH])

    def rope(p):
        rot = jnp.where(first_half, pltpu.roll(p, GROUP_WIDTH - HEAD_DIM // 2, 1),
                        pltpu.roll(p, HEAD_DIM // 2, 1))
        return p * cos + rot * sin

    aq[...] = rope(seg(0)).astype(BF16)
    ak[...] = rope(seg(1)).astype(BF16)
    av[...] = seg(2).astype(BF16)
    rq[...] = rope(seg(3)).astype(BF16)
    rk[...] = (rope(seg(4)) * (HEAD_DIM ** -0.5)).astype(BF16)
    rv[...] = seg(5).astype(BF16)
    rg[...] = seg(6)
    cx[...] = seg(7)
    cg[...] = seg(8)
    dq[...] = rope(seg(9)).astype(BF16)
    dk[...] = rope(seg(10)).astype(BF16)
    dv[...] = seg(11).astype(BF16)
    dqi[:, 0:GROUP_WIDTH] = rope(seg(12)).astype(BF16)
    dqi[:, GROUP_WIDTH:2 * GROUP_WIDTH] = rope(seg(13)).astype(BF16)
    last = seg(14)
    dki[...] = rope(last)[:, 0:IDX_DIM].astype(BF16)
    dwt[...] = last[:, LANES:2 * LANES].T[0:IDX_HEADS, :]


def _proj(x2, w_pad, cos_t, sin_t, T):
    N = x2.shape[0]
    tm = PROJ_TM
    tpb = T // tm
    row = lambda w: pl.BlockSpec((tm, w), lambda i: (i, 0))
    tab = pl.BlockSpec((tm, GROUP_WIDTH), lambda i: (i % tpb, 0))
    widths = [256] * 12 + [512, IDX_DIM]
    dtypes = [BF16, BF16, BF16, BF16, BF16, BF16, F32, F32, F32, BF16, BF16, BF16, BF16, BF16]
    return pl.pallas_call(
        _proj_kernel,
        grid=(N // tm,),
        in_specs=[row(D_MODEL), pl.BlockSpec((D_MODEL, IN_PAD), lambda i: (0, 0)), tab, tab],
        out_specs=[row(w) for w in widths] + [pl.BlockSpec((IDX_HEADS, tm), lambda i: (0, i))],
        out_shape=[jax.ShapeDtypeStruct((N, w), d) for w, d in zip(widths, dtypes)]
        + [jax.ShapeDtypeStruct((IDX_HEADS, N), F32)],
        compiler_params=_cparams(1),
        name="proj_rope",
    )(x2, w_pad, cos_t, sin_t)


def _moba_kernel(q_ref, k_ref, v_ref, o_ref, kmean_ref, sel_ref, qk_ref, m_ref, l_ref, acc_ref, *, n_blocks):
    j = pl.program_id(1)
    R = Q_BLOCK
    SR = GROUP_HEADS * R
    KB = MOBA_BLOCK

    @pl.when(j == 0)
    def _():
        kmean_ref[...] = jnp.zeros_like(kmean_ref)
        for n in range(n_blocks):
            kb = k_ref[0, n * KB:(n + 1) * KB, :].astype(F32)
            kmean_ref[n:n + 1, :] = jnp.mean(kb, axis=0, keepdims=True)

    own = j // (KB // R)
    q_raw = _head_stack(q_ref[0])
    q_stack = (q_raw.astype(F32) * (HEAD_DIM ** -0.5)).astype(BF16)

    gate = _dot_nt(kmean_ref[...], q_raw.astype(F32), precision=lax.Precision.HIGHEST)
    blk = lax.broadcasted_iota(I32, gate.shape, 0)
    past = blk < own
    g = jnp.where(past, gate, -jnp.inf)
    sel = jnp.zeros(gate.shape, F32)
    for _ in range(MOBA_TOPK):
        mx = jnp.max(g, axis=0, keepdims=True)
        first = jnp.min(jnp.where(g == mx, blk, MOBA_MAX_BLOCKS), axis=0, keepdims=True)
        pick = blk == first
        sel = jnp.where(pick & past, 1.0, sel)
        g = jnp.where(pick, -jnp.inf, g)
    sel_ref[...] = sel

    m_ref[...] = jnp.full(m_ref.shape, M_FLOOR, F32)
    l_ref[...] = jnp.zeros(l_ref.shape, F32)
    acc_ref[...] = jnp.zeros(acc_ref.shape, F32)

    def softmax_pv(s_raw, vb, bias):
        s = s_raw + bias
        m_old = m_ref[...]
        m_new = jnp.maximum(m_old, jnp.max(s, axis=0, keepdims=True))
        alpha = jnp.exp(m_old - m_new)
        p = jnp.exp(s - m_new)
        l_ref[...] = alpha * l_ref[...] + jnp.sum(p, axis=0, keepdims=True)
        acc_ref[...] = alpha * acc_ref[...] + _dot_tn(vb, p.astype(BF16))
        m_ref[...] = m_new

    def block_bias(n):
        return jnp.broadcast_to(jnp.where(sel_ref[pl.ds(n, 1), :] > 0.5, 0.0, NEG), (KB, SR))

    def qk(first_block, rows):
        st = pl.multiple_of(first_block * KB, KB)
        return _dot_nt(k_ref[0, pl.ds(st, rows), :], q_stack)

    def attend_span(first_block, bias, s_raw=None):
        st = pl.multiple_of(first_block * KB, KB)
        rows = bias.shape[0]
        s_raw = qk(first_block, rows) if s_raw is None else s_raw
        softmax_pv(s_raw, v_ref[0, pl.ds(st, rows), :], bias)

    keypos = own * KB + lax.broadcasted_iota(I32, (KB, SR), 0)
    qpos = j * R + (lax.broadcasted_iota(I32, (KB, SR), 1) & (R - 1))
    causal = jnp.where(keypos <= qpos, 0.0, NEG)
    odd = (own & 1) == 1

    @pl.when(odd)
    def _():
        attend_span(own - 1, jnp.concatenate([block_bias(own - 1), causal], axis=0))

    @pl.when(jnp.logical_not(odd))
    def _():
        attend_span(own, causal)

    n_pairs = own // 2

    @pl.when(n_pairs > 0)
    def _():
        qk_ref[...] = qk(0, 2 * KB)

    def body(pair, c):
        s_raw = qk_ref[...]
        qk_ref[...] = qk(2 * jnp.minimum(pair + 1, n_pairs - 1), 2 * KB)
        attend_span(2 * pair, jnp.concatenate([block_bias(2 * pair), block_bias(2 * pair + 1)], axis=0), s_raw)
        return c

    lax.fori_loop(0, n_pairs, body, 0)
    o_ref[0] = _head_unstack((acc_ref[...] / l_ref[...]).T, R).astype(o_ref.dtype)


def _moba(q, k, v):
    B, T, _ = q.shape
    n_blocks = T // MOBA_BLOCK
    assert T % MOBA_BLOCK == 0 and n_blocks <= MOBA_MAX_BLOCKS
    SR = GROUP_HEADS * Q_BLOCK
    return pl.pallas_call(
        functools.partial(_moba_kernel, n_blocks=n_blocks),
        grid=(B, T // Q_BLOCK),
        in_specs=[pl.BlockSpec((1, Q_BLOCK, GROUP_WIDTH), lambda b, j: (b, j, 0)),
                  pl.BlockSpec((1, T, GROUP_WIDTH), lambda b, j: (b, 0, 0)),
                  pl.BlockSpec((1, T, GROUP_WIDTH), lambda b, j: (b, 0, 0))],
        out_specs=pl.BlockSpec((1, Q_BLOCK, GROUP_WIDTH), lambda b, j: (b, j, 0)),
        out_shape=jax.ShapeDtypeStruct((B, T, GROUP_WIDTH), BF16),
        scratch_shapes=[pltpu.VMEM((MOBA_MAX_BLOCKS, GROUP_WIDTH), F32),
                        pltpu.VMEM((MOBA_MAX_BLOCKS, SR), F32),
                        pltpu.VMEM((2 * MOBA_BLOCK, SR), F32),
                        pltpu.VMEM((1, SR), F32), pltpu.VMEM((1, SR), F32),
                        pltpu.VMEM((GROUP_WIDTH, SR), F32)],
        compiler_params=_cparams(2),
        name="moba_attention",
    )(q, k, v)


def _ret_kernel(q_ref, k_ref, v_ref, g_ref, dmask_ref, xi_ref, zeta_ref, gdec_ref, bd_ref, avg_ref,
                gng_ref, gnb_ref, o_ref, r_ref):
    j = pl.program_id(0)

    @pl.when(j == 0)
    def _():
        r_ref[...] = jnp.zeros_like(r_ref)

    C = RET_CHUNK
    hp = lax.Precision.HIGHEST
    for b in range(q_ref.shape[0]):
        q = q_ref[b]
        k = k_ref[b]
        v = v_ref[b]
        q_stack = _head_stack(q)
        inner = _dot_nt(q_stack, k) * dmask_ref[...]
        o = _head_unstack(_dot(inner.astype(BF16), v), C)
        R = r_ref[b]
        o = o + _dot(q, R.astype(BF16)) * xi_ref[...]
        kz = (k.astype(F32) * zeta_ref[...]).astype(BF16)
        r_ref[b] = gdec_ref[...] * R + bd_ref[...] * _dot_tn(kz, v)

        mu = _dot(o, avg_ref[...], precision=hp)
        oc = o - mu
        var = _dot(oc * oc, avg_ref[...], precision=hp)
        y = oc * lax.rsqrt(var + LN_EPS) * gng_ref[...] + gnb_ref[...]
        gte = g_ref[b]
        o_ref[b] = (y * (gte * jax.nn.sigmoid(gte))).astype(o_ref.dtype)


def _ret_tables():
    H, C, d = GROUP_HEADS, RET_CHUNK, HEAD_DIM
    log_g = np.log(1.0 - 2.0 ** (-5.0 - np.arange(H, dtype=np.float64)))
    n = np.arange(C, dtype=np.float64)
    diff = n[:, None] - n[None, :]
    dmask = np.where(diff >= 0, np.exp(log_g[:, None, None] * np.maximum(diff, 0.0)), 0.0)
    xi = np.exp(log_g[:, None] * (n + 1.0))
    zeta = np.exp(log_g[:, None] * (C - 1.0 - n))
    g_chunk = np.exp(log_g * C)
    head = np.arange(GROUP_WIDTH) // d
    bd = (head[:, None] == head[None, :]).astype(np.float64)
    to32 = lambda a: jnp.asarray(a, dtype=F32)
    return dict(dmask=to32(dmask.reshape(H * C, C)), xi=to32(xi.T[:, head]), zeta=to32(zeta.T[:, head]),
                gdec=to32(bd * g_chunk[head][:, None]), bd=to32(bd), avg=to32(bd / d))


def _retention(rq, rk, rv, rg, gn_g, gn_b, tabs):
    B, T, _ = rq.shape
    C = RET_CHUNK
    blk = pl.BlockSpec((B, C, GROUP_WIDTH), lambda j: (0, j, 0))
    const = lambda a: pl.BlockSpec(a.shape, lambda j: (0,) * a.ndim)
    consts = [tabs["dmask"], tabs["xi"], tabs["zeta"], tabs["gdec"], tabs["bd"], tabs["avg"], gn_g, gn_b]
    return pl.pallas_call(
        _ret_kernel,
        grid=(T // C,),
        in_specs=[blk, blk, blk, blk] + [const(a) for a in consts],
        out_specs=blk,
        out_shape=jax.ShapeDtypeStruct((B, T, GROUP_WIDTH), BF16),
        scratch_shapes=[pltpu.VMEM((B, GROUP_WIDTH, GROUP_WIDTH), F32)],
        compiler_params=_cparams(1),
        name="retention",
    )(rq, rk, rv, rg, *consts)


def _rglru_kernel(x_ref, g_ref, cw_ref, cb_ref, wx_ref, bx_ref, wa_ref, ba_ref, lam_ref, o_ref,
                  xbuf, h_ref):
    j = pl.program_id(1)
    tc = RG_TC

    @pl.when(j == 0)
    def _():
        xbuf[0:SUBLANES, :] = jnp.zeros((SUBLANES, GROUP_WIDTH), F32)
        h_ref[...] = jnp.zeros_like(h_ref)

    xbuf[SUBLANES:SUBLANES + tc, :] = x_ref[0]
    xc = cb_ref[...] + cw_ref[RG_CONV - 1:RG_CONV, :] * xbuf[SUBLANES:SUBLANES + tc, :]
    for i in range(RG_CONV - 1):
        off = SUBLANES - (RG_CONV - 1) + i
        xc = xc + cw_ref[i:i + 1, :] * xbuf[off:off + tc, :]
    xbuf[0:SUBLANES, :] = xbuf[tc:tc + SUBLANES, :]

    xcb = xc.astype(BF16)
    gate_x = jax.nn.sigmoid(_dot(xcb, wx_ref[...]) + bx_ref[...])
    gate_a = jax.nn.sigmoid(_dot(xcb, wa_ref[...]) + ba_ref[...])
    lam = lam_ref[...]
    softplus_neg = jnp.maximum(-lam, 0.0) + jnp.log1p(jnp.exp(-jnp.abs(lam)))
    log_a = -RG_C * gate_a * softplus_neg
    a = jnp.exp(log_a)
    th = jnp.tanh(log_a)
    b = jnp.sqrt(-2.0 * th / (1.0 - th)) * (gate_x * xc)

    row = lax.broadcasted_iota(I32, (tc, GROUP_WIDTH), 0)
    d = 1
    while d < tc:
        keep = row >= d
        a_sh = jnp.where(keep, pltpu.roll(a, d, 0), 1.0)
        b_sh = jnp.where(keep, pltpu.roll(b, d, 0), 0.0)
        b = a * b_sh + b
        a = a * a_sh
        d *= 2
    h = b + a * h_ref[...]
    h_ref[...] = h[tc - 1:tc, :]

    xg = g_ref[0]
    gelu = 0.5 * xg * (1.0 + jnp.tanh(np.sqrt(2.0 / np.pi) * (xg + 0.044715 * xg * xg * xg)))
    o_ref[0] = (h * gelu).astype(o_ref.dtype)


def _block_diag(w):
    n, c, _ = w.shape
    eye = jnp.eye(n, dtype=w.dtype)
    return (eye[:, None, :, None] * w[:, :, None, :]).reshape(n * c, n * c)


def _rglru(cx, cg, conv_w, conv_b, wx, bx, wa, ba, lam):
    B, T, _ = cx.shape
    tc = RG_TC
    blk = pl.BlockSpec((1, tc, GROUP_WIDTH), lambda b, j: (b, j, 0))
    const = lambda a: pl.BlockSpec(a.shape, lambda b, j: (0,) * a.ndim)
    consts = [conv_w, conv_b, wx, bx, wa, ba, lam]
    return pl.pallas_call(
        _rglru_kernel,
        grid=(B, T // tc),
        in_specs=[blk, blk] + [const(a) for a in consts],
        out_specs=blk,
        out_shape=jax.ShapeDtypeStruct((B, T, GROUP_WIDTH), BF16),
        scratch_shapes=[pltpu.VMEM((tc + SUBLANES, GROUP_WIDTH), F32), pltpu.VMEM((1, GROUP_WIDTH), F32)],
        compiler_params=_cparams(2),
        name="rg_lru",
    )(cx, cg, *consts)


def _dsa_kernel(q_ref, k_ref, v_ref, qi_ref, ki_ref, wt_ref, o_ref,
                sc_ref, jcut_ref, qk_ref, m_ref, l_ref, acc_ref, *, n_sel, n_keys):
    j = pl.program_id(1)
    R = DSA_Q
    KB = DSA_KEY_BLOCK
    SR = GROUP_HEADS * R
    nkb = (j * R + R + KB - 1) // KB
    nsel = float(n_sel)

    rowk = lax.broadcasted_iota(I32, (KB, R), 0)
    qpos = j * R + lax.broadcasted_iota(I32, (KB, R), 1)

    qi = qi_ref[0]
    qi_stack = jnp.concatenate([qi[:, h * IDX_DIM:(h + 1) * IDX_DIM] for h in range(IDX_HEADS)], axis=0)
    wt = wt_ref[...]

    def score_body(kb, c):
        st = pl.multiple_of(kb * KB, KB)
        rel = _dot_nt(ki_ref[0, pl.ds(st, KB), :], qi_stack)
        sc = wt[0:1, :] * jnp.maximum(rel[:, 0:R], 0.0)
        for h in range(1, IDX_HEADS):
            sc = sc + wt[h:h + 1, :] * jnp.maximum(rel[:, h * R:(h + 1) * R], 0.0)
        sc_ref[pl.ds(st, KB), :] = jnp.where(st + rowk <= qpos, sc * IDX_SCALE, NEG)
        return c

    lax.fori_loop(0, nkb, score_body, 0)

    def blocks(fn, init):
        def body(kb, c):
            st = pl.multiple_of(kb * KB, KB)
            return fn(st, sc_ref[pl.ds(st, KB), :], c)
        return lax.fori_loop(0, nkb, body, init)

    def minmax(st, s, c):
        mn, mx = c
        return (jnp.minimum(mn, _fold_rows(jnp.where(s > 0.5 * NEG, s, BIG), jnp.min)),
                jnp.maximum(mx, _fold_rows(s, jnp.max)))

    mn8, mx8 = blocks(minmax, (jnp.full((FOLD_ROWS, R), BIG, F32), jnp.full((FOLD_ROWS, R), NEG, F32)))
    mn = jnp.min(mn8, axis=0, keepdims=True)
    mx = jnp.max(mx8, axis=0, keepdims=True)

    def count_ge(th):
        acc = blocks(lambda st, s, c: c + _fold_rows(jnp.where(s >= th, 1.0, 0.0), jnp.sum),
                     jnp.zeros((FOLD_ROWS, R), F32))
        return jnp.sum(acc, axis=0, keepdims=True)

    n_adm = (j * R + 1 + lax.broadcasted_iota(I32, (1, R), 1)).astype(F32)
    need = n_adm > nsel
    lo0 = jnp.where(need, mn, 0.5 * NEG)
    cgt0 = jnp.zeros((1, R), F32)
    act0 = jnp.where(need, 1.0, 0.0)
    tie0 = jnp.zeros((1, R), F32)

    def to_key(f):
        b = lax.bitcast_convert_type(f, I32)
        return b ^ (lax.shift_right_arithmetic(b, 31) & KEY_MAGNITUDE_BITS)

    def from_key(kk):
        return lax.bitcast_convert_type(kk ^ (lax.shift_right_arithmetic(kk, 31) & KEY_MAGNITUDE_BITS), F32)

    def bis_step(klo, khi, cgt, act, tie):
        on = act > 0.0
        kmid = lax.shift_right_arithmetic(klo, 1) + lax.shift_right_arithmetic(khi, 1) + (klo & khi & 1)
        stuck = kmid == klo
        cnt = count_ge(from_key(kmid))
        go = on & jnp.logical_not(stuck)
        up = go & (cnt >= nsel)
        dn = go & (cnt < nsel)
        return (jnp.where(up, kmid, klo), jnp.where(dn, kmid, khi), jnp.where(dn, cnt, cgt),
                jnp.where(go & (cnt != nsel), 1.0, 0.0), jnp.where(on & stuck, 1.0, tie))

    def bis_cond(c):
        return (c[1] > 0.0) & (c[0] < BISECT_MAX_STEPS)

    def bis_body(c):
        flag = jnp.max(c[5])
        st = bis_step(*bis_step(*c[2:]))
        return (c[0] + 2, flag) + st

    res = lax.while_loop(bis_cond, bis_body,
                         (jnp.int32(0), jnp.max(act0), to_key(lo0), to_key(mx) + 1, cgt0, act0, tie0))
    lo, cgt, tie = from_key(res[2]), res[4], res[6]

    jcut_ref[...] = jnp.full((1, R), float(n_keys), F32)

    @pl.when(jnp.max(tie) > 0.0)
    def _():
        want = nsel - cgt
        tied = tie > 0.0

        def jb(it, c):
            a, b = c
            mid = jnp.floor((a + b) * 0.5)
            hit8 = blocks(lambda st, s, cc: cc + _fold_rows(
                jnp.where((s == lo) & ((st + rowk).astype(F32) <= mid), 1.0, 0.0), jnp.sum),
                jnp.zeros((FOLD_ROWS, R), F32))
            ok = jnp.sum(hit8, axis=0, keepdims=True) >= want
            return jnp.where(ok, a, mid), jnp.where(ok, mid, b)

        _, b = lax.fori_loop(0, int(np.ceil(np.log2(n_keys))) + 1, jb,
                             (jnp.full((1, R), -1.0, F32), jnp.full((1, R), float(n_keys - 1), F32)))
        jcut_ref[...] = jnp.where(tied, b, float(n_keys))

    q_stack = _head_stack((q_ref[0].astype(F32) * (HEAD_DIM ** -0.5)).astype(BF16))
    m_ref[...] = jnp.full(m_ref.shape, M_FLOOR, F32)
    l_ref[...] = jnp.zeros(l_ref.shape, F32)
    acc_ref[...] = jnp.zeros(acc_ref.shape, F32)
    jcut = jcut_ref[...]
    AB = DSA_ATT_BLOCK
    nab = (j * R + R + AB - 1) // AB
    rowf = lax.broadcasted_iota(I32, (AB, R), 0).astype(F32)

    def qk(kb):
        st = pl.multiple_of(kb * AB, AB)
        return _dot_nt(k_ref[0, pl.ds(st, AB), :], q_stack)

    qk_ref[...] = qk(0)

    def att_body(kb, c):
        st = pl.multiple_of(kb * AB, AB)
        s_raw = qk_ref[...]
        qk_ref[...] = qk(jnp.minimum(kb + 1, nab - 1))
        sc = sc_ref[pl.ds(st, AB), :]
        keep = (sc > lo) | ((sc == lo) & (rowf <= jcut - st.astype(F32)))
        bias = jnp.where(keep, 0.0, NEG)
        s = s_raw + jnp.concatenate([bias] * GROUP_HEADS, axis=1)
        m_old = m_ref[...]
        m_new = jnp.maximum(m_old, jnp.max(s, axis=0, keepdims=True))
        alpha = jnp.exp(m_old - m_new)
        p = jnp.exp(s - m_new)
        l_ref[...] = alpha * l_ref[...] + jnp.sum(p, axis=0, keepdims=True)
        acc_ref[...] = alpha * acc_ref[...] + _dot_tn(v_ref[0, pl.ds(st, AB), :], p.astype(BF16))
        m_ref[...] = m_new
        return c

    lax.fori_loop(0, nab, att_body, 0)
    o_ref[0] = _head_unstack((acc_ref[...] / l_ref[...]).T, R).astype(o_ref.dtype)


def _dsa(q, k, v, qi, ki, wt):
    B, T, _ = q.shape
    KB = DSA_KEY_BLOCK
    assert T % KB == 0
    n_sel = min(DSA_TOPK, T // 4)
    R = DSA_Q
    assert T % R == 0
    SR = GROUP_HEADS * R
    nq = T // R
    qblk = lambda wd: pl.BlockSpec((1, R, wd), lambda b, j: (b, j, 0))
    full = lambda wd: pl.BlockSpec((1, T, wd), lambda b, j: (b, 0, 0))
    return pl.pallas_call(
        functools.partial(_dsa_kernel, n_sel=n_sel, n_keys=T),
        grid=(B, nq),
        in_specs=[qblk(GROUP_WIDTH), full(GROUP_WIDTH), full(GROUP_WIDTH),
                  qblk(IDX_HEADS * IDX_DIM), full(IDX_DIM),
                  pl.BlockSpec((IDX_HEADS, R), lambda b, j: (0, b * nq + j))],
        out_specs=qblk(GROUP_WIDTH),
        out_shape=jax.ShapeDtypeStruct((B, T, GROUP_WIDTH), BF16),
        scratch_shapes=[pltpu.VMEM((T, R), F32), pltpu.VMEM((1, R), F32), pltpu.VMEM((DSA_ATT_BLOCK, SR), F32),
                        pltpu.VMEM((1, SR), F32), pltpu.VMEM((1, SR), F32),
                        pltpu.VMEM((GROUP_WIDTH, SR), F32)],
        compiler_params=_cparams(2),
        name="dsa_attention",
    )(q, k, v, qi, ki, wt)


def _outproj_kernel(oa, orr, oc, od, x_ref, w_ref, g_ref, b_ref, rwt_ref, rb_ref, utri_ref,
                    x1_ref, ti_ref, tg_ref, rk_ref, cnt_ref, run_ref):
    GW = GROUP_WIDTH

    @pl.when(pl.program_id(0) == 0)
    def _():
        run_ref[...] = jnp.zeros_like(run_ref)

    acc = _dot(oa[...], w_ref[0:GW, :])
    acc = acc + _dot(orr[...], w_ref[GW:2 * GW, :])
    acc = acc + _dot(oc[...], w_ref[2 * GW:3 * GW, :])
    acc = acc + _dot(od[...], w_ref[3 * GW:4 * GW, :])
    x1 = _layer_norm_rows(ALPHA * x_ref[...] + acc, g_ref[...], b_ref[...])
    x1_ref[...] = x1

    logits = _dot_nt(rwt_ref[...], x1.astype(BF16)) + rb_ref[...]
    row = lax.broadcasted_iota(I32, logits.shape, 0)
    krow = lax.broadcasted_iota(I32, ti_ref.shape, 0)
    g = logits
    ti = jnp.zeros(ti_ref.shape, I32)
    tv = jnp.zeros(tg_ref.shape, F32)
    picks = []
    for kk in range(TOP_K):
        mx = jnp.max(g, axis=0, keepdims=True)
        first = jnp.min(jnp.where(g == mx, row, N_EXPERTS), axis=0, keepdims=True)
        ti = jnp.where(krow == kk, first, ti)
        tv = jnp.where(krow == kk, mx, tv)
        picks.append(row == first)
        g = jnp.where(picks[-1], -jnp.inf, g)
    e = jnp.exp(tv - jnp.max(tv, axis=0, keepdims=True))
    ti_ref[...] = ti
    tg_ref[...] = e / jnp.sum(e, axis=0, keepdims=True)

    sel = jnp.where(picks[0] | picks[1] | picks[2] | picks[3], 1.0, 0.0)
    before = run_ref[...] + _dot(sel.astype(BF16), utri_ref[...])
    rk = jnp.zeros(rk_ref.shape, F32)
    for kk in range(TOP_K):
        rk = jnp.where(krow == kk, jnp.sum(jnp.where(picks[kk], before, 0.0), axis=0, keepdims=True), rk)
    rk_ref[...] = rk.astype(I32)
    run_ref[...] = run_ref[...] + jnp.sum(sel, axis=1, keepdims=True)
    cnt_ref[...] = run_ref[...]


def _outproj(oa, orr, oc, od, x2, w_out, g, b, rwt, rb):
    N = x2.shape[0]
    tm = OUT_TM
    row = lambda w: pl.BlockSpec((tm, w), lambda i: (i, 0))
    picks = pl.BlockSpec((TOP_K, tm), lambda i: (0, i))
    const = lambda a: pl.BlockSpec(a.shape, lambda i: (0,) * a.ndim)
    utri = jnp.asarray(np.triu(np.ones((tm, tm), np.float32), 1), dtype=BF16)
    return pl.pallas_call(
        _outproj_kernel,
        grid=(N // tm,),
        in_specs=[row(GROUP_WIDTH)] * 4 + [row(D_MODEL), const(w_out), const(g), const(b), const(rwt), const(rb),
                                           const(utri)],
        out_specs=[row(D_MODEL), picks, picks, picks, pl.BlockSpec((N_EXPERTS, 1), lambda i: (0, 0))],
        out_shape=[jax.ShapeDtypeStruct((N, D_MODEL), F32), jax.ShapeDtypeStruct((TOP_K, N), I32),
                   jax.ShapeDtypeStruct((TOP_K, N), F32), jax.ShapeDtypeStruct((TOP_K, N), I32),
                   jax.ShapeDtypeStruct((N_EXPERTS, 1), F32)],
        scratch_shapes=[pltpu.VMEM((N_EXPERTS, 1), F32)],
        compiler_params=_cparams(1),
        name="outproj_ln_router",
    )(oa, orr, oc, od, x2, w_out, g, b, rwt, rb, utri)


def _dispatch_kernel(tv_ref, pos_ref, x_ref, xs_hbm, xbuf, sem, zsem, *, n_tiles):
    i = pl.program_id(0)
    n = pl.num_programs(0)
    tm = DSP_TM
    par = i % 2

    @pl.when(i == 0)
    def _():
        xbuf[1] = jnp.zeros((tm, D_MODEL), F32)

        def fill(t, c):
            @pl.when(tv_ref[t] < MOE_TM)
            def _():
                pltpu.make_async_copy(xbuf.at[1], xs_hbm.at[pl.ds(pl.multiple_of(t * MOE_TM, MOE_TM), MOE_TM)],
                                      zsem).start()
            return c

        def drain(t, c):
            @pl.when(tv_ref[t] < MOE_TM)
            def _():
                pltpu.make_async_copy(xbuf.at[1], xs_hbm.at[pl.ds(0, MOE_TM)], zsem).wait()
            return c

        lax.fori_loop(0, n_tiles, fill, 0)
        lax.fori_loop(0, n_tiles, drain, 0)

    def wait_step(p):
        for _ in range(TOP_K):
            pltpu.make_async_copy(xbuf.at[p], xbuf.at[p], sem.at[p]).wait()

    @pl.when(i >= 2)
    def _():
        wait_step(par)

    xbuf[par] = x_ref[...]

    def body(r, c):
        for kk in range(TOP_K):
            pltpu.make_async_copy(xbuf.at[par, pl.ds(r, 1)], xs_hbm.at[pl.ds(pos_ref[r * TOP_K + kk], 1)],
                                  sem.at[par]).start()
        return c

    lax.fori_loop(0, tm, body, 0, unroll=4)

    @pl.when(i == n - 1)
    def _():
        wait_step(1 - par)
        wait_step(par)


def _dispatch(pos, x1, tile_valid):
    N = x1.shape[0]
    tm = DSP_TM
    n_tiles = tile_valid.shape[0]
    assert N // tm >= 2 and tm == MOE_TM
    grid_spec = pltpu.PrefetchScalarGridSpec(
        num_scalar_prefetch=1,
        grid=(N // tm,),
        in_specs=[pl.BlockSpec((tm * TOP_K,), lambda i, tv: (i,), memory_space=pltpu.SMEM),
                  pl.BlockSpec((tm, D_MODEL), lambda i, tv: (i, 0))],
        out_specs=pl.BlockSpec(memory_space=pl.ANY),
        scratch_shapes=[pltpu.VMEM((2, tm, D_MODEL), F32), pltpu.SemaphoreType.DMA((2,)),
                        pltpu.SemaphoreType.DMA(())],
    )
    return pl.pallas_call(
        functools.partial(_dispatch_kernel, n_tiles=n_tiles),
        grid_spec=grid_spec,
        out_shape=jax.ShapeDtypeStruct((n_tiles * MOE_TM, D_MODEL), F32),
        compiler_params=_cparams(1),
        name="moe_dispatch",
    )(tile_valid, pos, x1)


def _moe_kernel(te_ref, nv_ref, x_ref, w1_ref, b1_ref, w2_ref, b2_ref, y_ref, w1b, w2b):
    i = pl.program_id(0)
    tm = MOE_TM
    n_valid = nv_ref[i]

    @pl.when((i == 0) | (te_ref[i] != te_ref[jnp.maximum(i - 1, 0)]))
    def _():
        step = LANES
        for c in range(D_MODEL // step):
            w1b[c * step:(c + 1) * step, :] = w1_ref[0, 0, c * step:(c + 1) * step, :].astype(BF16)
        for c in range(D_FF // step):
            w2b[c * step:(c + 1) * step, :] = w2_ref[0, 0, c * step:(c + 1) * step, :].astype(BF16)

    @pl.when(n_valid > 0)
    def _():
        h = _dot(x_ref[...].astype(BF16), w1b[...]) + b1_ref[0]
        glu_in = jnp.minimum(h[:, :D_FF], SWIGLU_LIMIT)
        up = jnp.clip(h[:, D_FF:], -SWIGLU_LIMIT, SWIGLU_LIMIT)
        glu = glu_in * jax.nn.sigmoid(SWIGLU_ALPHA * glu_in)
        y_ref[...] = _dot(((up + 1.0) * glu).astype(BF16), w2b[...]) + b2_ref[0]

    @pl.when(n_valid == 0)
    def _():
        y_ref[...] = jnp.zeros_like(y_ref)


def _moe_experts(xs, tile_expert, tile_valid, w1, b1, w2, b2, layer):
    tm = MOE_TM
    n_tiles = tile_expert.shape[0]
    grid_spec = pltpu.PrefetchScalarGridSpec(
        num_scalar_prefetch=2,
        grid=(n_tiles,),
        in_specs=[
            pl.BlockSpec((tm, D_MODEL), lambda i, te, nv: (i, 0)),
            pl.BlockSpec((1, 1, D_MODEL, 2 * D_FF), lambda i, te, nv: (layer, te[i], 0, 0)),
            pl.BlockSpec((1, 1, 2 * D_FF), lambda i, te, nv: (te[i], 0, 0)),
            pl.BlockSpec((1, 1, D_FF, D_MODEL), lambda i, te, nv: (layer, te[i], 0, 0)),
            pl.BlockSpec((1, 1, D_MODEL), lambda i, te, nv: (te[i], 0, 0)),
        ],
        out_specs=pl.BlockSpec((tm, D_MODEL), lambda i, te, nv: (i, 0)),
        scratch_shapes=[pltpu.VMEM((D_MODEL, 2 * D_FF), BF16), pltpu.VMEM((D_FF, D_MODEL), BF16)],
    )
    return pl.pallas_call(
        _moe_kernel,
        grid_spec=grid_spec,
        out_shape=jax.ShapeDtypeStruct((n_tiles * tm, D_MODEL), F32),
        compiler_params=_cparams(1),
        name="moe_experts",
    )(tile_expert, tile_valid, xs, w1, b1, w2, b2)


def _combine_kernel(pos_ref, posn_ref, y_hbm, x1_ref, tg_ref, g_ref, b_ref, o_ref, ybuf, sem):
    i = pl.program_id(0)
    n = pl.num_programs(0)
    tm = CMB_TM
    slot = i % 2

    def issue(idx_ref, s):
        def body(r, c):
            for kk in range(TOP_K):
                pltpu.make_async_copy(y_hbm.at[pl.ds(idx_ref[r * TOP_K + kk], 1)],
                                      ybuf.at[s, kk, pl.ds(r, 1)], sem.at[s]).start()
            return c
        lax.fori_loop(0, tm, body, 0, unroll=4)

    @pl.when(i == 0)
    def _():
        issue(pos_ref, 0)

    @pl.when(i + 1 < n)
    def _():
        issue(posn_ref, 1 - slot)

    pltpu.make_async_copy(ybuf.at[slot], ybuf.at[slot], sem.at[slot]).wait()
    tg = tg_ref[...]
    moe = tg[:, 0:1] * ybuf[slot, 0]
    for kk in range(1, TOP_K):
        moe = moe + tg[:, kk:kk + 1] * ybuf[slot, kk]
    o_ref[...] = _layer_norm_rows(ALPHA * x1_ref[...] + moe, g_ref[...], b_ref[...])


def _combine(pos, y_sorted, x1, tg, g, b):
    N = x1.shape[0]
    tm = CMB_TM
    n = N // tm
    row = lambda w: pl.BlockSpec((tm, w), lambda i: (i, 0))
    const = lambda a: pl.BlockSpec(a.shape, lambda i: (0,) * a.ndim)
    return pl.pallas_call(
        _combine_kernel,
        grid=(n,),
        in_specs=[pl.BlockSpec((tm * TOP_K,), lambda i: (i,), memory_space=pltpu.SMEM),
                  pl.BlockSpec((tm * TOP_K,), lambda i: (jnp.minimum(i + 1, n - 1),), memory_space=pltpu.SMEM),
                  pl.BlockSpec(memory_space=pl.ANY), row(D_MODEL), row(TOP_K), const(g), const(b)],
        out_specs=row(D_MODEL),
        out_shape=jax.ShapeDtypeStruct((N, D_MODEL), F32),
        scratch_shapes=[pltpu.VMEM((2, TOP_K, tm, D_MODEL), F32), pltpu.SemaphoreType.DMA((2,))],
        compiler_params=_cparams(1),
        name="moe_combine_ln",
    )(pos, pos, y_sorted, x1, tg, g, b)


def _routing_tables(top_i, rank, counts_f, n_tiles):
    tm = MOE_TM
    counts = counts_f.reshape(-1).astype(I32)
    padded = ((counts + tm - 1) // tm) * tm
    ends = jnp.cumsum(padded)
    offsets = ends - padded
    onehot = top_i[:, :, None] == jnp.arange(N_EXPERTS, dtype=I32)[None, None, :]
    pos = jnp.sum(jnp.where(onehot, offsets[None, None, :], 0), axis=-1) + rank
    tile_start = jnp.arange(n_tiles, dtype=I32) * tm
    tile_expert = jnp.sum((ends[None, :] <= tile_start[:, None]).astype(I32), axis=1)
    tile_expert = jnp.minimum(tile_expert, N_EXPERTS - 1)
    n_used = ends[-1] // tm
    last_expert = tile_expert[jnp.maximum(n_used - 1, 0)]
    tile_expert = jnp.where(tile_start < ends[-1], tile_expert, last_expert)
    valid_end = (offsets + counts)[tile_expert]
    tile_valid = jnp.clip(valid_end - tile_start, 0, tm)
    return pos.reshape(-1).astype(I32), tile_expert.astype(I32), tile_valid.astype(I32)


def _rope_tables(T):
    inv = ROPE_THETA ** (-jnp.arange(0, HEAD_DIM, 2, dtype=F32) / HEAD_DIM)
    ang = jnp.arange(T, dtype=F32)[:, None] * inv[None, :]
    cos, sin = jnp.cos(ang), jnp.sin(ang)
    cos_t = jnp.tile(jnp.concatenate([cos, cos], axis=-1), (1, GROUP_HEADS))
    sin_t = jnp.tile(jnp.concatenate([-sin, sin], axis=-1), (1, GROUP_HEADS))
    return cos_t, sin_t


def _pad_w_in(w_in):
    base = 12 * GROUP_WIDTH + IDX_HEADS * IDX_DIM
    w = jnp.zeros((D_MODEL, IN_PAD), F32)
    w = w.at[:, :base + IDX_DIM].set(w_in[:, :base + IDX_DIM])
    w = w.at[:, base + LANES:base + LANES + IDX_HEADS].set(w_in[:, base + IDX_DIM:])
    return w.astype(BF16)


def _layer(x2, B, T, cos_t, sin_t, tabs, w_in, ret_gn_g, ret_gn_b, conv_w, conv_b, rg_wx, rg_bx, rg_wa,
           rg_ba, rg_lambda, w_out, ln1_g, ln1_b, router_w, router_b, exp_w1, exp_b1, exp_w2, exp_b2,
           ln2_g, ln2_b, layer):
    N = B * T
    r2 = lambda a: a.reshape(1, -1)
    (aq, ak, av, rq, rk, rv, rg, cx, cg, dq, dk, dv, dqi, dki, dwt) = _proj(x2, _pad_w_in(w_in), cos_t, sin_t, T)
    seq = lambda a: a.reshape(B, T, a.shape[-1])
    o_a = _moba(seq(aq), seq(ak), seq(av))
    o_r = _retention(seq(rq), seq(rk), seq(rv), seq(rg), r2(ret_gn_g), r2(ret_gn_b), tabs)
    o_c = _rglru(seq(cx), seq(cg), conv_w, r2(conv_b), _block_diag(rg_wx).astype(BF16), r2(rg_bx),
                 _block_diag(rg_wa).astype(BF16), r2(rg_ba), r2(rg_lambda))
    o_d = _dsa(seq(dq), seq(dk), seq(dv), seq(dqi), seq(dki), dwt)
    flat = lambda a: a.reshape(N, GROUP_WIDTH)
    x1, top_i, top_g, rank, counts = _outproj(flat(o_a), flat(o_r), flat(o_c), flat(o_d), x2,
                                              w_out.astype(BF16), r2(ln1_g), r2(ln1_b),
                                              router_w.T.astype(BF16), router_b.reshape(-1, 1))
    top_i, top_g, rank = top_i.T, top_g.T, rank.T
    n_tiles = (N * TOP_K) // MOE_TM + N_EXPERTS
    pos, tile_expert, tile_valid = _routing_tables(top_i, rank, counts, n_tiles)
    xs = _dispatch(pos, x1, tile_valid)
    y_sorted = _moe_experts(xs, tile_expert, tile_valid, exp_w1, exp_b1.reshape(N_EXPERTS, 1, -1), exp_w2,
                            exp_b2.reshape(N_EXPERTS, 1, -1), layer)
    return _combine(pos, y_sorted, x1, top_g, r2(ln2_g), r2(ln2_b))


def kernel(x, w_in, ret_gn_g, ret_gn_b, conv_w, conv_b, rg_wx, rg_bx, rg_wa, rg_ba, rg_lambda, w_out,
           ln1_g, ln1_b, router_w, router_b, exp_w1, exp_b1, exp_w2, exp_b2, ln2_g, ln2_b):
    B, T, D = x.shape
    cos_t, sin_t = _rope_tables(T)
    tabs = _ret_tables()
    x2 = x.reshape(B * T, D)
    for l in range(w_in.shape[0]):
        x2 = _layer(x2, B, T, cos_t, sin_t, tabs, w_in[l], ret_gn_g[l], ret_gn_b[l], conv_w[l], conv_b[l],
                    rg_wx[l], rg_bx[l], rg_wa[l], rg_ba[l], rg_lambda[l], w_out[l], ln1_g[l], ln1_b[l],
                    router_w[l], router_b[l], exp_w1, exp_b1[l], exp_w2, exp_b2[l], ln2_g[l], ln2_b[l], l)
    return x2.reshape(B, T, D)
```

```python
import functools

import numpy as np
import jax
import jax.numpy as jnp
from jax import lax
from jax.experimental import pallas as pl
from jax.experimental.pallas import tpu as pltpu

F32 = jnp.float32
BF16 = jnp.bfloat16
I32 = jnp.int32

D_MODEL = 1024
DEPTH = 2
HEAD_DIM = 64
GROUP_WIDTH = 256
GROUP_HEADS = 4
ROPE_THETA = 10000.0
Q_BLOCK = 128
MOBA_BLOCK = 256
MOBA_TOPK = 3
MOBA_MAX_BLOCKS = 16
RET_CHUNK = 128
RG_CONV = 4
RG_C = 8.0
IDX_HEADS = 8
IDX_DIM = 64
IDX_SCALE = (IDX_HEADS ** -0.5) * (IDX_DIM ** -0.5)
DSA_TOPK = 256
DSA_Q = 256
DSA_KEY_BLOCK = 512
DSA_ATT_BLOCK = 512
N_EXPERTS = 32
TOP_K = 4
D_FF = 1024
SWIGLU_LIMIT = 7.0
SWIGLU_ALPHA = 1.702
ALPHA = (2 * DEPTH) ** 0.25
LN_EPS = 1e-5
IN_WIDTH = 12 * GROUP_WIDTH + IDX_HEADS * IDX_DIM + IDX_DIM + IDX_HEADS
IN_PAD = 15 * GROUP_WIDTH

LANES = 128
SUBLANES = 8
HEAD_SHIFT = HEAD_DIM.bit_length() - 1
KEY_MAGNITUDE_BITS = 0x7FFFFFFF
BISECT_MAX_STEPS = 36
NEG = -1e30
M_FLOOR = -1e29
BIG = 1e30
VMEM_LIMIT = 56 * 1024 * 1024
FOLD_ROWS = 32

PROJ_TM = 256
OUT_TM = 512
RG_TC = 256
MOE_TM = 512
DSP_TM = 512
CMB_TM = 256


def _cparams(ndims):
    return pltpu.CompilerParams(dimension_semantics=("arbitrary",) * ndims,
                                vmem_limit_bytes=VMEM_LIMIT)


def _dot(a, b, precision=None):
    return jnp.dot(a, b, preferred_element_type=F32, precision=precision)


def _dot_nt(a, b, precision=None):
    return lax.dot_general(a, b, (((1,), (1,)), ((), ())), preferred_element_type=F32,
                           precision=precision)


def _dot_tn(a, b):
    return lax.dot_general(a, b, (((0,), (0,)), ((), ())), preferred_element_type=F32)


def _head_stack(q):
    head = lax.shift_right_logical(lax.broadcasted_iota(I32, q.shape, 1), HEAD_SHIFT)
    qf = q.astype(F32)
    return jnp.concatenate([jnp.where(head == h, qf, 0.0) for h in range(GROUP_HEADS)],
                           axis=0).astype(q.dtype)


def _head_unstack(s, rows):
    head = lax.shift_right_logical(lax.broadcasted_iota(I32, (rows, GROUP_WIDTH), 1), HEAD_SHIFT)
    out = jnp.zeros((rows, GROUP_WIDTH), F32)
    for h in range(GROUP_HEADS):
        out = out + jnp.where(head == h, s[h * rows:(h + 1) * rows], 0.0)
    return out


def _fold_rows(x, op):
    return op(x.reshape(x.shape[0] // FOLD_ROWS, FOLD_ROWS, x.shape[1]), axis=0)


def _layer_norm_rows(y, g, b):
    mu = jnp.mean(y, axis=-1, keepdims=True)
    yc = y - mu
    var = jnp.mean(yc * yc, axis=-1, keepdims=True)
    return yc * lax.rsqrt(var + LN_EPS) * g + b


def _proj_kernel(x_ref, w_ref, cos_ref, sin_ref,
                 aq, ak, av, rq, rk, rv, rg, cx, cg, dq, dk, dv, dqi, dki, dwt):
    xb = x_ref[...].astype(BF16)
    cos = cos_ref[...]
    sin = sin_ref[...]
    first_half = (lax.broadcasted_iota(I32, cos.shape, 1) & (HEAD_DIM - 1)) < (HEAD_DIM // 2)

    def seg(i):
        return _dot(xb, w_ref[:, i * GROUP_WIDTH:(i + 1) * GROUP_WIDTH])

    def rope(p):
        rot = jnp.where(first_half, pltpu.roll(p, GROUP_WIDTH - HEAD_DIM // 2, 1),
                        pltpu.roll(p, HEAD_DIM // 2, 1))
        return p * cos + rot * sin

    aq[...] = rope(seg(0)).astype(BF16)
    ak[...] = rope(seg(1)).astype(BF16)
    av[...] = seg(2).astype(BF16)
    rq[...] = rope(seg(3)).astype(BF16)
    rk[...] = (rope(seg(4)) * (HEAD_DIM ** -0.5)).astype(BF16)
    rv[...] = seg(5).astype(BF16)
    rg[...] = seg(6)
    cx[...] = seg(7)
    cg[...] = seg(8)
    dq[...] = rope(seg(9)).astype(BF16)
    dk[...] = rope(seg(10)).astype(BF16)
    dv[...] = seg(11).astype(BF16)
    dqi[:, 0:GROUP_WIDTH] = rope(seg(12)).astype(BF16)
    dqi[:, GROUP_WIDTH:2 * GROUP_WIDTH] = rope(seg(13)).astype(BF16)
    last = seg(14)
    dki[...] = rope(last)[:, 0:IDX_DIM].astype(BF16)
    dwt[...] = last[:, LANES:2 * LANES].T[0:IDX_HEADS, :]


def _proj(x2, w_pad, cos_t, sin_t, T):
    N = x2.shape[0]
    tm = PROJ_TM
    tpb = T // tm
    row = lambda w: pl.BlockSpec((tm, w), lambda i: (i, 0))
    tab = pl.BlockSpec((tm, GROUP_WIDTH), lambda i: (i % tpb, 0))
    widths = [256] * 12 + [512, IDX_DIM]
    dtypes = [BF16, BF16, BF16, BF16, BF16, BF16, F32, F32, F32, BF16, BF16, BF16, BF16, BF16]
    return pl.pallas_call(
        _proj_kernel,
        grid=(N // tm,),
        in_specs=[row(D_MODEL), pl.BlockSpec((D_MODEL, IN_PAD), lambda i: (0, 0)), tab, tab],
        out_specs=[row(w) for w in widths] + [pl.BlockSpec((IDX_HEADS, tm), lambda i: (0, i))],
        out_shape=[jax.ShapeDtypeStruct((N, w), d) for w, d in zip(widths, dtypes)]
        + [jax.ShapeDtypeStruct((IDX_HEADS, N), F32)],
        compiler_params=_cparams(1),
        name="proj_rope",
    )(x2, w_pad, cos_t, sin_t)


def _moba_kernel(q_ref, k_ref, v_ref, o_ref, kmean_ref, sel_ref, qk_ref, m_ref, l_ref, acc_ref, *, n_blocks):
    j = pl.program_id(1)
    R = Q_BLOCK
    SR = GROUP_HEADS * R
    KB = MOBA_BLOCK

    @pl.when(j == 0)
    def _():
        kmean_ref[...] = jnp.zeros_like(kmean_ref)
        for n in range(n_blocks):
            kb = k_ref[0, n * KB:(n + 1) * KB, :].astype(F32)
            kmean_ref[n:n + 1, :] = jnp.mean(kb, axis=0, keepdims=True)

    own = j // (KB // R)
    q_raw = _head_stack(q_ref[0])
    q_stack = (q_raw.astype(F32) * (HEAD_DIM ** -0.5)).astype(BF16)

    gate = _dot_nt(kmean_ref[...], q_raw.astype(F32), precision=lax.Precision.HIGHEST)
    blk = lax.broadcasted_iota(I32, gate.shape, 0)
    past = blk < own
    g = jnp.where(past, gate, -jnp.inf)
    sel = jnp.zeros(gate.shape, F32)
    for _ in range(MOBA_TOPK):
        mx = jnp.max(g, axis=0, keepdims=True)
        first = jnp.min(jnp.where(g == mx, blk, MOBA_MAX_BLOCKS), axis=0, keepdims=True)
        pick = blk == first
        sel = jnp.where(pick & past, 1.0, sel)
        g = jnp.where(pick, -jnp.inf, g)
    sel_ref[...] = sel

    m_ref[...] = jnp.full(m_ref.shape, M_FLOOR, F32)
    l_ref[...] = jnp.zeros(l_ref.shape, F32)
    acc_ref[...] = jnp.zeros(acc_ref.shape, F32)

    def softmax_pv(s_raw, vb, bias):
        s = s_raw + bias
        m_old = m_ref[...]
        m_new = jnp.maximum(m_old, jnp.max(s, axis=0, keepdims=True))
        alpha = jnp.exp(m_old - m_new)
        p = jnp.exp(s - m_new)
        l_ref[...] = alpha * l_ref[...] + jnp.sum(p, axis=0, keepdims=True)
        acc_ref[...] = alpha * acc_ref[...] + _dot_tn(vb, p.astype(BF16))
        m_ref[...] = m_new

    def block_bias(n):
        return jnp.broadcast_to(jnp.where(sel_ref[pl.ds(n, 1), :] > 0.5, 0.0, NEG), (KB, SR))

    def qk(first_block, rows):
        st = pl.multiple_of(first_block * KB, KB)
        return _dot_nt(k_ref[0, pl.ds(st, rows), :], q_stack)

    def attend_span(first_block, bias, s_raw=None):
        st = pl.multiple_of(first_block * KB, KB)
        rows = bias.shape[0]
        s_raw = qk(first_block, rows) if s_raw is None else s_raw
        softmax_pv(s_raw, v_ref[0, pl.ds(st, rows), :], bias)

    keypos = own * KB + lax.broadcasted_iota(I32, (KB, SR), 0)
    qpos = j * R + (lax.broadcasted_iota(I32, (KB, SR), 1) & (R - 1))
    causal = jnp.where(keypos <= qpos, 0.0, NEG)
    odd = (own & 1) == 1

    @pl.when(odd)
    def _():
        attend_span(own - 1, jnp.concatenate([block_bias(own - 1), causal], axis=0))

    @pl.when(jnp.logical_not(odd))
    def _():
        attend_span(own, causal)

    n_pairs = own // 2

    @pl.when(n_pairs > 0)
    def _():
        qk_ref[...] = qk(0, 2 * KB)

    def body(pair, c):
        s_raw = qk_ref[...]
        qk_ref[...] = qk(2 * jnp.minimum(pair + 1, n_pairs - 1), 2 * KB)
        attend_span(2 * pair, jnp.concatenate([block_bias(2 * pair), block_bias(2 * pair + 1)], axis=0), s_raw)
        return c

    lax.fori_loop(0, n_pairs, body, 0)
    o_ref[0] = _head_unstack((acc_ref[...] / l_ref[...]).T, R).astype(o_ref.dtype)


def _moba(q, k, v):
    B, T, _ = q.shape
    n_blocks = T // MOBA_BLOCK
    assert T % MOBA_BLOCK == 0 and n_blocks <= MOBA_MAX_BLOCKS
    SR = GROUP_HEADS * Q_BLOCK
    return pl.pallas_call(
        functools.partial(_moba_kernel, n_blocks=n_blocks),
        grid=(B, T // Q_BLOCK),
        in_specs=[pl.BlockSpec((1, Q_BLOCK, GROUP_WIDTH), lambda b, j: (b, j, 0)),
                  pl.BlockSpec((1, T, GROUP_WIDTH), lambda b, j: (b, 0, 0)),
                  pl.BlockSpec((1, T, GROUP_WIDTH), lambda b, j: (b, 0, 0))],
        out_specs=pl.BlockSpec((1, Q_BLOCK, GROUP_WIDTH), lambda b, j: (b, j, 0)),
        out_shape=jax.ShapeDtypeStruct((B, T, GROUP_WIDTH), BF16),
        scratch_shapes=[pltpu.VMEM((MOBA_MAX_BLOCKS, GROUP_WIDTH), F32),
                        pltpu.VMEM((MOBA_MAX_BLOCKS, SR), F32),
                        pltpu.VMEM((2 * MOBA_BLOCK, SR), F32),
                        pltpu.VMEM((1, SR), F32), pltpu.VMEM((1, SR), F32),
                        pltpu.VMEM((GROUP_WIDTH, SR), F32)],
        compiler_params=_cparams(2),
        name="moba_attention",
    )(q, k, v)


def _ret_kernel(q_ref, k_ref, v_ref, g_ref, dmask_ref, xi_ref, zeta_ref, gdec_ref, bd_ref, avg_ref,
                gng_ref, gnb_ref, o_ref, r_ref):
    j = pl.program_id(0)

    @pl.when(j == 0)
    def _():
        r_ref[...] = jnp.zeros_like(r_ref)

    C = RET_CHUNK
    hp = lax.Precision.HIGHEST
    for b in range(q_ref.shape[0]):
        q = q_ref[b]
        k = k_ref[b]
        v = v_ref[b]
        q_stack = _head_stack(q)
        inner = _dot_nt(q_stack, k) * dmask_ref[...]
        o = _head_unstack(_dot(inner.astype(BF16), v), C)
        R = r_ref[b]
        o = o + _dot(q, R.astype(BF16)) * xi_ref[...]
        kz = (k.astype(F32) * zeta_ref[...]).astype(BF16)
        r_ref[b] = gdec_ref[...] * R + bd_ref[...] * _dot_tn(kz, v)

        mu = _dot(o, avg_ref[...], precision=hp)
        oc = o - mu
        var = _dot(oc * oc, avg_ref[...], precision=hp)
        y = oc * lax.rsqrt(var + LN_EPS) * gng_ref[...] + gnb_ref[...]
        gte = g_ref[b]
        o_ref[b] = (y * (gte * jax.nn.sigmoid(gte))).astype(o_ref.dtype)


def _ret_tables():
    H, C, d = GROUP_HEADS, RET_CHUNK, HEAD_DIM
    log_g = np.log(1.0 - 2.0 ** (-5.0 - np.arange(H, dtype=np.float64)))
    n = np.arange(C, dtype=np.float64)
    diff = n[:, None] - n[None, :]
    dmask = np.where(diff >= 0, np.exp(log_g[:, None, None] * np.maximum(diff, 0.0)), 0.0)
    xi = np.exp(log_g[:, None] * (n + 1.0))
    zeta = np.exp(log_g[:, None] * (C - 1.0 - n))
    g_chunk = np.exp(log_g * C)
    head = np.arange(GROUP_WIDTH) // d
    bd = (head[:, None] == head[None, :]).astype(np.float64)
    to32 = lambda a: jnp.asarray(a, dtype=F32)
    return dict(dmask=to32(dmask.reshape(H * C, C)), xi=to32(xi.T[:, head]), zeta=to32(zeta.T[:, head]),
                gdec=to32(bd * g_chunk[head][:, None]), bd=to32(bd), avg=to32(bd / d))


def _retention(rq, rk, rv, rg, gn_g, gn_b, tabs):
    B, T, _ = rq.shape
    C = RET_CHUNK
    blk = pl.BlockSpec((B, C, GROUP_WIDTH), lambda j: (0, j, 0))
    const = lambda a: pl.BlockSpec(a.shape, lambda j: (0,) * a.ndim)
    consts = [tabs["dmask"], tabs["xi"], tabs["zeta"], tabs["gdec"], tabs["bd"], tabs["avg"], gn_g, gn_b]
    return pl.pallas_call(
        _ret_kernel,
        grid=(T // C,),
        in_specs=[blk, blk, blk, blk] + [const(a) for a in consts],
        out_specs=blk,
        out_shape=jax.ShapeDtypeStruct((B, T, GROUP_WIDTH), BF16),
        scratch_shapes=[pltpu.VMEM((B, GROUP_WIDTH, GROUP_WIDTH), F32)],
        compiler_params=_cparams(1),
        name="retention",
    )(rq, rk, rv, rg, *consts)


def _rglru_kernel(x_ref, g_ref, cw_ref, cb_ref, wx_ref, bx_ref, wa_ref, ba_ref, lam_ref, o_ref,
                  xbuf, h_ref):
    j = pl.program_id(1)
    tc = RG_TC

    @pl.when(j == 0)
    def _():
        xbuf[0:SUBLANES, :] = jnp.zeros((SUBLANES, GROUP_WIDTH), F32)
        h_ref[...] = jnp.zeros_like(h_ref)

    xbuf[SUBLANES:SUBLANES + tc, :] = x_ref[0]
    xc = cb_ref[...] + cw_ref[RG_CONV - 1:RG_CONV, :] * xbuf[SUBLANES:SUBLANES + tc, :]
    for i in range(RG_CONV - 1):
        off = SUBLANES - (RG_CONV - 1) + i
        xc = xc + cw_ref[i:i + 1, :] * xbuf[off:off + tc, :]
    xbuf[0:SUBLANES, :] = xbuf[tc:tc + SUBLANES, :]

    xcb = xc.astype(BF16)
    gate_x = jax.nn.sigmoid(_dot(xcb, wx_ref[...]) + bx_ref[...])
    gate_a = jax.nn.sigmoid(_dot(xcb, wa_ref[...]) + ba_ref[...])
    lam = lam_ref[...]
    softplus_neg = jnp.maximum(-lam, 0.0) + jnp.log1p(jnp.exp(-jnp.abs(lam)))
    log_a = -RG_C * gate_a * softplus_neg
    a = jnp.exp(log_a)
    th = jnp.tanh(log_a)
    b = jnp.sqrt(-2.0 * th / (1.0 - th)) * (gate_x * xc)

    row = lax.broadcasted_iota(I32, (tc, GROUP_WIDTH), 0)
    d = 1
    while d < tc:
        keep = row >= d
        a_sh = jnp.where(keep, pltpu.roll(a, d, 0), 1.0)
        b_sh = jnp.where(keep, pltpu.roll(b, d, 0), 0.0)
        b = a * b_sh + b
        a = a * a_sh
        d *= 2
    h = b + a * h_ref[...]
    h_ref[...] = h[tc - 1:tc, :]

    xg = g_ref[0]
    gelu = 0.5 * xg * (1.0 + jnp.tanh(np.sqrt(2.0 / np.pi) * (xg + 0.044715 * xg * xg * xg)))
    o_ref[0] = (h * gelu).astype(o_ref.dtype)


def _block_diag(w):
    n, c, _ = w.shape
    eye = jnp.eye(n, dtype=w.dtype)
    return (eye[:, None, :, None] * w[:, :, None, :]).reshape(n * c, n * c)


def _rglru(cx, cg, conv_w, conv_b, wx, bx, wa, ba, lam):
    B, T, _ = cx.shape
    tc = RG_TC
    blk = pl.BlockSpec((1, tc, GROUP_WIDTH), lambda b, j: (b, j, 0))
    const = lambda a: pl.BlockSpec(a.shape, lambda b, j: (0,) * a.ndim)
    consts = [conv_w, conv_b, wx, bx, wa, ba, lam]
    return pl.pallas_call(
        _rglru_kernel,
        grid=(B, T // tc),
        in_specs=[blk, blk] + [const(a) for a in consts],
        out_specs=blk,
        out_shape=jax.ShapeDtypeStruct((B, T, GROUP_WIDTH), BF16),
        scratch_shapes=[pltpu.VMEM((tc + SUBLANES, GROUP_WIDTH), F32), pltpu.VMEM((1, GROUP_WIDTH), F32)],
        compiler_params=_cparams(2),
        name="rg_lru",
    )(cx, cg, *consts)


def _dsa_kernel(q_ref, k_ref, v_ref, qi_ref, ki_ref, wt_ref, o_ref,
                sc_ref, jcut_ref, qk_ref, m_ref, l_ref, acc_ref, *, n_sel, n_keys):
    j = pl.program_id(1)
    R = DSA_Q
    KB = DSA_KEY_BLOCK
    SR = GROUP_HEADS * R
    nkb = (j * R + R + KB - 1) // KB
    nsel = float(n_sel)

    rowk = lax.broadcasted_iota(I32, (KB, R), 0)
    qpos = j * R + lax.broadcasted_iota(I32, (KB, R), 1)

    qi = qi_ref[0]
    qi_stack = jnp.concatenate([qi[:, h * IDX_DIM:(h + 1) * IDX_DIM] for h in range(IDX_HEADS)], axis=0)
    wt = wt_ref[...]

    def score_body(kb, c):
        st = pl.multiple_of(kb * KB, KB)
        rel = _dot_nt(ki_ref[0, pl.ds(st, KB), :], qi_stack)
        sc = wt[0:1, :] * jnp.maximum(rel[:, 0:R], 0.0)
        for h in range(1, IDX_HEADS):
            sc = sc + wt[h:h + 1, :] * jnp.maximum(rel[:, h * R:(h + 1) * R], 0.0)
        sc_ref[pl.ds(st, KB), :] = jnp.where(st + rowk <= qpos, sc * IDX_SCALE, NEG)
        return c

    lax.fori_loop(0, nkb, score_body, 0)

    def blocks(fn, init):
        def body(kb, c):
            st = pl.multiple_of(kb * KB, KB)
            return fn(st, sc_ref[pl.ds(st, KB), :], c)
        return lax.fori_loop(0, nkb, body, init)

    def survey(st, s, c):
        mn, mx, gt, ge = c
        return (jnp.minimum(mn, _fold_rows(jnp.where(s > 0.5 * NEG, s, BIG), jnp.min)),
                jnp.maximum(mx, _fold_rows(s, jnp.max)),
                gt + _fold_rows(jnp.where(s > 0.0, 1.0, 0.0), jnp.sum),
                ge + _fold_rows(jnp.where(s >= 0.0, 1.0, 0.0), jnp.sum))

    zeros_fold = jnp.zeros((FOLD_ROWS, R), F32)
    mn8, mx8, gt8, ge8 = blocks(survey, (jnp.full((FOLD_ROWS, R), BIG, F32), jnp.full((FOLD_ROWS, R), NEG, F32),
                                         zeros_fold, zeros_fold))
    mn = jnp.min(mn8, axis=0, keepdims=True)
    mx = jnp.max(mx8, axis=0, keepdims=True)
    c_gt0 = jnp.sum(gt8, axis=0, keepdims=True)
    c_ge0 = jnp.sum(ge8, axis=0, keepdims=True)

    def count_ge(th):
        acc = blocks(lambda st, s, c: c + _fold_rows(jnp.where(s >= th, 1.0, 0.0), jnp.sum),
                     jnp.zeros((FOLD_ROWS, R), F32))
        return jnp.sum(acc, axis=0, keepdims=True)

    n_adm = (j * R + 1 + lax.broadcasted_iota(I32, (1, R), 1)).astype(F32)
    need = n_adm > nsel
    lo0 = jnp.where(need, mn, 0.5 * NEG)
    above = need & (c_gt0 >= nsel)
    below = need & (c_ge0 < nsel)
    at_zero = need & jnp.logical_not(above | below)
    cgt0 = jnp.where(below, c_ge0, jnp.where(at_zero, c_gt0, 0.0))
    act0 = jnp.where(above | below, 1.0, 0.0)
    tie0 = jnp.where(at_zero, 1.0, 0.0)

    def to_key(f):
        b = lax.bitcast_convert_type(f, I32)
        return b ^ (lax.shift_right_arithmetic(b, 31) & KEY_MAGNITUDE_BITS)

    def from_key(kk):
        return lax.bitcast_convert_type(kk ^ (lax.shift_right_arithmetic(kk, 31) & KEY_MAGNITUDE_BITS), F32)

    def bis_step(klo, khi, cgt, act, tie):
        on = act > 0.0
        kmid = lax.shift_right_arithmetic(klo, 1) + lax.shift_right_arithmetic(khi, 1) + (klo & khi & 1)
        stuck = kmid == klo
        cnt = count_ge(from_key(kmid))
        go = on & jnp.logical_not(stuck)
        up = go & (cnt >= nsel)
        dn = go & (cnt < nsel)
        return (jnp.where(up, kmid, klo), jnp.where(dn, kmid, khi), jnp.where(dn, cnt, cgt),
                jnp.where(go & (cnt != nsel), 1.0, 0.0), jnp.where(on & stuck, 1.0, tie))

    def bis_cond(c):
        return (c[1] > 0.0) & (c[0] < BISECT_MAX_STEPS)

    def bis_body(c):
        flag = jnp.max(c[5])
        st = bis_step(*bis_step(*c[2:]))
        return (c[0] + 2, flag) + st

    klo0 = jnp.where(above, 1, jnp.where(at_zero, 0, to_key(lo0)))
    khi0 = jnp.where(below, 0, to_key(mx) + 1)
    res = lax.while_loop(bis_cond, bis_body, (jnp.int32(0), jnp.max(act0), klo0, khi0, cgt0, act0, tie0))
    lo, cgt, tie = from_key(res[2]), res[4], res[6]

    jcut_ref[...] = jnp.full((1, R), float(n_keys), F32)

    @pl.when(jnp.max(tie) > 0.0)
    def _():
        want = nsel - cgt
        tied = tie > 0.0

        def jb(it, c):
            a, b = c
            mid = jnp.floor((a + b) * 0.5)
            hit8 = blocks(lambda st, s, cc: cc + _fold_rows(
                jnp.where((s == lo) & ((st + rowk).astype(F32) <= mid), 1.0, 0.0), jnp.sum),
                jnp.zeros((FOLD_ROWS, R), F32))
            ok = jnp.sum(hit8, axis=0, keepdims=True) >= want
            return jnp.where(ok, a, mid), jnp.where(ok, mid, b)

        _, b = lax.fori_loop(0, int(np.ceil(np.log2(n_keys))) + 1, jb,
                             (jnp.full((1, R), -1.0, F32), jnp.full((1, R), float(n_keys - 1), F32)))
        jcut_ref[...] = jnp.where(tied, b, float(n_keys))

    q_stack = _head_stack((q_ref[0].astype(F32) * (HEAD_DIM ** -0.5)).astype(BF16))
    m_ref[...] = jnp.full(m_ref.shape, M_FLOOR, F32)
    l_ref[...] = jnp.zeros(l_ref.shape, F32)
    acc_ref[...] = jnp.zeros(acc_ref.shape, F32)
    jcut = jcut_ref[...]
    AB = DSA_ATT_BLOCK
    nab = (j * R + R + AB - 1) // AB
    rowf = lax.broadcasted_iota(I32, (AB, R), 0).astype(F32)

    def qk(kb):
        st = pl.multiple_of(kb * AB, AB)
        return _dot_nt(k_ref[0, pl.ds(st, AB), :], q_stack)

    qk_ref[...] = qk(0)

    def att_body(kb, c):
        st = pl.multiple_of(kb * AB, AB)
        s_raw = qk_ref[...]
        qk_ref[...] = qk(jnp.minimum(kb + 1, nab - 1))
        sc = sc_ref[pl.ds(st, AB), :]
        keep = (sc > lo) | ((sc == lo) & (rowf <= jcut - st.astype(F32)))
        bias = jnp.where(keep, 0.0, NEG)
        s = s_raw + jnp.concatenate([bias] * GROUP_HEADS, axis=1)
        m_old = m_ref[...]
        m_new = jnp.maximum(m_old, jnp.max(s, axis=0, keepdims=True))
        alpha = jnp.exp(m_old - m_new)
        p = jnp.exp(s - m_new)
        l_ref[...] = alpha * l_ref[...] + jnp.sum(p, axis=0, keepdims=True)
        acc_ref[...] = alpha * acc_ref[...] + _dot_tn(v_ref[0, pl.ds(st, AB), :], p.astype(BF16))
        m_ref[...] = m_new
        return c

    lax.fori_loop(0, nab, att_body, 0)
    o_ref[0] = _head_unstack((acc_ref[...] / l_ref[...]).T, R).astype(o_ref.dtype)


def _dsa(q, k, v, qi, ki, wt):
    B, T, _ = q.shape
    KB = DSA_KEY_BLOCK
    assert T % KB == 0
    n_sel = min(DSA_TOPK, T // 4)
    R = DSA_Q
    assert T % R == 0
    SR = GROUP_HEADS * R
    nq = T // R
    qblk = lambda wd: pl.BlockSpec((1, R, wd), lambda b, j: (b, j, 0))
    full = lambda wd: pl.BlockSpec((1, T, wd), lambda b, j: (b, 0, 0))
    return pl.pallas_call(
        functools.partial(_dsa_kernel, n_sel=n_sel, n_keys=T),
        grid=(B, nq),
        in_specs=[qblk(GROUP_WIDTH), full(GROUP_WIDTH), full(GROUP_WIDTH),
                  qblk(IDX_HEADS * IDX_DIM), full(IDX_DIM),
                  pl.BlockSpec((IDX_HEADS, R), lambda b, j: (0, b * nq + j))],
        out_specs=qblk(GROUP_WIDTH),
        out_shape=jax.ShapeDtypeStruct((B, T, GROUP_WIDTH), BF16),
        scratch_shapes=[pltpu.VMEM((T, R), F32), pltpu.VMEM((1, R), F32), pltpu.VMEM((DSA_ATT_BLOCK, SR), F32),
                        pltpu.VMEM((1, SR), F32), pltpu.VMEM((1, SR), F32),
                        pltpu.VMEM((GROUP_WIDTH, SR), F32)],
        compiler_params=_cparams(2),
        name="dsa_attention",
    )(q, k, v, qi, ki, wt)


def _outproj_kernel(oa, orr, oc, od, x_ref, w_ref, g_ref, b_ref, rwt_ref, rb_ref, utri_ref,
                    x1_ref, ti_ref, tg_ref, rk_ref, cnt_ref, run_ref):
    GW = GROUP_WIDTH

    @pl.when(pl.program_id(0) == 0)
    def _():
        run_ref[...] = jnp.zeros_like(run_ref)

    acc = _dot(oa[...], w_ref[0:GW, :])
    acc = acc + _dot(orr[...], w_ref[GW:2 * GW, :])
    acc = acc + _dot(oc[...], w_ref[2 * GW:3 * GW, :])
    acc = acc + _dot(od[...], w_ref[3 * GW:4 * GW, :])
    x1 = _layer_norm_rows(ALPHA * x_ref[...] + acc, g_ref[...], b_ref[...])
    x1_ref[...] = x1

    logits = _dot_nt(rwt_ref[...], x1.astype(BF16)) + rb_ref[...]
    row = lax.broadcasted_iota(I32, logits.shape, 0)
    krow = lax.broadcasted_iota(I32, ti_ref.shape, 0)
    g = logits
    ti = jnp.zeros(ti_ref.shape, I32)
    tv = jnp.zeros(tg_ref.shape, F32)
    picks = []
    for kk in range(TOP_K):
        mx = jnp.max(g, axis=0, keepdims=True)
        first = jnp.min(jnp.where(g == mx, row, N_EXPERTS), axis=0, keepdims=True)
        ti = jnp.where(krow == kk, first, ti)
        tv = jnp.where(krow == kk, mx, tv)
        picks.append(row == first)
        g = jnp.where(picks[-1], -jnp.inf, g)
    e = jnp.exp(tv - jnp.max(tv, axis=0, keepdims=True))
    ti_ref[...] = ti
    tg_ref[...] = e / jnp.sum(e, axis=0, keepdims=True)

    sel = jnp.where(picks[0] | picks[1] | picks[2] | picks[3], 1.0, 0.0)
    before = run_ref[...] + _dot(sel.astype(BF16), utri_ref[...])
    rk = jnp.zeros(rk_ref.shape, F32)
    for kk in range(TOP_K):
        rk = jnp.where(krow == kk, jnp.sum(jnp.where(picks[kk], before, 0.0), axis=0, keepdims=True), rk)
    rk_ref[...] = rk.astype(I32)
    run_ref[...] = run_ref[...] + jnp.sum(sel, axis=1, keepdims=True)
    cnt_ref[...] = run_ref[...]


def _outproj(oa, orr, oc, od, x2, w_out, g, b, rwt, rb):
    N = x2.shape[0]
    tm = OUT_TM
    row = lambda w: pl.BlockSpec((tm, w), lambda i: (i, 0))
    picks = pl.BlockSpec((TOP_K, tm), lambda i: (0, i))
    const = lambda a: pl.BlockSpec(a.shape, lambda i: (0,) * a.ndim)
    utri = jnp.asarray(np.triu(np.ones((tm, tm), np.float32), 1), dtype=BF16)
    return pl.pallas_call(
        _outproj_kernel,
        grid=(N // tm,),
        in_specs=[row(GROUP_WIDTH)] * 4 + [row(D_MODEL), const(w_out), const(g), const(b), const(rwt), const(rb),
                                           const(utri)],
        out_specs=[row(D_MODEL), picks, picks, picks, pl.BlockSpec((N_EXPERTS, 1), lambda i: (0, 0))],
        out_shape=[jax.ShapeDtypeStruct((N, D_MODEL), F32), jax.ShapeDtypeStruct((TOP_K, N), I32),
                   jax.ShapeDtypeStruct((TOP_K, N), F32), jax.ShapeDtypeStruct((TOP_K, N), I32),
                   jax.ShapeDtypeStruct((N_EXPERTS, 1), F32)],
        scratch_shapes=[pltpu.VMEM((N_EXPERTS, 1), F32)],
        compiler_params=_cparams(1),
        name="outproj_ln_router",
    )(oa, orr, oc, od, x2, w_out, g, b, rwt, rb, utri)


def _dispatch_kernel(tv_ref, pos_ref, x_ref, xs_hbm, xbuf, sem, zsem, *, n_tiles):
    i = pl.program_id(0)
    n = pl.num_programs(0)
    tm = DSP_TM
    par = i % 2

    @pl.when(i == 0)
    def _():
        xbuf[1] = jnp.zeros((tm, D_MODEL), F32)

        def fill(t, c):
            @pl.when(tv_ref[t] < MOE_TM)
            def _():
                pltpu.make_async_copy(xbuf.at[1], xs_hbm.at[pl.ds(pl.multiple_of(t * MOE_TM, MOE_TM), MOE_TM)],
                                      zsem).start()
            return c

        def drain(t, c):
            @pl.when(tv_ref[t] < MOE_TM)
            def _():
                pltpu.make_async_copy(xbuf.at[1], xs_hbm.at[pl.ds(0, MOE_TM)], zsem).wait()
            return c

        lax.fori_loop(0, n_tiles, fill, 0)
        lax.fori_loop(0, n_tiles, drain, 0)

    def wait_step(p):
        for _ in range(TOP_K):
            pltpu.make_async_copy(xbuf.at[p], xbuf.at[p], sem.at[p]).wait()

    @pl.when(i >= 2)
    def _():
        wait_step(par)

    xbuf[par] = x_ref[...]

    def body(r, c):
        for kk in range(TOP_K):
            pltpu.make_async_copy(xbuf.at[par, pl.ds(r, 1)], xs_hbm.at[pl.ds(pos_ref[r * TOP_K + kk], 1)],
                                  sem.at[par]).start()
        return c

    lax.fori_loop(0, tm, body, 0, unroll=4)

    @pl.when(i == n - 1)
    def _():
        wait_step(1 - par)
        wait_step(par)


def _dispatch(pos, x1, tile_valid):
    N = x1.shape[0]
    tm = DSP_TM
    n_tiles = tile_valid.shape[0]
    assert N // tm >= 2 and tm == MOE_TM
    grid_spec = pltpu.PrefetchScalarGridSpec(
        num_scalar_prefetch=1,
        grid=(N // tm,),
        in_specs=[pl.BlockSpec((tm * TOP_K,), lambda i, tv: (i,), memory_space=pltpu.SMEM),
                  pl.BlockSpec((tm, D_MODEL), lambda i, tv: (i, 0))],
        out_specs=pl.BlockSpec(memory_space=pl.ANY),
        scratch_shapes=[pltpu.VMEM((2, tm, D_MODEL), F32), pltpu.SemaphoreType.DMA((2,)),
                        pltpu.SemaphoreType.DMA(())],
    )
    return pl.pallas_call(
        functools.partial(_dispatch_kernel, n_tiles=n_tiles),
        grid_spec=grid_spec,
        out_shape=jax.ShapeDtypeStruct((n_tiles * MOE_TM, D_MODEL), F32),
        compiler_params=_cparams(1),
        name="moe_dispatch",
    )(tile_valid, pos, x1)


def _moe_kernel(te_ref, nv_ref, x_ref, w1_ref, b1_ref, w2_ref, b2_ref, y_ref, w1b, w2b):
    i = pl.program_id(0)
    tm = MOE_TM
    n_valid = nv_ref[i]

    @pl.when((i == 0) | (te_ref[i] != te_ref[jnp.maximum(i - 1, 0)]))
    def _():
        step = LANES
        for c in range(D_MODEL // step):
            w1b[c * step:(c + 1) * step, :] = w1_ref[0, 0, c * step:(c + 1) * step, :].astype(BF16)
        for c in range(D_FF // step):
            w2b[c * step:(c + 1) * step, :] = w2_ref[0, 0, c * step:(c + 1) * step, :].astype(BF16)

    @pl.when(n_valid > 0)
    def _():
        h = _dot(x_ref[...].astype(BF16), w1b[...]) + b1_ref[0]
        glu_in = jnp.minimum(h[:, :D_FF], SWIGLU_LIMIT)
        up = jnp.clip(h[:, D_FF:], -SWIGLU_LIMIT, SWIGLU_LIMIT)
        glu = glu_in * jax.nn.sigmoid(SWIGLU_ALPHA * glu_in)
        y_ref[...] = _dot(((up + 1.0) * glu).astype(BF16), w2b[...]) + b2_ref[0]

    @pl.when(n_valid == 0)
    def _():
        y_ref[...] = jnp.zeros_like(y_ref)


def _moe_experts(xs, tile_expert, tile_valid, w1, b1, w2, b2, layer):
    tm = MOE_TM
    n_tiles = tile_expert.shape[0]
    grid_spec = pltpu.PrefetchScalarGridSpec(
        num_scalar_prefetch=2,
        grid=(n_tiles,),
        in_specs=[
            pl.BlockSpec((tm, D_MODEL), lambda i, te, nv: (i, 0)),
            pl.BlockSpec((1, 1, D_MODEL, 2 * D_FF), lambda i, te, nv: (layer, te[i], 0, 0)),
            pl.BlockSpec((1, 1, 2 * D_FF), lambda i, te, nv: (te[i], 0, 0)),
            pl.BlockSpec((1, 1, D_FF, D_MODEL), lambda i, te, nv: (layer, te[i], 0, 0)),
            pl.BlockSpec((1, 1, D_MODEL), lambda i, te, nv: (te[i], 0, 0)),
        ],
        out_specs=pl.BlockSpec((tm, D_MODEL), lambda i, te, nv: (i, 0)),
        scratch_shapes=[pltpu.VMEM((D_MODEL, 2 * D_FF), BF16), pltpu.VMEM((D_FF, D_MODEL), BF16)],
    )
    return pl.pallas_call(
        _moe_kernel,
        grid_spec=grid_spec,
        out_shape=jax.ShapeDtypeStruct((n_tiles * tm, D_MODEL), F32),
        compiler_params=_cparams(1),
        name="moe_experts",
    )(tile_expert, tile_valid, xs, w1, b1, w2, b2)


def _combine_kernel(pos_ref, posn_ref, y_hbm, x1_ref, tg_ref, g_ref, b_ref, o_ref, ybuf, sem):
    i = pl.program_id(0)
    n = pl.num_programs(0)
    tm = CMB_TM
    slot = i % 2

    def issue(idx_ref, s):
        def body(r, c):
            for kk in range(TOP_K):
                pltpu.make_async_copy(y_hbm.at[pl.ds(idx_ref[r * TOP_K + kk], 1)],
                                      ybuf.at[s, kk, pl.ds(r, 1)], sem.at[s]).start()
            return c
        lax.fori_loop(0, tm, body, 0, unroll=4)

    @pl.when(i == 0)
    def _():
        issue(pos_ref, 0)

    @pl.when(i + 1 < n)
    def _():
        issue(posn_ref, 1 - slot)

    pltpu.make_async_copy(ybuf.at[slot], ybuf.at[slot], sem.at[slot]).wait()
    tg = tg_ref[...]
    moe = tg[:, 0:1] * ybuf[slot, 0]
    for kk in range(1, TOP_K):
        moe = moe + tg[:, kk:kk + 1] * ybuf[slot, kk]
    o_ref[...] = _layer_norm_rows(ALPHA * x1_ref[...] + moe, g_ref[...], b_ref[...])


def _combine(pos, y_sorted, x1, tg, g, b):
    N = x1.shape[0]
    tm = CMB_TM
    n = N // tm
    row = lambda w: pl.BlockSpec((tm, w), lambda i: (i, 0))
    const = lambda a: pl.BlockSpec(a.shape, lambda i: (0,) * a.ndim)
    return pl.pallas_call(
        _combine_kernel,
        grid=(n,),
        in_specs=[pl.BlockSpec((tm * TOP_K,), lambda i: (i,), memory_space=pltpu.SMEM),
                  pl.BlockSpec((tm * TOP_K,), lambda i: (jnp.minimum(i + 1, n - 1),), memory_space=pltpu.SMEM),
                  pl.BlockSpec(memory_space=pl.ANY), row(D_MODEL), row(TOP_K), const(g), const(b)],
        out_specs=row(D_MODEL),
        out_shape=jax.ShapeDtypeStruct((N, D_MODEL), F32),
        scratch_shapes=[pltpu.VMEM((2, TOP_K, tm, D_MODEL), F32), pltpu.SemaphoreType.DMA((2,))],
        compiler_params=_cparams(1),
        name="moe_combine_ln",
    )(pos, pos, y_sorted, x1, tg, g, b)


def _routing_tables(top_i, rank, counts_f, n_tiles):
    tm = MOE_TM
    counts = counts_f.reshape(-1).astype(I32)
    padded = ((counts + tm - 1) // tm) * tm
    ends = jnp.cumsum(padded)
    offsets = ends - padded
    onehot = top_i[:, :, None] == jnp.arange(N_EXPERTS, dtype=I32)[None, None, :]
    pos = jnp.sum(jnp.where(onehot, offsets[None, None, :], 0), axis=-1) + rank
    tile_start = jnp.arange(n_tiles, dtype=I32) * tm
    tile_expert = jnp.sum((ends[None, :] <= tile_start[:, None]).astype(I32), axis=1)
    tile_expert = jnp.minimum(tile_expert, N_EXPERTS - 1)
    n_used = ends[-1] // tm
    last_expert = tile_expert[jnp.maximum(n_used - 1, 0)]
    tile_expert = jnp.where(tile_start < ends[-1], tile_expert, last_expert)
    valid_end = (offsets + counts)[tile_expert]
    tile_valid = jnp.clip(valid_end - tile_start, 0, tm)
    return pos.reshape(-1).astype(I32), tile_expert.astype(I32), tile_valid.astype(I32)


def _rope_tables(T):
    inv = ROPE_THETA ** (-jnp.arange(0, HEAD_DIM, 2, dtype=F32) / HEAD_DIM)
    ang = jnp.arange(T, dtype=F32)[:, None] * inv[None, :]
    cos, sin = jnp.cos(ang), jnp.sin(ang)
    cos_t = jnp.tile(jnp.concatenate([cos, cos], axis=-1), (1, GROUP_HEADS))
    sin_t = jnp.tile(jnp.concatenate([-sin, sin], axis=-1), (1, GROUP_HEADS))
    return cos_t, sin_t


def _pad_w_in(w_in):
    base = 12 * GROUP_WIDTH + IDX_HEADS * IDX_DIM
    w = jnp.zeros((D_MODEL, IN_PAD), F32)
    w = w.at[:, :base + IDX_DIM].set(w_in[:, :base + IDX_DIM])
    w = w.at[:, base + LANES:base + LANES + IDX_HEADS].set(w_in[:, base + IDX_DIM:])
    return w.astype(BF16)


def _layer(x2, B, T, cos_t, sin_t, tabs, w_in, ret_gn_g, ret_gn_b, conv_w, conv_b, rg_wx, rg_bx, rg_wa,
           rg_ba, rg_lambda, w_out, ln1_g, ln1_b, router_w, router_b, exp_w1, exp_b1, exp_w2, exp_b2,
           ln2_g, ln2_b, layer):
    N = B * T
    r2 = lambda a: a.reshape(1, -1)
    (aq, ak, av, rq, rk, rv, rg, cx, cg, dq, dk, dv, dqi, dki, dwt) = _proj(x2, _pad_w_in(w_in), cos_t, sin_t, T)
    seq = lambda a: a.reshape(B, T, a.shape[-1])
    o_a = _moba(seq(aq), seq(ak), seq(av))
    o_r = _retention(seq(rq), seq(rk), seq(rv), seq(rg), r2(ret_gn_g), r2(ret_gn_b), tabs)
    o_c = _rglru(seq(cx), seq(cg), conv_w, r2(conv_b), _block_diag(rg_wx).astype(BF16), r2(rg_bx),
                 _block_diag(rg_wa).astype(BF16), r2(rg_ba), r2(rg_lambda))
    o_d = _dsa(seq(dq), seq(dk), seq(dv), seq(dqi), seq(dki), dwt)
    flat = lambda a: a.reshape(N, GROUP_WIDTH)
    x1, top_i, top_g, rank, counts = _outproj(flat(o_a), flat(o_r), flat(o_c), flat(o_d), x2,
                                              w_out.astype(BF16), r2(ln1_g), r2(ln1_b),
                                              router_w.T.astype(BF16), router_b.reshape(-1, 1))
    top_i, top_g, rank = top_i.T, top_g.T, rank.T
    n_tiles = (N * TOP_K) // MOE_TM + N_EXPERTS
    pos, tile_expert, tile_valid = _routing_tables(top_i, rank, counts, n_tiles)
    xs = _dispatch(pos, x1, tile_valid)
    y_sorted = _moe_experts(xs, tile_expert, tile_valid, exp_w1, exp_b1.reshape(N_EXPERTS, 1, -1), exp_w2,
                            exp_b2.reshape(N_EXPERTS, 1, -1), layer)
    return _combine(pos, y_sorted, x1, top_g, r2(ln2_g), r2(ln2_b))


def kernel(x, w_in, ret_gn_g, ret_gn_b, conv_w, conv_b, rg_wx, rg_bx, rg_wa, rg_ba, rg_lambda, w_out,
           ln1_g, ln1_b, router_w, router_b, exp_w1, exp_b1, exp_w2, exp_b2, ln2_g, ln2_b):
    B, T, D = x.shape
    cos_t, sin_t = _rope_tables(T)
    tabs = _ret_tables()
    x2 = x.reshape(B * T, D)
    for l in range(w_in.shape[0]):
        x2 = _layer(x2, B, T, cos_t, sin_t, tabs, w_in[l], ret_gn_g[l], ret_gn_b[l], conv_w[l], conv_b[l],
                    rg_wx[l], rg_bx[l], rg_wa[l], rg_ba[l], rg_lambda[l], w_out[l], ln1_g[l], ln1_b[l],
                    router_w[l], router_b[l], exp_w1, exp_b1[l], exp_w2, exp_b2[l], ln2_g[l], ln2_b[l], l)
    return x2.reshape(B, T, D)
```

```python
import functools

import numpy as np
import jax
import jax.numpy as jnp
from jax import lax
from jax.experimental import pallas as pl
from jax.experimental.pallas import tpu as pltpu

F32 = jnp.float32
BF16 = jnp.bfloat16
I32 = jnp.int32

D_MODEL = 1024
DEPTH = 2
HEAD_DIM = 64
GROUP_WIDTH = 256
GROUP_HEADS = 4
ROPE_THETA = 10000.0
Q_BLOCK = 128
MOBA_BLOCK = 256
MOBA_TOPK = 3
MOBA_MAX_BLOCKS = 16
RET_CHUNK = 128
RG_CONV = 4
RG_C = 8.0
IDX_HEADS = 8
IDX_DIM = 64
IDX_SCALE = (IDX_HEADS ** -0.5) * (IDX_DIM ** -0.5)
DSA_TOPK = 256
DSA_Q = 256
DSA_KEY_BLOCK = 512
DSA_ATT_BLOCK = 512
N_EXPERTS = 32
TOP_K = 4
D_FF = 1024
SWIGLU_LIMIT = 7.0
SWIGLU_ALPHA = 1.702
ALPHA = (2 * DEPTH) ** 0.25
LN_EPS = 1e-5
IN_WIDTH = 12 * GROUP_WIDTH + IDX_HEADS * IDX_DIM + IDX_DIM + IDX_HEADS
IN_PAD = 15 * GROUP_WIDTH

LANES = 128
SUBLANES = 8
HEAD_SHIFT = HEAD_DIM.bit_length() - 1
KEY_MAGNITUDE_BITS = 0x7FFFFFFF
BISECT_MAX_STEPS = 36
NEG = -3.0e38
M_FLOOR = -1.5e38
BIG = 3.0e38
VMEM_LIMIT = 56 * 1024 * 1024
FOLD_ROWS = 32

PROJ_TM = 256
OUT_TM = 512
RG_TC = 256
MOE_TM = 512
DSP_TM = 512
CMB_TM = 256


def _cparams(ndims):
    return pltpu.CompilerParams(dimension_semantics=("arbitrary",) * ndims,
                                vmem_limit_bytes=VMEM_LIMIT)


def _dot(a, b, precision=None):
    return jnp.dot(a, b, preferred_element_type=F32, precision=precision)


def _dot_nt(a, b, precision=None):
    return lax.dot_general(a, b, (((1,), (1,)), ((), ())), preferred_element_type=F32,
                           precision=precision)


def _dot_tn(a, b):
    return lax.dot_general(a, b, (((0,), (0,)), ((), ())), preferred_element_type=F32)


def _head_stack(q):
    head = lax.shift_right_logical(lax.broadcasted_iota(I32, q.shape, 1), HEAD_SHIFT)
    qf = q.astype(F32)
    return jnp.concatenate([jnp.where(head == h, qf, 0.0) for h in range(GROUP_HEADS)],
                           axis=0).astype(q.dtype)


def _head_unstack(s, rows):
    head = lax.shift_right_logical(lax.broadcasted_iota(I32, (rows, GROUP_WIDTH), 1), HEAD_SHIFT)
    out = jnp.zeros((rows, GROUP_WIDTH), F32)
    for h in range(GROUP_HEADS):
        out = out + jnp.where(head == h, s[h * rows:(h + 1) * rows], 0.0)
    return out


def _fold_rows(x, op):
    return op(x.reshape(x.shape[0] // FOLD_ROWS, FOLD_ROWS, x.shape[1]), axis=0)


def _layer_norm_rows(y, g, b):
    mu = jnp.mean(y, axis=-1, keepdims=True)
    yc = y - mu
    var = jnp.mean(yc * yc, axis=-1, keepdims=True)
    return yc * lax.rsqrt(var + LN_EPS) * g + b


def _proj_kernel(x_ref, w_ref, cos_ref, sin_ref,
                 aq, ak, av, rq, rk, rv, rg, cx, cg, dq, dk, dv, dqi, dki, dwt):
    xb = x_ref[...].astype(BF16)
    cos = cos_ref[...]
    sin = sin_ref[...]
    first_half = (lax.broadcasted_iota(I32, cos.shape, 1) & (HEAD_DIM - 1)) < (HEAD_DIM // 2)

    def seg(i):
        return _dot(xb, w_ref[:, i * GROUP_WIDTH:(i + 1) * GROUP_WIDTH])

    def rope(p):
        rot = jnp.where(first_half, pltpu.roll(p, GROUP_WIDTH - HEAD_DIM // 2, 1),
                        pltpu.roll(p, HEAD_DIM // 2, 1))
        return p * cos + rot * sin

    aq[...] = rope(seg(0)).astype(BF16)
    ak[...] = rope(seg(1)).astype(BF16)
    av[...] = seg(2).astype(BF16)
    rq[...] = rope(seg(3)).astype(BF16)
    rk[...] = (rope(seg(4)) * (HEAD_DIM ** -0.5)).astype(BF16)
    rv[...] = seg(5).astype(BF16)
    rg[...] = seg(6)
    cx[...] = seg(7)
    cg[...] = seg(8)
    dq[...] = rope(seg(9)).astype(BF16)
    dk[...] = rope(seg(10)).astype(BF16)
    dv[...] = seg(11).astype(BF16)
    dqi[:, 0:GROUP_WIDTH] = rope(seg(12)).astype(BF16)
    dqi[:, GROUP_WIDTH:2 * GROUP_WIDTH] = rope(seg(13)).astype(BF16)
    last = seg(14)
    dki[...] = rope(last)[:, 0:IDX_DIM].astype(BF16)
    dwt[...] = last[:, LANES:2 * LANES].T[0:IDX_HEADS, :]


def _proj(x2, w_pad, cos_t, sin_t, T):
    N = x2.shape[0]
    tm = PROJ_TM
    tpb = T // tm
    row = lambda w: pl.BlockSpec((tm, w), lambda i: (i, 0))
    tab = pl.BlockSpec((tm, GROUP_WIDTH), lambda i: (i % tpb, 0))
    widths = [256] * 12 + [512, IDX_DIM]
    dtypes = [BF16, BF16, BF16, BF16, BF16, BF16, F32, F32, F32, BF16, BF16, BF16, BF16, BF16]
    return pl.pallas_call(
        _proj_kernel,
        grid=(N // tm,),
        in_specs=[row(D_MODEL), pl.BlockSpec((D_MODEL, IN_PAD), lambda i: (0, 0)), tab, tab],
        out_specs=[row(w) for w in widths] + [pl.BlockSpec((IDX_HEADS, tm), lambda i: (0, i))],
        out_shape=[jax.ShapeDtypeStruct((N, w), d) for w, d in zip(widths, dtypes)]
        + [jax.ShapeDtypeStruct((IDX_HEADS, N), F32)],
        compiler_params=_cparams(1),
        name="proj_rope",
    )(x2, w_pad, cos_t, sin_t)


def _moba_kernel(q_ref, k_ref, v_ref, o_ref, kmean_ref, sel_ref, qk_ref, m_ref, l_ref, acc_ref, *, n_blocks):
    j = pl.program_id(1)
    R = Q_BLOCK
    SR = GROUP_HEADS * R
    KB = MOBA_BLOCK

    @pl.when(j == 0)
    def _():
        kmean_ref[...] = jnp.zeros_like(kmean_ref)
        for n in range(n_blocks):
            kb = k_ref[0, n * KB:(n + 1) * KB, :].astype(F32)
            kmean_ref[n:n + 1, :] = jnp.mean(kb, axis=0, keepdims=True)

    own = j // (KB // R)
    q_raw = _head_stack(q_ref[0])
    q_stack = (q_raw.astype(F32) * (HEAD_DIM ** -0.5)).astype(BF16)

    gate = _dot_nt(kmean_ref[...], q_raw.astype(F32), precision=lax.Precision.HIGHEST)
    blk = lax.broadcasted_iota(I32, gate.shape, 0)
    past = blk < own
    g = jnp.where(past, gate, -jnp.inf)
    sel = jnp.zeros(gate.shape, F32)
    for _ in range(MOBA_TOPK):
        mx = jnp.max(g, axis=0, keepdims=True)
        first = jnp.min(jnp.where(g == mx, blk, MOBA_MAX_BLOCKS), axis=0, keepdims=True)
        pick = blk == first
        sel = jnp.where(pick & past, 1.0, sel)
        g = jnp.where(pick, -jnp.inf, g)
    sel_ref[...] = sel

    m_ref[...] = jnp.full(m_ref.shape, M_FLOOR, F32)
    l_ref[...] = jnp.zeros(l_ref.shape, F32)
    acc_ref[...] = jnp.zeros(acc_ref.shape, F32)

    def softmax_pv(s_raw, vb, bias):
        s = s_raw + bias
        m_old = m_ref[...]
        m_new = jnp.maximum(m_old, jnp.max(s, axis=0, keepdims=True))
        alpha = jnp.exp(m_old - m_new)
        p = jnp.exp(s - m_new)
        l_ref[...] = alpha * l_ref[...] + jnp.sum(p, axis=0, keepdims=True)
        acc_ref[...] = alpha * acc_ref[...] + _dot_tn(vb, p.astype(BF16))
        m_ref[...] = m_new

    def block_bias(n):
        return jnp.broadcast_to(jnp.where(sel_ref[pl.ds(n, 1), :] > 0.5, 0.0, NEG), (KB, SR))

    def qk(first_block, rows):
        st = pl.multiple_of(first_block * KB, KB)
        return _dot_nt(k_ref[0, pl.ds(st, rows), :], q_stack)

    def attend_span(first_block, bias, s_raw=None):
        st = pl.multiple_of(first_block * KB, KB)
        rows = bias.shape[0]
        s_raw = qk(first_block, rows) if s_raw is None else s_raw
        softmax_pv(s_raw, v_ref[0, pl.ds(st, rows), :], bias)

    keypos = own * KB + lax.broadcasted_iota(I32, (KB, SR), 0)
    qpos = j * R + (lax.broadcasted_iota(I32, (KB, SR), 1) & (R - 1))
    causal = jnp.where(keypos <= qpos, 0.0, NEG)
    odd = (own & 1) == 1

    @pl.when(odd)
    def _():
        attend_span(own - 1, jnp.concatenate([block_bias(own - 1), causal], axis=0))

    @pl.when(jnp.logical_not(odd))
    def _():
        attend_span(own, causal)

    n_pairs = own // 2

    @pl.when(n_pairs > 0)
    def _():
        qk_ref[...] = qk(0, 2 * KB)

    def body(pair, c):
        s_raw = qk_ref[...]
        qk_ref[...] = qk(2 * jnp.minimum(pair + 1, n_pairs - 1), 2 * KB)
        attend_span(2 * pair, jnp.concatenate([block_bias(2 * pair), block_bias(2 * pair + 1)], axis=0), s_raw)
        return c

    lax.fori_loop(0, n_pairs, body, 0)
    o_ref[0] = _head_unstack((acc_ref[...] / l_ref[...]).T, R).astype(o_ref.dtype)


def _moba(q, k, v):
    B, T, _ = q.shape
    n_blocks = T // MOBA_BLOCK
    assert T % MOBA_BLOCK == 0 and n_blocks <= MOBA_MAX_BLOCKS
    SR = GROUP_HEADS * Q_BLOCK
    return pl.pallas_call(
        functools.partial(_moba_kernel, n_blocks=n_blocks),
        grid=(B, T // Q_BLOCK),
        in_specs=[pl.BlockSpec((1, Q_BLOCK, GROUP_WIDTH), lambda b, j: (b, j, 0)),
                  pl.BlockSpec((1, T, GROUP_WIDTH), lambda b, j: (b, 0, 0)),
                  pl.BlockSpec((1, T, GROUP_WIDTH), lambda b, j: (b, 0, 0))],
        out_specs=pl.BlockSpec((1, Q_BLOCK, GROUP_WIDTH), lambda b, j: (b, j, 0)),
        out_shape=jax.ShapeDtypeStruct((B, T, GROUP_WIDTH), BF16),
        scratch_shapes=[pltpu.VMEM((MOBA_MAX_BLOCKS, GROUP_WIDTH), F32),
                        pltpu.VMEM((MOBA_MAX_BLOCKS, SR), F32),
                        pltpu.VMEM((2 * MOBA_BLOCK, SR), F32),
                        pltpu.VMEM((1, SR), F32), pltpu.VMEM((1, SR), F32),
                        pltpu.VMEM((GROUP_WIDTH, SR), F32)],
        compiler_params=_cparams(2),
        name="moba_attention",
    )(q, k, v)


def _ret_kernel(q_ref, k_ref, v_ref, g_ref, dmask_ref, xi_ref, zeta_ref, gdec_ref, bd_ref, avg_ref,
                gng_ref, gnb_ref, o_ref, r_ref):
    j = pl.program_id(0)

    @pl.when(j == 0)
    def _():
        r_ref[...] = jnp.zeros_like(r_ref)

    C = RET_CHUNK
    hp = lax.Precision.HIGHEST
    for b in range(q_ref.shape[0]):
        q = q_ref[b]
        k = k_ref[b]
        v = v_ref[b]
        q_stack = _head_stack(q)
        inner = _dot_nt(q_stack, k) * dmask_ref[...]
        o = _head_unstack(_dot(inner.astype(BF16), v), C)
        R = r_ref[b]
        o = o + _dot(q, R.astype(BF16)) * xi_ref[...]
        kz = (k.astype(F32) * zeta_ref[...]).astype(BF16)
        r_ref[b] = gdec_ref[...] * R + bd_ref[...] * _dot_tn(kz, v)

        mu = _dot(o, avg_ref[...], precision=hp)
        oc = o - mu
        var = _dot(oc * oc, avg_ref[...], precision=hp)
        y = oc * lax.rsqrt(var + LN_EPS) * gng_ref[...] + gnb_ref[...]
        gte = g_ref[b]
        o_ref[b] = (y * (gte * jax.nn.sigmoid(gte))).astype(o_ref.dtype)


def _ret_tables():
    H, C, d = GROUP_HEADS, RET_CHUNK, HEAD_DIM
    log_g = np.log(1.0 - 2.0 ** (-5.0 - np.arange(H, dtype=np.float64)))
    n = np.arange(C, dtype=np.float64)
    diff = n[:, None] - n[None, :]
    dmask = np.where(diff >= 0, np.exp(log_g[:, None, None] * np.maximum(diff, 0.0)), 0.0)
    xi = np.exp(log_g[:, None] * (n + 1.0))
    zeta = np.exp(log_g[:, None] * (C - 1.0 - n))
    g_chunk = np.exp(log_g * C)
    head = np.arange(GROUP_WIDTH) // d
    bd = (head[:, None] == head[None, :]).astype(np.float64)
    to32 = lambda a: jnp.asarray(a, dtype=F32)
    return dict(dmask=to32(dmask.reshape(H * C, C)), xi=to32(xi.T[:, head]), zeta=to32(zeta.T[:, head]),
                gdec=to32(bd * g_chunk[head][:, None]), bd=to32(bd), avg=to32(bd / d))


def _retention(rq, rk, rv, rg, gn_g, gn_b, tabs):
    B, T, _ = rq.shape
    C = RET_CHUNK
    blk = pl.BlockSpec((B, C, GROUP_WIDTH), lambda j: (0, j, 0))
    const = lambda a: pl.BlockSpec(a.shape, lambda j: (0,) * a.ndim)
    consts = [tabs["dmask"], tabs["xi"], tabs["zeta"], tabs["gdec"], tabs["bd"], tabs["avg"], gn_g, gn_b]
    return pl.pallas_call(
        _ret_kernel,
        grid=(T // C,),
        in_specs=[blk, blk, blk, blk] + [const(a) for a in consts],
        out_specs=blk,
        out_shape=jax.ShapeDtypeStruct((B, T, GROUP_WIDTH), BF16),
        scratch_shapes=[pltpu.VMEM((B, GROUP_WIDTH, GROUP_WIDTH), F32)],
        compiler_params=_cparams(1),
        name="retention",
    )(rq, rk, rv, rg, *consts)


def _rglru_kernel(x_ref, g_ref, cw_ref, cb_ref, wx_ref, bx_ref, wa_ref, ba_ref, lam_ref, o_ref,
                  xbuf, h_ref):
    j = pl.program_id(1)
    tc = RG_TC

    @pl.when(j == 0)
    def _():
        xbuf[0:SUBLANES, :] = jnp.zeros((SUBLANES, GROUP_WIDTH), F32)
        h_ref[...] = jnp.zeros_like(h_ref)

    xbuf[SUBLANES:SUBLANES + tc, :] = x_ref[0]
    xc = cb_ref[...] + cw_ref[RG_CONV - 1:RG_CONV, :] * xbuf[SUBLANES:SUBLANES + tc, :]
    for i in range(RG_CONV - 1):
        off = SUBLANES - (RG_CONV - 1) + i
        xc = xc + cw_ref[i:i + 1, :] * xbuf[off:off + tc, :]
    xbuf[0:SUBLANES, :] = xbuf[tc:tc + SUBLANES, :]

    xcb = xc.astype(BF16)
    gate_x = jax.nn.sigmoid(_dot(xcb, wx_ref[...]) + bx_ref[...])
    gate_a = jax.nn.sigmoid(_dot(xcb, wa_ref[...]) + ba_ref[...])
    lam = lam_ref[...]
    softplus_neg = jnp.maximum(-lam, 0.0) + jnp.log1p(jnp.exp(-jnp.abs(lam)))
    log_a = -RG_C * gate_a * softplus_neg
    a = jnp.exp(log_a)
    th = jnp.tanh(log_a)
    b = jnp.sqrt(-2.0 * th / (1.0 - th)) * (gate_x * xc)

    row = lax.broadcasted_iota(I32, (tc, GROUP_WIDTH), 0)
    d = 1
    while d < tc:
        keep = row >= d
        a_sh = jnp.where(keep, pltpu.roll(a, d, 0), 1.0)
        b_sh = jnp.where(keep, pltpu.roll(b, d, 0), 0.0)
        b = a * b_sh + b
        a = a * a_sh
        d *= 2
    h = b + a * h_ref[...]
    h_ref[...] = h[tc - 1:tc, :]

    xg = g_ref[0]
    gelu = 0.5 * xg * (1.0 + jnp.tanh(np.sqrt(2.0 / np.pi) * (xg + 0.044715 * xg * xg * xg)))
    o_ref[0] = (h * gelu).astype(o_ref.dtype)


def _block_diag(w):
    n, c, _ = w.shape
    eye = jnp.eye(n, dtype=w.dtype)
    return (eye[:, None, :, None] * w[:, :, None, :]).reshape(n * c, n * c)


def _rglru(cx, cg, conv_w, conv_b, wx, bx, wa, ba, lam):
    B, T, _ = cx.shape
    tc = RG_TC
    blk = pl.BlockSpec((1, tc, GROUP_WIDTH), lambda b, j: (b, j, 0))
    const = lambda a: pl.BlockSpec(a.shape, lambda b, j: (0,) * a.ndim)
    consts = [conv_w, conv_b, wx, bx, wa, ba, lam]
    return pl.pallas_call(
        _rglru_kernel,
        grid=(B, T // tc),
        in_specs=[blk, blk] + [const(a) for a in consts],
        out_specs=blk,
        out_shape=jax.ShapeDtypeStruct((B, T, GROUP_WIDTH), BF16),
        scratch_shapes=[pltpu.VMEM((tc + SUBLANES, GROUP_WIDTH), F32), pltpu.VMEM((1, GROUP_WIDTH), F32)],
        compiler_params=_cparams(2),
        name="rg_lru",
    )(cx, cg, *consts)


def _dsa_kernel(q_ref, k_ref, v_ref, qi_ref, ki_ref, wt_ref, o_ref,
                sc_ref, jcut_ref, qk_ref, m_ref, l_ref, acc_ref, *, n_sel, n_keys):
    j = pl.program_id(1)
    R = DSA_Q
    KB = DSA_KEY_BLOCK
    SR = GROUP_HEADS * R
    nkb = (j * R + R + KB - 1) // KB
    nsel = float(n_sel)

    rowk = lax.broadcasted_iota(I32, (KB, R), 0)
    qpos = j * R + lax.broadcasted_iota(I32, (KB, R), 1)

    qi = qi_ref[0]
    qi_stack = jnp.concatenate([qi[:, h * IDX_DIM:(h + 1) * IDX_DIM] for h in range(IDX_HEADS)], axis=0)
    wt = wt_ref[...]

    def score_body(kb, c):
        st = pl.multiple_of(kb * KB, KB)
        rel = _dot_nt(ki_ref[0, pl.ds(st, KB), :], qi_stack)
        sc = wt[0:1, :] * jnp.maximum(rel[:, 0:R], 0.0)
        for h in range(1, IDX_HEADS):
            sc = sc + wt[h:h + 1, :] * jnp.maximum(rel[:, h * R:(h + 1) * R], 0.0)
        sc_ref[pl.ds(st, KB), :] = jnp.where(st + rowk <= qpos, sc * IDX_SCALE, NEG)
        return c

    lax.fori_loop(0, nkb, score_body, 0)

    def blocks(fn, init):
        def body(kb, c):
            st = pl.multiple_of(kb * KB, KB)
            return fn(st, sc_ref[pl.ds(st, KB), :], c)
        return lax.fori_loop(0, nkb, body, init)

    def survey(st, s, c):
        mn, mx, gt, ge = c
        return (jnp.minimum(mn, _fold_rows(jnp.where(s > 0.5 * NEG, s, BIG), jnp.min)),
                jnp.maximum(mx, _fold_rows(s, jnp.max)),
                gt + _fold_rows(jnp.where(s > 0.0, 1.0, 0.0), jnp.sum),
                ge + _fold_rows(jnp.where(s >= 0.0, 1.0, 0.0), jnp.sum))

    zeros_fold = jnp.zeros((FOLD_ROWS, R), F32)
    mn8, mx8, gt8, ge8 = blocks(survey, (jnp.full((FOLD_ROWS, R), BIG, F32), jnp.full((FOLD_ROWS, R), NEG, F32),
                                         zeros_fold, zeros_fold))
    mn = jnp.min(mn8, axis=0, keepdims=True)
    mx = jnp.max(mx8, axis=0, keepdims=True)
    c_gt0 = jnp.sum(gt8, axis=0, keepdims=True)
    c_ge0 = jnp.sum(ge8, axis=0, keepdims=True)

    def count_ge(th):
        acc = blocks(lambda st, s, c: c + _fold_rows(jnp.where(s >= th, 1.0, 0.0), jnp.sum),
                     jnp.zeros((FOLD_ROWS, R), F32))
        return jnp.sum(acc, axis=0, keepdims=True)

    n_adm = (j * R + 1 + lax.broadcasted_iota(I32, (1, R), 1)).astype(F32)
    need = n_adm > nsel
    lo0 = jnp.where(need, mn, 0.5 * NEG)
    above = need & (c_gt0 >= nsel)
    below = need & (c_ge0 < nsel)
    at_zero = need & jnp.logical_not(above | below)
    cgt0 = jnp.where(below, c_ge0, jnp.where(at_zero, c_gt0, 0.0))
    act0 = jnp.where(above | below, 1.0, 0.0)
    tie0 = jnp.where(at_zero, 1.0, 0.0)

    def to_key(f):
        b = lax.bitcast_convert_type(f, I32)
        return b ^ (lax.shift_right_arithmetic(b, 31) & KEY_MAGNITUDE_BITS)

    def from_key(kk):
        return lax.bitcast_convert_type(kk ^ (lax.shift_right_arithmetic(kk, 31) & KEY_MAGNITUDE_BITS), F32)

    def bis_step(klo, khi, cgt, act, tie):
        on = act > 0.0
        kmid = lax.shift_right_arithmetic(klo, 1) + lax.shift_right_arithmetic(khi, 1) + (klo & khi & 1)
        stuck = kmid == klo
        cnt = count_ge(from_key(kmid))
        go = on & jnp.logical_not(stuck)
        up = go & (cnt >= nsel)
        dn = go & (cnt < nsel)
        return (jnp.where(up, kmid, klo), jnp.where(dn, kmid, khi), jnp.where(dn, cnt, cgt),
                jnp.where(go & (cnt != nsel), 1.0, 0.0), jnp.where(on & stuck, 1.0, tie))

    def bis_cond(c):
        return (c[1] > 0.0) & (c[0] < BISECT_MAX_STEPS)

    def bis_body(c):
        flag = jnp.max(c[5])
        st = bis_step(*bis_step(*c[2:]))
        return (c[0] + 2, flag) + st

    klo0 = jnp.where(above, 1, jnp.where(at_zero, 0, to_key(lo0)))
    khi0 = jnp.where(below, 0, to_key(mx) + 1)
    res = lax.while_loop(bis_cond, bis_body, (jnp.int32(0), jnp.max(act0), klo0, khi0, cgt0, act0, tie0))
    lo, cgt, tie = from_key(res[2]), res[4], res[6]

    jcut_ref[...] = jnp.full((1, R), float(n_keys), F32)

    @pl.when(jnp.max(tie) > 0.0)
    def _():
        want = nsel - cgt
        tied = tie > 0.0

        def jb(it, c):
            a, b = c
            mid = jnp.floor((a + b) * 0.5)
            hit8 = blocks(lambda st, s, cc: cc + _fold_rows(
                jnp.where((s == lo) & ((st + rowk).astype(F32) <= mid), 1.0, 0.0), jnp.sum),
                jnp.zeros((FOLD_ROWS, R), F32))
            ok = jnp.sum(hit8, axis=0, keepdims=True) >= want
            return jnp.where(ok, a, mid), jnp.where(ok, mid, b)

        _, b = lax.fori_loop(0, int(np.ceil(np.log2(n_keys))) + 1, jb,
                             (jnp.full((1, R), -1.0, F32), jnp.full((1, R), float(n_keys - 1), F32)))
        jcut_ref[...] = jnp.where(tied, b, float(n_keys))

    q_stack = _head_stack((q_ref[0].astype(F32) * (HEAD_DIM ** -0.5)).astype(BF16))
    m_ref[...] = jnp.full(m_ref.shape, M_FLOOR, F32)
    l_ref[...] = jnp.zeros(l_ref.shape, F32)
    acc_ref[...] = jnp.zeros(acc_ref.shape, F32)
    jcut = jcut_ref[...]
    AB = DSA_ATT_BLOCK
    nab = (j * R + R + AB - 1) // AB
    rowf = lax.broadcasted_iota(I32, (AB, R), 0).astype(F32)

    def qk(kb):
        st = pl.multiple_of(kb * AB, AB)
        return _dot_nt(k_ref[0, pl.ds(st, AB), :], q_stack)

    qk_ref[...] = qk(0)

    def att_body(kb, c):
        st = pl.multiple_of(kb * AB, AB)
        s_raw = qk_ref[...]
        qk_ref[...] = qk(jnp.minimum(kb + 1, nab - 1))
        sc = sc_ref[pl.ds(st, AB), :]
        keep = (sc > lo) | ((sc == lo) & (rowf <= jcut - st.astype(F32)))
        bias = jnp.where(keep, 0.0, NEG)
        s = s_raw + jnp.concatenate([bias] * GROUP_HEADS, axis=1)
        m_old = m_ref[...]
        m_new = jnp.maximum(m_old, jnp.max(s, axis=0, keepdims=True))
        alpha = jnp.exp(m_old - m_new)
        p = jnp.exp(s - m_new)
        l_ref[...] = alpha * l_ref[...] + jnp.sum(p, axis=0, keepdims=True)
        acc_ref[...] = alpha * acc_ref[...] + _dot_tn(v_ref[0, pl.ds(st, AB), :], p.astype(BF16))
        m_ref[...] = m_new
        return c

    lax.fori_loop(0, nab, att_body, 0)
    o_ref[0] = _head_unstack((acc_ref[...] / l_ref[...]).T, R).astype(o_ref.dtype)


def _dsa(q, k, v, qi, ki, wt):
    B, T, _ = q.shape
    KB = DSA_KEY_BLOCK
    assert T % KB == 0
    n_sel = min(DSA_TOPK, T // 4)
    R = DSA_Q
    assert T % R == 0
    SR = GROUP_HEADS * R
    nq = T // R
    qblk = lambda wd: pl.BlockSpec((1, R, wd), lambda b, j: (b, j, 0))
    full = lambda wd: pl.BlockSpec((1, T, wd), lambda b, j: (b, 0, 0))
    return pl.pallas_call(
        functools.partial(_dsa_kernel, n_sel=n_sel, n_keys=T),
        grid=(B, nq),
        in_specs=[qblk(GROUP_WIDTH), full(GROUP_WIDTH), full(GROUP_WIDTH),
                  qblk(IDX_HEADS * IDX_DIM), full(IDX_DIM),
                  pl.BlockSpec((IDX_HEADS, R), lambda b, j: (0, b * nq + j))],
        out_specs=qblk(GROUP_WIDTH),
        out_shape=jax.ShapeDtypeStruct((B, T, GROUP_WIDTH), BF16),
        scratch_shapes=[pltpu.VMEM((T, R), F32), pltpu.VMEM((1, R), F32), pltpu.VMEM((DSA_ATT_BLOCK, SR), F32),
                        pltpu.VMEM((1, SR), F32), pltpu.VMEM((1, SR), F32),
                        pltpu.VMEM((GROUP_WIDTH, SR), F32)],
        compiler_params=_cparams(2),
        name="dsa_attention",
    )(q, k, v, qi, ki, wt)


def _outproj_kernel(oa, orr, oc, od, x_ref, w_ref, g_ref, b_ref, rwt_ref, rb_ref, utri_ref,
                    x1_ref, ti_ref, tg_ref, rk_ref, cnt_ref, run_ref):
    GW = GROUP_WIDTH

    @pl.when(pl.program_id(0) == 0)
    def _():
        run_ref[...] = jnp.zeros_like(run_ref)

    acc = _dot(oa[...], w_ref[0:GW, :])
    acc = acc + _dot(orr[...], w_ref[GW:2 * GW, :])
    acc = acc + _dot(oc[...], w_ref[2 * GW:3 * GW, :])
    acc = acc + _dot(od[...], w_ref[3 * GW:4 * GW, :])
    x1 = _layer_norm_rows(ALPHA * x_ref[...] + acc, g_ref[...], b_ref[...])
    x1_ref[...] = x1

    logits = _dot_nt(rwt_ref[...], x1.astype(BF16)) + rb_ref[...]
    row = lax.broadcasted_iota(I32, logits.shape, 0)
    krow = lax.broadcasted_iota(I32, ti_ref.shape, 0)
    g = logits
    ti = jnp.zeros(ti_ref.shape, I32)
    tv = jnp.zeros(tg_ref.shape, F32)
    picks = []
    for kk in range(TOP_K):
        mx = jnp.max(g, axis=0, keepdims=True)
        first = jnp.min(jnp.where(g == mx, row, N_EXPERTS), axis=0, keepdims=True)
        ti = jnp.where(krow == kk, first, ti)
        tv = jnp.where(krow == kk, mx, tv)
        picks.append(row == first)
        g = jnp.where(picks[-1], -jnp.inf, g)
    e = jnp.exp(tv - jnp.max(tv, axis=0, keepdims=True))
    ti_ref[...] = ti
    tg_ref[...] = e / jnp.sum(e, axis=0, keepdims=True)

    sel = jnp.where(picks[0] | picks[1] | picks[2] | picks[3], 1.0, 0.0)
    before = run_ref[...] + _dot(sel.astype(BF16), utri_ref[...])
    rk = jnp.zeros(rk_ref.shape, F32)
    for kk in range(TOP_K):
        rk = jnp.where(krow == kk, jnp.sum(jnp.where(picks[kk], before, 0.0), axis=0, keepdims=True), rk)
    rk_ref[...] = rk.astype(I32)
    run_ref[...] = run_ref[...] + jnp.sum(sel, axis=1, keepdims=True)
    cnt_ref[...] = run_ref[...]


def _outproj(oa, orr, oc, od, x2, w_out, g, b, rwt, rb):
    N = x2.shape[0]
    tm = OUT_TM
    row = lambda w: pl.BlockSpec((tm, w), lambda i: (i, 0))
    picks = pl.BlockSpec((TOP_K, tm), lambda i: (0, i))
    const = lambda a: pl.BlockSpec(a.shape, lambda i: (0,) * a.ndim)
    utri = jnp.asarray(np.triu(np.ones((tm, tm), np.float32), 1), dtype=BF16)
    return pl.pallas_call(
        _outproj_kernel,
        grid=(N // tm,),
        in_specs=[row(GROUP_WIDTH)] * 4 + [row(D_MODEL), const(w_out), const(g), const(b), const(rwt), const(rb),
                                           const(utri)],
        out_specs=[row(D_MODEL), picks, picks, picks, pl.BlockSpec((N_EXPERTS, 1), lambda i: (0, 0))],
        out_shape=[jax.ShapeDtypeStruct((N, D_MODEL), F32), jax.ShapeDtypeStruct((TOP_K, N), I32),
                   jax.ShapeDtypeStruct((TOP_K, N), F32), jax.ShapeDtypeStruct((TOP_K, N), I32),
                   jax.ShapeDtypeStruct((N_EXPERTS, 1), F32)],
        scratch_shapes=[pltpu.VMEM((N_EXPERTS, 1), F32)],
        compiler_params=_cparams(1),
        name="outproj_ln_router",
    )(oa, orr, oc, od, x2, w_out, g, b, rwt, rb, utri)


def _dispatch_kernel(tv_ref, pos_ref, x_ref, xs_hbm, xbuf, sem, zsem, *, n_tiles):
    i = pl.program_id(0)
    n = pl.num_programs(0)
    tm = DSP_TM
    par = i % 2

    @pl.when(i == 0)
    def _():
        xbuf[1] = jnp.zeros((tm, D_MODEL), F32)

        def fill(t, c):
            @pl.when(tv_ref[t] < MOE_TM)
            def _():
                pltpu.make_async_copy(xbuf.at[1], xs_hbm.at[pl.ds(pl.multiple_of(t * MOE_TM, MOE_TM), MOE_TM)],
                                      zsem).start()
            return c

        def drain(t, c):
            @pl.when(tv_ref[t] < MOE_TM)
            def _():
                pltpu.make_async_copy(xbuf.at[1], xs_hbm.at[pl.ds(0, MOE_TM)], zsem).wait()
            return c

        lax.fori_loop(0, n_tiles, fill, 0)
        lax.fori_loop(0, n_tiles, drain, 0)

    def wait_step(p):
        for _ in range(TOP_K):
            pltpu.make_async_copy(xbuf.at[p], xbuf.at[p], sem.at[p]).wait()

    @pl.when(i >= 2)
    def _():
        wait_step(par)

    xbuf[par] = x_ref[...]

    def body(r, c):
        for kk in range(TOP_K):
            pltpu.make_async_copy(xbuf.at[par, pl.ds(r, 1)], xs_hbm.at[pl.ds(pos_ref[r * TOP_K + kk], 1)],
                                  sem.at[par]).start()
        return c

    lax.fori_loop(0, tm, body, 0, unroll=4)

    @pl.when(i == n - 1)
    def _():
        wait_step(1 - par)
        wait_step(par)


def _dispatch(pos, x1, tile_valid):
    N = x1.shape[0]
    tm = DSP_TM
    n_tiles = tile_valid.shape[0]
    assert N // tm >= 2 and tm == MOE_TM
    grid_spec = pltpu.PrefetchScalarGridSpec(
        num_scalar_prefetch=1,
        grid=(N // tm,),
        in_specs=[pl.BlockSpec((tm * TOP_K,), lambda i, tv: (i,), memory_space=pltpu.SMEM),
                  pl.BlockSpec((tm, D_MODEL), lambda i, tv: (i, 0))],
        out_specs=pl.BlockSpec(memory_space=pl.ANY),
        scratch_shapes=[pltpu.VMEM((2, tm, D_MODEL), F32), pltpu.SemaphoreType.DMA((2,)),
                        pltpu.SemaphoreType.DMA(())],
    )
    return pl.pallas_call(
        functools.partial(_dispatch_kernel, n_tiles=n_tiles),
        grid_spec=grid_spec,
        out_shape=jax.ShapeDtypeStruct((n_tiles * MOE_TM, D_MODEL), F32),
        compiler_params=_cparams(1),
        name="moe_dispatch",
    )(tile_valid, pos, x1)


def _moe_kernel(te_ref, nv_ref, x_ref, w1_ref, b1_ref, w2_ref, b2_ref, y_ref, w1b, w2b):
    i = pl.program_id(0)
    tm = MOE_TM
    n_valid = nv_ref[i]

    @pl.when((i == 0) | (te_ref[i] != te_ref[jnp.maximum(i - 1, 0)]))
    def _():
        step = LANES
        for c in range(D_MODEL // step):
            w1b[c * step:(c + 1) * step, :] = w1_ref[0, 0, c * step:(c + 1) * step, :].astype(BF16)
        for c in range(D_FF // step):
            w2b[c * step:(c + 1) * step, :] = w2_ref[0, 0, c * step:(c + 1) * step, :].astype(BF16)

    @pl.when(n_valid > 0)
    def _():
        h = _dot(x_ref[...].astype(BF16), w1b[...]) + b1_ref[0]
        glu_in = jnp.minimum(h[:, :D_FF], SWIGLU_LIMIT)
        up = jnp.clip(h[:, D_FF:], -SWIGLU_LIMIT, SWIGLU_LIMIT)
        glu = glu_in * jax.nn.sigmoid(SWIGLU_ALPHA * glu_in)
        y_ref[...] = _dot(((up + 1.0) * glu).astype(BF16), w2b[...]) + b2_ref[0]

    @pl.when(n_valid == 0)
    def _():
        y_ref[...] = jnp.zeros_like(y_ref)


def _moe_experts(xs, tile_expert, tile_valid, w1, b1, w2, b2, layer):
    tm = MOE_TM
    n_tiles = tile_expert.shape[0]
    grid_spec = pltpu.PrefetchScalarGridSpec(
        num_scalar_prefetch=2,
        grid=(n_tiles,),
        in_specs=[
            pl.BlockSpec((tm, D_MODEL), lambda i, te, nv: (i, 0)),
            pl.BlockSpec((1, 1, D_MODEL, 2 * D_FF), lambda i, te, nv: (layer, te[i], 0, 0)),
            pl.BlockSpec((1, 1, 2 * D_FF), lambda i, te, nv: (te[i], 0, 0)),
            pl.BlockSpec((1, 1, D_FF, D_MODEL), lambda i, te, nv: (layer, te[i], 0, 0)),
            pl.BlockSpec((1, 1, D_MODEL), lambda i, te, nv: (te[i], 0, 0)),
        ],
        out_specs=pl.BlockSpec((tm, D_MODEL), lambda i, te, nv: (i, 0)),
        scratch_shapes=[pltpu.VMEM((D_MODEL, 2 * D_FF), BF16), pltpu.VMEM((D_FF, D_MODEL), BF16)],
    )
    return pl.pallas_call(
        _moe_kernel,
        grid_spec=grid_spec,
        out_shape=jax.ShapeDtypeStruct((n_tiles * tm, D_MODEL), F32),
        compiler_params=_cparams(1),
        name="moe_experts",
    )(tile_expert, tile_valid, xs, w1, b1, w2, b2)


def _combine_kernel(pos_ref, posn_ref, y_hbm, x1_ref, tg_ref, g_ref, b_ref, o_ref, ybuf, sem):
    i = pl.program_id(0)
    n = pl.num_programs(0)
    tm = CMB_TM
    slot = i % 2

    def issue(idx_ref, s):
        def body(r, c):
            for kk in range(TOP_K):
                pltpu.make_async_copy(y_hbm.at[pl.ds(idx_ref[r * TOP_K + kk], 1)],
                                      ybuf.at[s, kk, pl.ds(r, 1)], sem.at[s]).start()
            return c
        lax.fori_loop(0, tm, body, 0, unroll=4)

    @pl.when(i == 0)
    def _():
        issue(pos_ref, 0)

    @pl.when(i + 1 < n)
    def _():
        issue(posn_ref, 1 - slot)

    pltpu.make_async_copy(ybuf.at[slot], ybuf.at[slot], sem.at[slot]).wait()
    tg = tg_ref[...]
    moe = tg[:, 0:1] * ybuf[slot, 0]
    for kk in range(1, TOP_K):
        moe = moe + tg[:, kk:kk + 1] * ybuf[slot, kk]
    o_ref[...] = _layer_norm_rows(ALPHA * x1_ref[...] + moe, g_ref[...], b_ref[...])


def _combine(pos, y_sorted, x1, tg, g, b):
    N = x1.shape[0]
    tm = CMB_TM
    n = N // tm
    row = lambda w: pl.BlockSpec((tm, w), lambda i: (i, 0))
    const = lambda a: pl.BlockSpec(a.shape, lambda i: (0,) * a.ndim)
    return pl.pallas_call(
        _combine_kernel,
        grid=(n,),
        in_specs=[pl.BlockSpec((tm * TOP_K,), lambda i: (i,), memory_space=pltpu.SMEM),
                  pl.BlockSpec((tm * TOP_K,), lambda i: (jnp.minimum(i + 1, n - 1),), memory_space=pltpu.SMEM),
                  pl.BlockSpec(memory_space=pl.ANY), row(D_MODEL), row(TOP_K), const(g), const(b)],
        out_specs=row(D_MODEL),
        out_shape=jax.ShapeDtypeStruct((N, D_MODEL), F32),
        scratch_shapes=[pltpu.VMEM((2, TOP_K, tm, D_MODEL), F32), pltpu.SemaphoreType.DMA((2,))],
        compiler_params=_cparams(1),
        name="moe_combine_ln",
    )(pos, pos, y_sorted, x1, tg, g, b)


def _routing_tables(top_i, rank, counts_f, n_tiles):
    tm = MOE_TM
    counts = counts_f.reshape(-1).astype(I32)
    padded = ((counts + tm - 1) // tm) * tm
    ends = jnp.cumsum(padded)
    offsets = ends - padded
    onehot = top_i[:, :, None] == jnp.arange(N_EXPERTS, dtype=I32)[None, None, :]
    pos = jnp.sum(jnp.where(onehot, offsets[None, None, :], 0), axis=-1) + rank
    tile_start = jnp.arange(n_tiles, dtype=I32) * tm
    tile_expert = jnp.sum((ends[None, :] <= tile_start[:, None]).astype(I32), axis=1)
    tile_expert = jnp.minimum(tile_expert, N_EXPERTS - 1)
    n_used = ends[-1] // tm
    last_expert = tile_expert[jnp.maximum(n_used - 1, 0)]
    tile_expert = jnp.where(tile_start < ends[-1], tile_expert, last_expert)
    valid_end = (offsets + counts)[tile_expert]
    tile_valid = jnp.clip(valid_end - tile_start, 0, tm)
    return pos.reshape(-1).astype(I32), tile_expert.astype(I32), tile_valid.astype(I32)


def _rope_tables(T):
    inv = ROPE_THETA ** (-jnp.arange(0, HEAD_DIM, 2, dtype=F32) / HEAD_DIM)
    ang = jnp.arange(T, dtype=F32)[:, None] * inv[None, :]
    cos, sin = jnp.cos(ang), jnp.sin(ang)
    cos_t = jnp.tile(jnp.concatenate([cos, cos], axis=-1), (1, GROUP_HEADS))
    sin_t = jnp.tile(jnp.concatenate([-sin, sin], axis=-1), (1, GROUP_HEADS))
    return cos_t, sin_t


def _pad_w_in(w_in):
    base = 12 * GROUP_WIDTH + IDX_HEADS * IDX_DIM
    w = jnp.zeros((D_MODEL, IN_PAD), F32)
    w = w.at[:, :base + IDX_DIM].set(w_in[:, :base + IDX_DIM])
    w = w.at[:, base + LANES:base + LANES + IDX_HEADS].set(w_in[:, base + IDX_DIM:])
    return w.astype(BF16)


def _layer(x2, B, T, cos_t, sin_t, tabs, w_in, ret_gn_g, ret_gn_b, conv_w, conv_b, rg_wx, rg_bx, rg_wa,
           rg_ba, rg_lambda, w_out, ln1_g, ln1_b, router_w, router_b, exp_w1, exp_b1, exp_w2, exp_b2,
           ln2_g, ln2_b, layer):
    N = B * T
    r2 = lambda a: a.reshape(1, -1)
    (aq, ak, av, rq, rk, rv, rg, cx, cg, dq, dk, dv, dqi, dki, dwt) = _proj(x2, _pad_w_in(w_in), cos_t, sin_t, T)
    seq = lambda a: a.reshape(B, T, a.shape[-1])
    o_a = _moba(seq(aq), seq(ak), seq(av))
    o_r = _retention(seq(rq), seq(rk), seq(rv), seq(rg), r2(ret_gn_g), r2(ret_gn_b), tabs)
    o_c = _rglru(seq(cx), seq(cg), conv_w, r2(conv_b), _block_diag(rg_wx).astype(BF16), r2(rg_bx),
                 _block_diag(rg_wa).astype(BF16), r2(rg_ba), r2(rg_lambda))
    o_d = _dsa(seq(dq), seq(dk), seq(dv), seq(dqi), seq(dki), dwt)
    flat = lambda a: a.reshape(N, GROUP_WIDTH)
    x1, top_i, top_g, rank, counts = _outproj(flat(o_a), flat(o_r), flat(o_c), flat(o_d), x2,
                                              w_out.astype(BF16), r2(ln1_g), r2(ln1_b),
                                              router_w.T.astype(BF16), router_b.reshape(-1, 1))
    top_i, top_g, rank = top_i.T, top_g.T, rank.T
    n_tiles = (N * TOP_K) // MOE_TM + N_EXPERTS
    pos, tile_expert, tile_valid = _routing_tables(top_i, rank, counts, n_tiles)
    xs = _dispatch(pos, x1, tile_valid)
    y_sorted = _moe_experts(xs, tile_expert, tile_valid, exp_w1, exp_b1.reshape(N_EXPERTS, 1, -1), exp_w2,
                            exp_b2.reshape(N_EXPERTS, 1, -1), layer)
    return _combine(pos, y_sorted, x1, top_g, r2(ln2_g), r2(ln2_b))


def kernel(x, w_in, ret_gn_g, ret_gn_b, conv_w, conv_b, rg_wx, rg_bx, rg_wa, rg_ba, rg_lambda, w_out,
           ln1_g, ln1_b, router_w, router_b, exp_w1, exp_b1, exp_w2, exp_b2, ln2_g, ln2_b):
    B, T, D = x.shape
    cos_t, sin_t = _rope_tables(T)
    tabs = _ret_tables()
    x2 = x.reshape(B * T, D)
    for l in range(w_in.shape[0]):
        x2 = _layer(x2, B, T, cos_t, sin_t, tabs, w_in[l], ret_gn_g[l], ret_gn_b[l], conv_w[l], conv_b[l],
                    rg_wx[l], rg_bx[l], rg_wa[l], rg_ba[l], rg_lambda[l], w_out[l], ln1_g[l], ln1_b[l],
                    router_w[l], router_b[l], exp_w1, exp_b1[l], exp_w2, exp_b2[l], ln2_g[l], ln2_b[l], l)
    return x2.reshape(B, T, D)
```

```python
import functools

import numpy as np
import jax
import jax.numpy as jnp
from jax import lax
from jax.experimental import pallas as pl
from jax.experimental.pallas import tpu as pltpu

F32 = jnp.float32
BF16 = jnp.bfloat16
I32 = jnp.int32

D_MODEL = 1024
DEPTH = 2
HEAD_DIM = 64
GROUP_WIDTH = 256
GROUP_HEADS = 4
ROPE_THETA = 10000.0
MOBA_Q = 256
MOBA_BLOCK = 256
MOBA_TOPK = 3
MOBA_MAX_BLOCKS = 16
RET_CHUNK = 128
RG_CONV = 4
RG_C = 8.0
IDX_HEADS = 8
IDX_DIM = 64
IDX_SCALE = (IDX_HEADS ** -0.5) * (IDX_DIM ** -0.5)
DSA_TOPK = 256
DSA_Q = 256
DSA_KEY_BLOCK = 512
DSA_ATT_BLOCK = 512
N_EXPERTS = 32
TOP_K = 4
D_FF = 1024
SWIGLU_LIMIT = 7.0
SWIGLU_ALPHA = 1.702
ALPHA = (2 * DEPTH) ** 0.25
LN_EPS = 1e-5
IN_WIDTH = 12 * GROUP_WIDTH + IDX_HEADS * IDX_DIM + IDX_DIM + IDX_HEADS
IN_PAD = 15 * GROUP_WIDTH

LANES = 128
SUBLANES = 8
HEAD_SHIFT = HEAD_DIM.bit_length() - 1
KEY_MAGNITUDE_BITS = 0x7FFFFFFF
BISECT_MAX_STEPS = 36
NEG = -3.0e38
M_FLOOR = -1.5e38
BIG = 3.0e38
VMEM_LIMIT = 56 * 1024 * 1024
FOLD_ROWS = 32

PROJ_TM = 256
OUT_TM = 512
RG_TC = 256
MOE_TM = 512
DSP_TM = 512
CMB_TM = 256


def _cparams(ndims):
    return pltpu.CompilerParams(dimension_semantics=("arbitrary",) * ndims,
                                vmem_limit_bytes=VMEM_LIMIT)


def _dot(a, b, precision=None):
    return jnp.dot(a, b, preferred_element_type=F32, precision=precision)


def _dot_nt(a, b, precision=None):
    return lax.dot_general(a, b, (((1,), (1,)), ((), ())), preferred_element_type=F32,
                           precision=precision)


def _dot_tn(a, b):
    return lax.dot_general(a, b, (((0,), (0,)), ((), ())), preferred_element_type=F32)


def _head_stack(q):
    head = lax.shift_right_logical(lax.broadcasted_iota(I32, q.shape, 1), HEAD_SHIFT)
    qf = q.astype(F32)
    return jnp.concatenate([jnp.where(head == h, qf, 0.0) for h in range(GROUP_HEADS)],
                           axis=0).astype(q.dtype)


def _head_unstack(s, rows):
    head = lax.shift_right_logical(lax.broadcasted_iota(I32, (rows, GROUP_WIDTH), 1), HEAD_SHIFT)
    out = jnp.zeros((rows, GROUP_WIDTH), F32)
    for h in range(GROUP_HEADS):
        out = out + jnp.where(head == h, s[h * rows:(h + 1) * rows], 0.0)
    return out


def _fold_rows(x, op):
    return op(x.reshape(x.shape[0] // FOLD_ROWS, FOLD_ROWS, x.shape[1]), axis=0)


def _layer_norm_rows(y, g, b):
    mu = jnp.mean(y, axis=-1, keepdims=True)
    yc = y - mu
    var = jnp.mean(yc * yc, axis=-1, keepdims=True)
    return yc * lax.rsqrt(var + LN_EPS) * g + b


def _proj_kernel(x_ref, w_ref, cos_ref, sin_ref,
                 aq, ak, av, rq, rk, rv, rg, cx, cg, dq, dk, dv, dqi, dki, dwt):
    xb = x_ref[...].astype(BF16)
    cos = cos_ref[...]
    sin = sin_ref[...]
    first_half = (lax.broadcasted_iota(I32, cos.shape, 1) & (HEAD_DIM - 1)) < (HEAD_DIM // 2)

    def seg(i):
        return _dot(xb, w_ref[:, i * GROUP_WIDTH:(i + 1) * GROUP_WIDTH])

    def rope(p):
        rot = jnp.where(first_half, pltpu.roll(p, GROUP_WIDTH - HEAD_DIM // 2, 1),
                        pltpu.roll(p, HEAD_DIM // 2, 1))
        return p * cos + rot * sin

    aq[...] = rope(seg(0)).astype(BF16)
    ak[...] = rope(seg(1)).astype(BF16)
    av[...] = seg(2).astype(BF16)
    rq[...] = rope(seg(3)).astype(BF16)
    rk[...] = (rope(seg(4)) * (HEAD_DIM ** -0.5)).astype(BF16)
    rv[...] = seg(5).astype(BF16)
    rg[...] = seg(6)
    cx[...] = seg(7)
    cg[...] = seg(8)
    dq[...] = rope(seg(9)).astype(BF16)
    dk[...] = rope(seg(10)).astype(BF16)
    dv[...] = seg(11).astype(BF16)
    dqi[:, 0:GROUP_WIDTH] = rope(seg(12)).astype(BF16)
    dqi[:, GROUP_WIDTH:2 * GROUP_WIDTH] = rope(seg(13)).astype(BF16)
    last = seg(14)
    dki[...] = rope(last)[:, 0:IDX_DIM].astype(BF16)
    dwt[...] = last[:, LANES:2 * LANES].T[0:IDX_HEADS, :]


def _proj(x2, w_pad, cos_t, sin_t, T):
    N = x2.shape[0]
    tm = PROJ_TM
    tpb = T // tm
    row = lambda w: pl.BlockSpec((tm, w), lambda i: (i, 0))
    tab = pl.BlockSpec((tm, GROUP_WIDTH), lambda i: (i % tpb, 0))
    widths = [256] * 12 + [512, IDX_DIM]
    dtypes = [BF16, BF16, BF16, BF16, BF16, BF16, F32, F32, F32, BF16, BF16, BF16, BF16, BF16]
    return pl.pallas_call(
        _proj_kernel,
        grid=(N // tm,),
        in_specs=[row(D_MODEL), pl.BlockSpec((D_MODEL, IN_PAD), lambda i: (0, 0)), tab, tab],
        out_specs=[row(w) for w in widths] + [pl.BlockSpec((IDX_HEADS, tm), lambda i: (0, i))],
        out_shape=[jax.ShapeDtypeStruct((N, w), d) for w, d in zip(widths, dtypes)]
        + [jax.ShapeDtypeStruct((IDX_HEADS, N), F32)],
        compiler_params=_cparams(1),
        name="proj_rope",
    )(x2, w_pad, cos_t, sin_t)


def _moba_kernel(q_ref, k_ref, v_ref, o_ref, kmean_ref, sel_ref, qk_ref, m_ref, l_ref, acc_ref, *, n_blocks):
    j = pl.program_id(1)
    R = MOBA_Q
    SR = GROUP_HEADS * R
    KB = MOBA_BLOCK

    @pl.when(j == 0)
    def _():
        kmean_ref[...] = jnp.zeros_like(kmean_ref)
        for n in range(n_blocks):
            kb = k_ref[0, n * KB:(n + 1) * KB, :].astype(F32)
            kmean_ref[n:n + 1, :] = jnp.mean(kb, axis=0, keepdims=True)

    own = j // (KB // R)
    q_raw = _head_stack(q_ref[0])
    q_stack = (q_raw.astype(F32) * (HEAD_DIM ** -0.5)).astype(BF16)

    gate = _dot_nt(kmean_ref[...], q_raw.astype(F32), precision=lax.Precision.HIGHEST)
    blk = lax.broadcasted_iota(I32, gate.shape, 0)
    past = blk < own
    g = jnp.where(past, gate, -jnp.inf)
    sel = jnp.zeros(gate.shape, F32)
    for _ in range(MOBA_TOPK):
        mx = jnp.max(g, axis=0, keepdims=True)
        first = jnp.min(jnp.where(g == mx, blk, MOBA_MAX_BLOCKS), axis=0, keepdims=True)
        pick = blk == first
        sel = jnp.where(pick & past, 1.0, sel)
        g = jnp.where(pick, -jnp.inf, g)
    sel_ref[...] = sel

    m_ref[...] = jnp.full(m_ref.shape, M_FLOOR, F32)
    l_ref[...] = jnp.zeros(l_ref.shape, F32)
    acc_ref[...] = jnp.zeros(acc_ref.shape, F32)

    def softmax_pv(s_raw, vb, bias):
        s = s_raw + bias
        m_old = m_ref[...]
        m_new = jnp.maximum(m_old, jnp.max(s, axis=0, keepdims=True))
        alpha = jnp.exp(m_old - m_new)
        p = jnp.exp(s - m_new)
        l_ref[...] = alpha * l_ref[...] + jnp.sum(p, axis=0, keepdims=True)
        acc_ref[...] = alpha * acc_ref[...] + _dot_tn(vb, p.astype(BF16))
        m_ref[...] = m_new

    def block_bias(n):
        return jnp.broadcast_to(jnp.where(sel_ref[pl.ds(n, 1), :] > 0.5, 0.0, NEG), (KB, SR))

    def qk(first_block, rows):
        st = pl.multiple_of(first_block * KB, KB)
        return _dot_nt(k_ref[0, pl.ds(st, rows), :], q_stack)

    def attend_span(first_block, bias, s_raw=None):
        st = pl.multiple_of(first_block * KB, KB)
        rows = bias.shape[0]
        s_raw = qk(first_block, rows) if s_raw is None else s_raw
        softmax_pv(s_raw, v_ref[0, pl.ds(st, rows), :], bias)

    keypos = own * KB + lax.broadcasted_iota(I32, (KB, SR), 0)
    qpos = j * R + (lax.broadcasted_iota(I32, (KB, SR), 1) & (R - 1))
    causal = jnp.where(keypos <= qpos, 0.0, NEG)
    odd = (own & 1) == 1

    @pl.when(odd)
    def _():
        attend_span(own - 1, jnp.concatenate([block_bias(own - 1), causal], axis=0))

    @pl.when(jnp.logical_not(odd))
    def _():
        attend_span(own, causal)

    n_pairs = own // 2

    @pl.when(n_pairs > 0)
    def _():
        qk_ref[...] = qk(0, 2 * KB)

    def body(pair, c):
        s_raw = qk_ref[...]
        qk_ref[...] = qk(2 * jnp.minimum(pair + 1, n_pairs - 1), 2 * KB)
        attend_span(2 * pair, jnp.concatenate([block_bias(2 * pair), block_bias(2 * pair + 1)], axis=0), s_raw)
        return c

    lax.fori_loop(0, n_pairs, body, 0)
    o_ref[0] = _head_unstack((acc_ref[...] / l_ref[...]).T, R).astype(o_ref.dtype)


def _moba(q, k, v):
    B, T, _ = q.shape
    n_blocks = T // MOBA_BLOCK
    assert T % MOBA_BLOCK == 0 and n_blocks <= MOBA_MAX_BLOCKS
    SR = GROUP_HEADS * MOBA_Q
    return pl.pallas_call(
        functools.partial(_moba_kernel, n_blocks=n_blocks),
        grid=(B, T // MOBA_Q),
        in_specs=[pl.BlockSpec((1, MOBA_Q, GROUP_WIDTH), lambda b, j: (b, j, 0)),
                  pl.BlockSpec((1, T, GROUP_WIDTH), lambda b, j: (b, 0, 0)),
                  pl.BlockSpec((1, T, GROUP_WIDTH), lambda b, j: (b, 0, 0))],
        out_specs=pl.BlockSpec((1, MOBA_Q, GROUP_WIDTH), lambda b, j: (b, j, 0)),
        out_shape=jax.ShapeDtypeStruct((B, T, GROUP_WIDTH), BF16),
        scratch_shapes=[pltpu.VMEM((MOBA_MAX_BLOCKS, GROUP_WIDTH), F32),
                        pltpu.VMEM((MOBA_MAX_BLOCKS, SR), F32),
                        pltpu.VMEM((2 * MOBA_BLOCK, SR), F32),
                        pltpu.VMEM((1, SR), F32), pltpu.VMEM((1, SR), F32),
                        pltpu.VMEM((GROUP_WIDTH, SR), F32)],
        compiler_params=_cparams(2),
        name="moba_attention",
    )(q, k, v)


def _ret_kernel(q_ref, k_ref, v_ref, g_ref, dmask_ref, xi_ref, zeta_ref, gdec_ref, bd_ref, avg_ref,
                gng_ref, gnb_ref, o_ref, r_ref):
    j = pl.program_id(0)

    @pl.when(j == 0)
    def _():
        r_ref[...] = jnp.zeros_like(r_ref)

    C = RET_CHUNK
    hp = lax.Precision.HIGHEST
    for b in range(q_ref.shape[0]):
        q = q_ref[b]
        k = k_ref[b]
        v = v_ref[b]
        q_stack = _head_stack(q)
        inner = _dot_nt(q_stack, k) * dmask_ref[...]
        o = _head_unstack(_dot(inner.astype(BF16), v), C)
        R = r_ref[b]
        o = o + _dot(q, R.astype(BF16)) * xi_ref[...]
        kz = (k.astype(F32) * zeta_ref[...]).astype(BF16)
        r_ref[b] = gdec_ref[...] * R + bd_ref[...] * _dot_tn(kz, v)

        mu = _dot(o, avg_ref[...], precision=hp)
        oc = o - mu
        var = _dot(oc * oc, avg_ref[...], precision=hp)
        y = oc * lax.rsqrt(var + LN_EPS) * gng_ref[...] + gnb_ref[...]
        gte = g_ref[b]
        o_ref[b] = (y * (gte * jax.nn.sigmoid(gte))).astype(o_ref.dtype)


def _ret_tables():
    H, C, d = GROUP_HEADS, RET_CHUNK, HEAD_DIM
    log_g = np.log(1.0 - 2.0 ** (-5.0 - np.arange(H, dtype=np.float64)))
    n = np.arange(C, dtype=np.float64)
    diff = n[:, None] - n[None, :]
    dmask = np.where(diff >= 0, np.exp(log_g[:, None, None] * np.maximum(diff, 0.0)), 0.0)
    xi = np.exp(log_g[:, None] * (n + 1.0))
    zeta = np.exp(log_g[:, None] * (C - 1.0 - n))
    g_chunk = np.exp(log_g * C)
    head = np.arange(GROUP_WIDTH) // d
    bd = (head[:, None] == head[None, :]).astype(np.float64)
    to32 = lambda a: jnp.asarray(a, dtype=F32)
    return dict(dmask=to32(dmask.reshape(H * C, C)), xi=to32(xi.T[:, head]), zeta=to32(zeta.T[:, head]),
                gdec=to32(bd * g_chunk[head][:, None]), bd=to32(bd), avg=to32(bd / d))


def _retention(rq, rk, rv, rg, gn_g, gn_b, tabs):
    B, T, _ = rq.shape
    C = RET_CHUNK
    blk = pl.BlockSpec((B, C, GROUP_WIDTH), lambda j: (0, j, 0))
    const = lambda a: pl.BlockSpec(a.shape, lambda j: (0,) * a.ndim)
    consts = [tabs["dmask"], tabs["xi"], tabs["zeta"], tabs["gdec"], tabs["bd"], tabs["avg"], gn_g, gn_b]
    return pl.pallas_call(
        _ret_kernel,
        grid=(T // C,),
        in_specs=[blk, blk, blk, blk] + [const(a) for a in consts],
        out_specs=blk,
        out_shape=jax.ShapeDtypeStruct((B, T, GROUP_WIDTH), BF16),
        scratch_shapes=[pltpu.VMEM((B, GROUP_WIDTH, GROUP_WIDTH), F32)],
        compiler_params=_cparams(1),
        name="retention",
    )(rq, rk, rv, rg, *consts)


def _rglru_kernel(x_ref, g_ref, cw_ref, cb_ref, wx_ref, bx_ref, wa_ref, ba_ref, lam_ref, o_ref,
                  xbuf, h_ref):
    j = pl.program_id(1)
    tc = RG_TC

    @pl.when(j == 0)
    def _():
        xbuf[0:SUBLANES, :] = jnp.zeros((SUBLANES, GROUP_WIDTH), F32)
        h_ref[...] = jnp.zeros_like(h_ref)

    xbuf[SUBLANES:SUBLANES + tc, :] = x_ref[0]
    xc = cb_ref[...] + cw_ref[RG_CONV - 1:RG_CONV, :] * xbuf[SUBLANES:SUBLANES + tc, :]
    for i in range(RG_CONV - 1):
        off = SUBLANES - (RG_CONV - 1) + i
        xc = xc + cw_ref[i:i + 1, :] * xbuf[off:off + tc, :]
    xbuf[0:SUBLANES, :] = xbuf[tc:tc + SUBLANES, :]

    xcb = xc.astype(BF16)
    gate_x = jax.nn.sigmoid(_dot(xcb, wx_ref[...]) + bx_ref[...])
    gate_a = jax.nn.sigmoid(_dot(xcb, wa_ref[...]) + ba_ref[...])
    lam = lam_ref[...]
    softplus_neg = jnp.maximum(-lam, 0.0) + jnp.log1p(jnp.exp(-jnp.abs(lam)))
    log_a = -RG_C * gate_a * softplus_neg
    a = jnp.exp(log_a)
    th = jnp.tanh(log_a)
    b = jnp.sqrt(-2.0 * th / (1.0 - th)) * (gate_x * xc)

    row = lax.broadcasted_iota(I32, (tc, GROUP_WIDTH), 0)
    d = 1
    while d < tc:
        keep = row >= d
        a_sh = jnp.where(keep, pltpu.roll(a, d, 0), 1.0)
        b_sh = jnp.where(keep, pltpu.roll(b, d, 0), 0.0)
        b = a * b_sh + b
        a = a * a_sh
        d *= 2
    h = b + a * h_ref[...]
    h_ref[...] = h[tc - 1:tc, :]

    xg = g_ref[0]
    gelu = 0.5 * xg * (1.0 + jnp.tanh(np.sqrt(2.0 / np.pi) * (xg + 0.044715 * xg * xg * xg)))
    o_ref[0] = (h * gelu).astype(o_ref.dtype)


def _block_diag(w):
    n, c, _ = w.shape
    eye = jnp.eye(n, dtype=w.dtype)
    return (eye[:, None, :, None] * w[:, :, None, :]).reshape(n * c, n * c)


def _rglru(cx, cg, conv_w, conv_b, wx, bx, wa, ba, lam):
    B, T, _ = cx.shape
    tc = RG_TC
    blk = pl.BlockSpec((1, tc, GROUP_WIDTH), lambda b, j: (b, j, 0))
    const = lambda a: pl.BlockSpec(a.shape, lambda b, j: (0,) * a.ndim)
    consts = [conv_w, conv_b, wx, bx, wa, ba, lam]
    return pl.pallas_call(
        _rglru_kernel,
        grid=(B, T // tc),
        in_specs=[blk, blk] + [const(a) for a in consts],
        out_specs=blk,
        out_shape=jax.ShapeDtypeStruct((B, T, GROUP_WIDTH), BF16),
        scratch_shapes=[pltpu.VMEM((tc + SUBLANES, GROUP_WIDTH), F32), pltpu.VMEM((1, GROUP_WIDTH), F32)],
        compiler_params=_cparams(2),
        name="rg_lru",
    )(cx, cg, *consts)


def _dsa_kernel(q_ref, k_ref, v_ref, qi_ref, ki_ref, wt_ref, o_ref,
                sc_ref, jcut_ref, qk_ref, m_ref, l_ref, acc_ref, *, n_sel, n_keys):
    j = pl.program_id(1)
    R = DSA_Q
    KB = DSA_KEY_BLOCK
    SR = GROUP_HEADS * R
    nkb = (j * R + R + KB - 1) // KB
    nsel = float(n_sel)

    rowk = lax.broadcasted_iota(I32, (KB, R), 0)
    qpos = j * R + lax.broadcasted_iota(I32, (KB, R), 1)

    qi = qi_ref[0]
    qi_stack = jnp.concatenate([qi[:, h * IDX_DIM:(h + 1) * IDX_DIM] for h in range(IDX_HEADS)], axis=0)
    wt = wt_ref[...]

    def score_body(kb, c):
        st = pl.multiple_of(kb * KB, KB)
        rel = _dot_nt(ki_ref[0, pl.ds(st, KB), :], qi_stack)
        sc = wt[0:1, :] * jnp.maximum(rel[:, 0:R], 0.0)
        for h in range(1, IDX_HEADS):
            sc = sc + wt[h:h + 1, :] * jnp.maximum(rel[:, h * R:(h + 1) * R], 0.0)
        sc_ref[pl.ds(st, KB), :] = jnp.where(st + rowk <= qpos, sc * IDX_SCALE, NEG)
        return c

    lax.fori_loop(0, nkb, score_body, 0)

    def blocks(fn, init):
        def body(kb, c):
            st = pl.multiple_of(kb * KB, KB)
            return fn(st, sc_ref[pl.ds(st, KB), :], c)
        return lax.fori_loop(0, nkb, body, init)

    def survey(st, s, c):
        mn, mx, gt, ge = c
        return (jnp.minimum(mn, _fold_rows(jnp.where(s > 0.5 * NEG, s, BIG), jnp.min)),
                jnp.maximum(mx, _fold_rows(s, jnp.max)),
                gt + _fold_rows(jnp.where(s > 0.0, 1.0, 0.0), jnp.sum),
                ge + _fold_rows(jnp.where(s >= 0.0, 1.0, 0.0), jnp.sum))

    zeros_fold = jnp.zeros((FOLD_ROWS, R), F32)
    mn8, mx8, gt8, ge8 = blocks(survey, (jnp.full((FOLD_ROWS, R), BIG, F32), jnp.full((FOLD_ROWS, R), NEG, F32),
                                         zeros_fold, zeros_fold))
    mn = jnp.min(mn8, axis=0, keepdims=True)
    mx = jnp.max(mx8, axis=0, keepdims=True)
    c_gt0 = jnp.sum(gt8, axis=0, keepdims=True)
    c_ge0 = jnp.sum(ge8, axis=0, keepdims=True)

    def count_ge(th):
        acc = blocks(lambda st, s, c: c + _fold_rows(jnp.where(s >= th, 1.0, 0.0), jnp.sum),
                     jnp.zeros((FOLD_ROWS, R), F32))
        return jnp.sum(acc, axis=0, keepdims=True)

    n_adm = (j * R + 1 + lax.broadcasted_iota(I32, (1, R), 1)).astype(F32)
    need = n_adm > nsel
    lo0 = jnp.where(need, mn, 0.5 * NEG)
    above = need & (c_gt0 >= nsel)
    below = need & (c_ge0 < nsel)
    at_zero = need & jnp.logical_not(above | below)
    cgt0 = jnp.where(below, c_ge0, jnp.where(at_zero, c_gt0, 0.0))
    act0 = jnp.where(above | below, 1.0, 0.0)
    tie0 = jnp.where(at_zero, 1.0, 0.0)

    def to_key(f):
        b = lax.bitcast_convert_type(f, I32)
        return b ^ (lax.shift_right_arithmetic(b, 31) & KEY_MAGNITUDE_BITS)

    def from_key(kk):
        return lax.bitcast_convert_type(kk ^ (lax.shift_right_arithmetic(kk, 31) & KEY_MAGNITUDE_BITS), F32)

    def bis_step(klo, khi, cgt, act, tie):
        on = act > 0.0
        kmid = lax.shift_right_arithmetic(klo, 1) + lax.shift_right_arithmetic(khi, 1) + (klo & khi & 1)
        stuck = kmid == klo
        cnt = count_ge(from_key(kmid))
        go = on & jnp.logical_not(stuck)
        up = go & (cnt >= nsel)
        dn = go & (cnt < nsel)
        return (jnp.where(up, kmid, klo), jnp.where(dn, kmid, khi), jnp.where(dn, cnt, cgt),
                jnp.where(go & (cnt != nsel), 1.0, 0.0), jnp.where(on & stuck, 1.0, tie))

    def bis_cond(c):
        return (c[1] > 0.0) & (c[0] < BISECT_MAX_STEPS)

    def bis_body(c):
        flag = jnp.max(c[5])
        st = bis_step(*bis_step(*c[2:]))
        return (c[0] + 2, flag) + st

    klo0 = jnp.where(above, 1, jnp.where(at_zero, 0, to_key(lo0)))
    khi0 = jnp.where(below, 0, to_key(mx) + 1)
    res = lax.while_loop(bis_cond, bis_body, (jnp.int32(0), jnp.max(act0), klo0, khi0, cgt0, act0, tie0))
    lo, cgt, tie = from_key(res[2]), res[4], res[6]

    jcut_ref[...] = jnp.full((1, R), float(n_keys), F32)

    @pl.when(jnp.max(tie) > 0.0)
    def _():
        want = nsel - cgt
        tied = tie > 0.0

        def jb(it, c):
            a, b = c
            mid = jnp.floor((a + b) * 0.5)
            hit8 = blocks(lambda st, s, cc: cc + _fold_rows(
                jnp.where((s == lo) & ((st + rowk).astype(F32) <= mid), 1.0, 0.0), jnp.sum),
                jnp.zeros((FOLD_ROWS, R), F32))
            ok = jnp.sum(hit8, axis=0, keepdims=True) >= want
            return jnp.where(ok, a, mid), jnp.where(ok, mid, b)

        _, b = lax.fori_loop(0, int(np.ceil(np.log2(n_keys))) + 1, jb,
                             (jnp.full((1, R), -1.0, F32), jnp.full((1, R), float(n_keys - 1), F32)))
        jcut_ref[...] = jnp.where(tied, b, float(n_keys))

    q_stack = _head_stack((q_ref[0].astype(F32) * (HEAD_DIM ** -0.5)).astype(BF16))
    m_ref[...] = jnp.full(m_ref.shape, M_FLOOR, F32)
    l_ref[...] = jnp.zeros(l_ref.shape, F32)
    acc_ref[...] = jnp.zeros(acc_ref.shape, F32)
    jcut = jcut_ref[...]
    AB = DSA_ATT_BLOCK
    nab = (j * R + R + AB - 1) // AB
    rowf = lax.broadcasted_iota(I32, (AB, R), 0).astype(F32)

    def qk(kb):
        st = pl.multiple_of(kb * AB, AB)
        return _dot_nt(k_ref[0, pl.ds(st, AB), :], q_stack)

    qk_ref[...] = qk(0)

    def att_body(kb, c):
        st = pl.multiple_of(kb * AB, AB)
        s_raw = qk_ref[...]
        qk_ref[...] = qk(jnp.minimum(kb + 1, nab - 1))
        sc = sc_ref[pl.ds(st, AB), :]
        keep = (sc > lo) | ((sc == lo) & (rowf <= jcut - st.astype(F32)))
        bias = jnp.where(keep, 0.0, NEG)
        s = s_raw + jnp.concatenate([bias] * GROUP_HEADS, axis=1)
        m_old = m_ref[...]
        m_new = jnp.maximum(m_old, jnp.max(s, axis=0, keepdims=True))
        alpha = jnp.exp(m_old - m_new)
        p = jnp.exp(s - m_new)
        l_ref[...] = alpha * l_ref[...] + jnp.sum(p, axis=0, keepdims=True)
        acc_ref[...] = alpha * acc_ref[...] + _dot_tn(v_ref[0, pl.ds(st, AB), :], p.astype(BF16))
        m_ref[...] = m_new
        return c

    lax.fori_loop(0, nab, att_body, 0)
    o_ref[0] = _head_unstack((acc_ref[...] / l_ref[...]).T, R).astype(o_ref.dtype)


def _dsa(q, k, v, qi, ki, wt):
    B, T, _ = q.shape
    KB = DSA_KEY_BLOCK
    assert T % KB == 0
    n_sel = min(DSA_TOPK, T // 4)
    R = DSA_Q
    assert T % R == 0
    SR = GROUP_HEADS * R
    nq = T // R
    qblk = lambda wd: pl.BlockSpec((1, R, wd), lambda b, j: (b, j, 0))
    full = lambda wd: pl.BlockSpec((1, T, wd), lambda b, j: (b, 0, 0))
    return pl.pallas_call(
        functools.partial(_dsa_kernel, n_sel=n_sel, n_keys=T),
        grid=(B, nq),
        in_specs=[qblk(GROUP_WIDTH), full(GROUP_WIDTH), full(GROUP_WIDTH),
                  qblk(IDX_HEADS * IDX_DIM), full(IDX_DIM),
                  pl.BlockSpec((IDX_HEADS, R), lambda b, j: (0, b * nq + j))],
        out_specs=qblk(GROUP_WIDTH),
        out_shape=jax.ShapeDtypeStruct((B, T, GROUP_WIDTH), BF16),
        scratch_shapes=[pltpu.VMEM((T, R), F32), pltpu.VMEM((1, R), F32), pltpu.VMEM((DSA_ATT_BLOCK, SR), F32),
                        pltpu.VMEM((1, SR), F32), pltpu.VMEM((1, SR), F32),
                        pltpu.VMEM((GROUP_WIDTH, SR), F32)],
        compiler_params=_cparams(2),
        name="dsa_attention",
    )(q, k, v, qi, ki, wt)


def _outproj_kernel(oa, orr, oc, od, x_ref, w_ref, g_ref, b_ref, rwt_ref, rb_ref, utri_ref,
                    x1_ref, ti_ref, tg_ref, rk_ref, cnt_ref, run_ref):
    GW = GROUP_WIDTH

    @pl.when(pl.program_id(0) == 0)
    def _():
        run_ref[...] = jnp.zeros_like(run_ref)

    acc = _dot(oa[...], w_ref[0:GW, :])
    acc = acc + _dot(orr[...], w_ref[GW:2 * GW, :])
    acc = acc + _dot(oc[...], w_ref[2 * GW:3 * GW, :])
    acc = acc + _dot(od[...], w_ref[3 * GW:4 * GW, :])
    x1 = _layer_norm_rows(ALPHA * x_ref[...] + acc, g_ref[...], b_ref[...])
    x1_ref[...] = x1

    logits = _dot_nt(rwt_ref[...], x1.astype(BF16)) + rb_ref[...]
    row = lax.broadcasted_iota(I32, logits.shape, 0)
    krow = lax.broadcasted_iota(I32, ti_ref.shape, 0)
    g = logits
    ti = jnp.zeros(ti_ref.shape, I32)
    tv = jnp.zeros(tg_ref.shape, F32)
    picks = []
    for kk in range(TOP_K):
        mx = jnp.max(g, axis=0, keepdims=True)
        first = jnp.min(jnp.where(g == mx, row, N_EXPERTS), axis=0, keepdims=True)
        ti = jnp.where(krow == kk, first, ti)
        tv = jnp.where(krow == kk, mx, tv)
        picks.append(row == first)
        g = jnp.where(picks[-1], -jnp.inf, g)
    e = jnp.exp(tv - jnp.max(tv, axis=0, keepdims=True))
    ti_ref[...] = ti
    tg_ref[...] = e / jnp.sum(e, axis=0, keepdims=True)

    sel = jnp.where(picks[0] | picks[1] | picks[2] | picks[3], 1.0, 0.0)
    before = run_ref[...] + _dot(sel.astype(BF16), utri_ref[...])
    rk = jnp.zeros(rk_ref.shape, F32)
    for kk in range(TOP_K):
        rk = jnp.where(krow == kk, jnp.sum(jnp.where(picks[kk], before, 0.0), axis=0, keepdims=True), rk)
    rk_ref[...] = rk.astype(I32)
    run_ref[...] = run_ref[...] + jnp.sum(sel, axis=1, keepdims=True)
    cnt_ref[...] = run_ref[...]


def _outproj(oa, orr, oc, od, x2, w_out, g, b, rwt, rb):
    N = x2.shape[0]
    tm = OUT_TM
    row = lambda w: pl.BlockSpec((tm, w), lambda i: (i, 0))
    picks = pl.BlockSpec((TOP_K, tm), lambda i: (0, i))
    const = lambda a: pl.BlockSpec(a.shape, lambda i: (0,) * a.ndim)
    utri = jnp.asarray(np.triu(np.ones((tm, tm), np.float32), 1), dtype=BF16)
    return pl.pallas_call(
        _outproj_kernel,
        grid=(N // tm,),
        in_specs=[row(GROUP_WIDTH)] * 4 + [row(D_MODEL), const(w_out), const(g), const(b), const(rwt), const(rb),
                                           const(utri)],
        out_specs=[row(D_MODEL), picks, picks, picks, pl.BlockSpec((N_EXPERTS, 1), lambda i: (0, 0))],
        out_shape=[jax.ShapeDtypeStruct((N, D_MODEL), F32), jax.ShapeDtypeStruct((TOP_K, N), I32),
                   jax.ShapeDtypeStruct((TOP_K, N), F32), jax.ShapeDtypeStruct((TOP_K, N), I32),
                   jax.ShapeDtypeStruct((N_EXPERTS, 1), F32)],
        scratch_shapes=[pltpu.VMEM((N_EXPERTS, 1), F32)],
        compiler_params=_cparams(1),
        name="outproj_ln_router",
    )(oa, orr, oc, od, x2, w_out, g, b, rwt, rb, utri)


def _dispatch_kernel(tv_ref, pos_ref, x_ref, xs_hbm, xbuf, sem, zsem, *, n_tiles):
    i = pl.program_id(0)
    n = pl.num_programs(0)
    tm = DSP_TM
    par = i % 2

    @pl.when(i == 0)
    def _():
        xbuf[1] = jnp.zeros((tm, D_MODEL), F32)

        def fill(t, c):
            @pl.when(tv_ref[t] < MOE_TM)
            def _():
                pltpu.make_async_copy(xbuf.at[1], xs_hbm.at[pl.ds(pl.multiple_of(t * MOE_TM, MOE_TM), MOE_TM)],
                                      zsem).start()
            return c

        def drain(t, c):
            @pl.when(tv_ref[t] < MOE_TM)
            def _():
                pltpu.make_async_copy(xbuf.at[1], xs_hbm.at[pl.ds(0, MOE_TM)], zsem).wait()
            return c

        lax.fori_loop(0, n_tiles, fill, 0)
        lax.fori_loop(0, n_tiles, drain, 0)

    def wait_step(p):
        for _ in range(TOP_K):
            pltpu.make_async_copy(xbuf.at[p], xbuf.at[p], sem.at[p]).wait()

    @pl.when(i >= 2)
    def _():
        wait_step(par)

    xbuf[par] = x_ref[...]

    def body(r, c):
        for kk in range(TOP_K):
            pltpu.make_async_copy(xbuf.at[par, pl.ds(r, 1)], xs_hbm.at[pl.ds(pos_ref[r * TOP_K + kk], 1)],
                                  sem.at[par]).start()
        return c

    lax.fori_loop(0, tm, body, 0, unroll=4)

    @pl.when(i == n - 1)
    def _():
        wait_step(1 - par)
        wait_step(par)


def _dispatch(pos, x1, tile_valid):
    N = x1.shape[0]
    tm = DSP_TM
    n_tiles = tile_valid.shape[0]
    assert N // tm >= 2 and tm == MOE_TM
    grid_spec = pltpu.PrefetchScalarGridSpec(
        num_scalar_prefetch=1,
        grid=(N // tm,),
        in_specs=[pl.BlockSpec((tm * TOP_K,), lambda i, tv: (i,), memory_space=pltpu.SMEM),
                  pl.BlockSpec((tm, D_MODEL), lambda i, tv: (i, 0))],
        out_specs=pl.BlockSpec(memory_space=pl.ANY),
        scratch_shapes=[pltpu.VMEM((2, tm, D_MODEL), F32), pltpu.SemaphoreType.DMA((2,)),
                        pltpu.SemaphoreType.DMA(())],
    )
    return pl.pallas_call(
        functools.partial(_dispatch_kernel, n_tiles=n_tiles),
        grid_spec=grid_spec,
        out_shape=jax.ShapeDtypeStruct((n_tiles * MOE_TM, D_MODEL), F32),
        compiler_params=_cparams(1),
        name="moe_dispatch",
    )(tile_valid, pos, x1)


def _moe_kernel(te_ref, nv_ref, x_ref, w1_ref, b1_ref, w2_ref, b2_ref, y_ref, w1b, w2b):
    i = pl.program_id(0)
    tm = MOE_TM
    n_valid = nv_ref[i]

    @pl.when((i == 0) | (te_ref[i] != te_ref[jnp.maximum(i - 1, 0)]))
    def _():
        step = LANES
        for c in range(D_MODEL // step):
            w1b[c * step:(c + 1) * step, :] = w1_ref[0, 0, c * step:(c + 1) * step, :].astype(BF16)
        for c in range(D_FF // step):
            w2b[c * step:(c + 1) * step, :] = w2_ref[0, 0, c * step:(c + 1) * step, :].astype(BF16)

    @pl.when(n_valid > 0)
    def _():
        h = _dot(x_ref[...].astype(BF16), w1b[...]) + b1_ref[0]
        glu_in = jnp.minimum(h[:, :D_FF], SWIGLU_LIMIT)
        up = jnp.clip(h[:, D_FF:], -SWIGLU_LIMIT, SWIGLU_LIMIT)
        glu = glu_in * jax.nn.sigmoid(SWIGLU_ALPHA * glu_in)
        y_ref[...] = _dot(((up + 1.0) * glu).astype(BF16), w2b[...]) + b2_ref[0]

    @pl.when(n_valid == 0)
    def _():
        y_ref[...] = jnp.zeros_like(y_ref)


def _moe_experts(xs, tile_expert, tile_valid, w1, b1, w2, b2, layer):
    tm = MOE_TM
    n_tiles = tile_expert.shape[0]
    grid_spec = pltpu.PrefetchScalarGridSpec(
        num_scalar_prefetch=2,
        grid=(n_tiles,),
        in_specs=[
            pl.BlockSpec((tm, D_MODEL), lambda i, te, nv: (i, 0)),
            pl.BlockSpec((1, 1, D_MODEL, 2 * D_FF), lambda i, te, nv: (layer, te[i], 0, 0)),
            pl.BlockSpec((1, 1, 2 * D_FF), lambda i, te, nv: (te[i], 0, 0)),
            pl.BlockSpec((1, 1, D_FF, D_MODEL), lambda i, te, nv: (layer, te[i], 0, 0)),
            pl.BlockSpec((1, 1, D_MODEL), lambda i, te, nv: (te[i], 0, 0)),
        ],
        out_specs=pl.BlockSpec((tm, D_MODEL), lambda i, te, nv: (i, 0)),
        scratch_shapes=[pltpu.VMEM((D_MODEL, 2 * D_FF), BF16), pltpu.VMEM((D_FF, D_MODEL), BF16)],
    )
    return pl.pallas_call(
        _moe_kernel,
        grid_spec=grid_spec,
        out_shape=jax.ShapeDtypeStruct((n_tiles * tm, D_MODEL), F32),
        compiler_params=_cparams(1),
        name="moe_experts",
    )(tile_expert, tile_valid, xs, w1, b1, w2, b2)


def _combine_kernel(pos_ref, posn_ref, y_hbm, x1_ref, tg_ref, g_ref, b_ref, o_ref, ybuf, sem):
    i = pl.program_id(0)
    n = pl.num_programs(0)
    tm = CMB_TM
    slot = i % 2

    def issue(idx_ref, s):
        def body(r, c):
            for kk in range(TOP_K):
                pltpu.make_async_copy(y_hbm.at[pl.ds(idx_ref[r * TOP_K + kk], 1)],
                                      ybuf.at[s, kk, pl.ds(r, 1)], sem.at[s]).start()
            return c
        lax.fori_loop(0, tm, body, 0, unroll=4)

    @pl.when(i == 0)
    def _():
        issue(pos_ref, 0)

    @pl.when(i + 1 < n)
    def _():
        issue(posn_ref, 1 - slot)

    pltpu.make_async_copy(ybuf.at[slot], ybuf.at[slot], sem.at[slot]).wait()
    tg = tg_ref[...]
    moe = tg[:, 0:1] * ybuf[slot, 0]
    for kk in range(1, TOP_K):
        moe = moe + tg[:, kk:kk + 1] * ybuf[slot, kk]
    o_ref[...] = _layer_norm_rows(ALPHA * x1_ref[...] + moe, g_ref[...], b_ref[...])


def _combine(pos, y_sorted, x1, tg, g, b):
    N = x1.shape[0]
    tm = CMB_TM
    n = N // tm
    row = lambda w: pl.BlockSpec((tm, w), lambda i: (i, 0))
    const = lambda a: pl.BlockSpec(a.shape, lambda i: (0,) * a.ndim)
    return pl.pallas_call(
        _combine_kernel,
        grid=(n,),
        in_specs=[pl.BlockSpec((tm * TOP_K,), lambda i: (i,), memory_space=pltpu.SMEM),
                  pl.BlockSpec((tm * TOP_K,), lambda i: (jnp.minimum(i + 1, n - 1),), memory_space=pltpu.SMEM),
                  pl.BlockSpec(memory_space=pl.ANY), row(D_MODEL), row(TOP_K), const(g), const(b)],
        out_specs=row(D_MODEL),
        out_shape=jax.ShapeDtypeStruct((N, D_MODEL), F32),
        scratch_shapes=[pltpu.VMEM((2, TOP_K, tm, D_MODEL), F32), pltpu.SemaphoreType.DMA((2,))],
        compiler_params=_cparams(1),
        name="moe_combine_ln",
    )(pos, pos, y_sorted, x1, tg, g, b)


def _routing_tables(top_i, rank, counts_f, n_tiles):
    tm = MOE_TM
    counts = counts_f.reshape(-1).astype(I32)
    padded = ((counts + tm - 1) // tm) * tm
    ends = jnp.cumsum(padded)
    offsets = ends - padded
    onehot = top_i[:, :, None] == jnp.arange(N_EXPERTS, dtype=I32)[None, None, :]
    pos = jnp.sum(jnp.where(onehot, offsets[None, None, :], 0), axis=-1) + rank
    tile_start = jnp.arange(n_tiles, dtype=I32) * tm
    tile_expert = jnp.sum((ends[None, :] <= tile_start[:, None]).astype(I32), axis=1)
    tile_expert = jnp.minimum(tile_expert, N_EXPERTS - 1)
    n_used = ends[-1] // tm
    last_expert = tile_expert[jnp.maximum(n_used - 1, 0)]
    tile_expert = jnp.where(tile_start < ends[-1], tile_expert, last_expert)
    valid_end = (offsets + counts)[tile_expert]
    tile_valid = jnp.clip(valid_end - tile_start, 0, tm)
    return pos.reshape(-1).astype(I32), tile_expert.astype(I32), tile_valid.astype(I32)


def _rope_tables(T):
    inv = ROPE_THETA ** (-jnp.arange(0, HEAD_DIM, 2, dtype=F32) / HEAD_DIM)
    ang = jnp.arange(T, dtype=F32)[:, None] * inv[None, :]
    cos, sin = jnp.cos(ang), jnp.sin(ang)
    cos_t = jnp.tile(jnp.concatenate([cos, cos], axis=-1), (1, GROUP_HEADS))
    sin_t = jnp.tile(jnp.concatenate([-sin, sin], axis=-1), (1, GROUP_HEADS))
    return cos_t, sin_t


def _pad_w_in(w_in):
    base = 12 * GROUP_WIDTH + IDX_HEADS * IDX_DIM
    w = jnp.zeros((D_MODEL, IN_PAD), F32)
    w = w.at[:, :base + IDX_DIM].set(w_in[:, :base + IDX_DIM])
    w = w.at[:, base + LANES:base + LANES + IDX_HEADS].set(w_in[:, base + IDX_DIM:])
    return w.astype(BF16)


def _layer(x2, B, T, cos_t, sin_t, tabs, w_in, ret_gn_g, ret_gn_b, conv_w, conv_b, rg_wx, rg_bx, rg_wa,
           rg_ba, rg_lambda, w_out, ln1_g, ln1_b, router_w, router_b, exp_w1, exp_b1, exp_w2, exp_b2,
           ln2_g, ln2_b, layer):
    N = B * T
    r2 = lambda a: a.reshape(1, -1)
    (aq, ak, av, rq, rk, rv, rg, cx, cg, dq, dk, dv, dqi, dki, dwt) = _proj(x2, _pad_w_in(w_in), cos_t, sin_t, T)
    seq = lambda a: a.reshape(B, T, a.shape[-1])
    o_a = _moba(seq(aq), seq(ak), seq(av))
    o_r = _retention(seq(rq), seq(rk), seq(rv), seq(rg), r2(ret_gn_g), r2(ret_gn_b), tabs)
    o_c = _rglru(seq(cx), seq(cg), conv_w, r2(conv_b), _block_diag(rg_wx).astype(BF16), r2(rg_bx),
                 _block_diag(rg_wa).astype(BF16), r2(rg_ba), r2(rg_lambda))
    o_d = _dsa(seq(dq), seq(dk), seq(dv), seq(dqi), seq(dki), dwt)
    flat = lambda a: a.reshape(N, GROUP_WIDTH)
    x1, top_i, top_g, rank, counts = _outproj(flat(o_a), flat(o_r), flat(o_c), flat(o_d), x2,
                                              w_out.astype(BF16), r2(ln1_g), r2(ln1_b),
                                              router_w.T.astype(BF16), router_b.reshape(-1, 1))
    top_i, top_g, rank = top_i.T, top_g.T, rank.T
    n_tiles = (N * TOP_K) // MOE_TM + N_EXPERTS
    pos, tile_expert, tile_valid = _routing_tables(top_i, rank, counts, n_tiles)
    xs = _dispatch(pos, x1, tile_valid)
    y_sorted = _moe_experts(xs, tile_expert, tile_valid, exp_w1, exp_b1.reshape(N_EXPERTS, 1, -1), exp_w2,
                            exp_b2.reshape(N_EXPERTS, 1, -1), layer)
    return _combine(pos, y_sorted, x1, top_g, r2(ln2_g), r2(ln2_b))


def kernel(x, w_in, ret_gn_g, ret_gn_b, conv_w, conv_b, rg_wx, rg_bx, rg_wa, rg_ba, rg_lambda, w_out,
           ln1_g, ln1_b, router_w, router_b, exp_w1, exp_b1, exp_w2, exp_b2, ln2_g, ln2_b):
    B, T, D = x.shape
    cos_t, sin_t = _rope_tables(T)
    tabs = _ret_tables()
    x2 = x.reshape(B * T, D)
    for l in range(w_in.shape[0]):
        x2 = _layer(x2, B, T, cos_t, sin_t, tabs, w_in[l], ret_gn_g[l], ret_gn_b[l], conv_w[l], conv_b[l],
                    rg_wx[l], rg_bx[l], rg_wa[l], rg_ba[l], rg_lambda[l], w_out[l], ln1_g[l], ln1_b[l],
                    router_w[l], router_b[l], exp_w1, exp_b1[l], exp_w2, exp_b2[l], ln2_g[l], ln2_b[l], l)
    return x2.reshape(B, T, D)
```

```python
import functools

import numpy as np
import jax
import jax.numpy as jnp
from jax import lax
from jax.experimental import pallas as pl
from jax.experimental.pallas import tpu as pltpu

F32 = jnp.float32
BF16 = jnp.bfloat16
I32 = jnp.int32

D_MODEL = 1024
DEPTH = 2
HEAD_DIM = 64
GROUP_WIDTH = 256
GROUP_HEADS = 4
ROPE_THETA = 10000.0
MOBA_Q = 256
MOBA_BLOCK = 256
MOBA_TOPK = 3
MOBA_MAX_BLOCKS = 16
RET_CHUNK = 128
RG_CONV = 4
RG_C = 8.0
IDX_HEADS = 8
IDX_DIM = 64
IDX_SCALE = (IDX_HEADS ** -0.5) * (IDX_DIM ** -0.5)
DSA_TOPK = 256
DSA_Q = 256
DSA_KEY_BLOCK = 512
DSA_ATT_BLOCK = 512
N_EXPERTS = 32
TOP_K = 4
D_FF = 1024
SWIGLU_LIMIT = 7.0
SWIGLU_ALPHA = 1.702
ALPHA = (2 * DEPTH) ** 0.25
LN_EPS = 1e-5
IN_WIDTH = 12 * GROUP_WIDTH + IDX_HEADS * IDX_DIM + IDX_DIM + IDX_HEADS
IN_PAD = 15 * GROUP_WIDTH

LANES = 128
SUBLANES = 8
HEAD_SHIFT = HEAD_DIM.bit_length() - 1
KEY_MAGNITUDE_BITS = 0x7FFFFFFF
BISECT_MAX_STEPS = 36
NEG = -3.0e38
M_FLOOR = -1.5e38
BIG = 3.0e38
VMEM_LIMIT = 56 * 1024 * 1024
FOLD_ROWS = 32

PROJ_TM = 256
OUT_TM = 512
RG_TC = 256
MOE_TM = 512
DSP_TM = 512
CMB_TM = 256


def _cparams(ndims):
    return pltpu.CompilerParams(dimension_semantics=("arbitrary",) * ndims,
                                vmem_limit_bytes=VMEM_LIMIT)


def _dot(a, b, precision=None):
    return jnp.dot(a, b, preferred_element_type=F32, precision=precision)


def _dot_nt(a, b, precision=None):
    return lax.dot_general(a, b, (((1,), (1,)), ((), ())), preferred_element_type=F32,
                           precision=precision)


def _dot_tn(a, b):
    return lax.dot_general(a, b, (((0,), (0,)), ((), ())), preferred_element_type=F32)


def _split3(x):
    hi = x.astype(BF16)
    r1 = x - hi.astype(F32)
    mid = r1.astype(BF16)
    lo = (r1 - mid.astype(F32)).astype(BF16)
    return lo, mid, hi


def _head_stack(q):
    head = lax.shift_right_logical(lax.broadcasted_iota(I32, q.shape, 1), HEAD_SHIFT)
    qf = q.astype(F32)
    return jnp.concatenate([jnp.where(head == h, qf, 0.0) for h in range(GROUP_HEADS)],
                           axis=0).astype(q.dtype)


def _head_unstack(s, rows):
    head = lax.shift_right_logical(lax.broadcasted_iota(I32, (rows, GROUP_WIDTH), 1), HEAD_SHIFT)
    out = jnp.zeros((rows, GROUP_WIDTH), F32)
    for h in range(GROUP_HEADS):
        out = out + jnp.where(head == h, s[h * rows:(h + 1) * rows], 0.0)
    return out


def _fold_rows(x, op):
    return op(x.reshape(x.shape[0] // FOLD_ROWS, FOLD_ROWS, x.shape[1]), axis=0)


def _layer_norm_rows(y, g, b):
    mu = jnp.mean(y, axis=-1, keepdims=True)
    yc = y - mu
    var = jnp.mean(yc * yc, axis=-1, keepdims=True)
    return yc * lax.rsqrt(var + LN_EPS) * g + b


def _proj_kernel(x_ref, w_ref, cos_ref, sin_ref,
                 aq, ak, av, rq, rk, rv, rg, cx, cg, dq, dk, dv, dqi, dki, dwt):
    xb = x_ref[...].astype(BF16)
    cos = cos_ref[...]
    sin = sin_ref[...]
    first_half = (lax.broadcasted_iota(I32, cos.shape, 1) & (HEAD_DIM - 1)) < (HEAD_DIM // 2)

    def seg(i):
        return _dot(xb, w_ref[:, i * GROUP_WIDTH:(i + 1) * GROUP_WIDTH])

    def rope(p):
        rot = jnp.where(first_half, pltpu.roll(p, GROUP_WIDTH - HEAD_DIM // 2, 1),
                        pltpu.roll(p, HEAD_DIM // 2, 1))
        return p * cos + rot * sin

    aq[...] = rope(seg(0)).astype(BF16)
    ak[...] = rope(seg(1)).astype(BF16)
    av[...] = seg(2).astype(BF16)
    rq[...] = rope(seg(3)).astype(BF16)
    rk[...] = (rope(seg(4)) * (HEAD_DIM ** -0.5)).astype(BF16)
    rv[...] = seg(5).astype(BF16)
    rg[...] = seg(6)
    cx[...] = seg(7)
    cg[...] = seg(8)
    dq[...] = rope(seg(9)).astype(BF16)
    dk[...] = rope(seg(10)).astype(BF16)
    dv[...] = seg(11).astype(BF16)
    dqi[:, 0:GROUP_WIDTH] = rope(seg(12)).astype(BF16)
    dqi[:, GROUP_WIDTH:2 * GROUP_WIDTH] = rope(seg(13)).astype(BF16)
    last = seg(14)
    dki[...] = rope(last)[:, 0:IDX_DIM].astype(BF16)
    dwt[...] = last[:, LANES:2 * LANES].T[0:IDX_HEADS, :]


def _proj(x2, w_pad, cos_t, sin_t, T):
    N = x2.shape[0]
    tm = PROJ_TM
    tpb = T // tm
    row = lambda w: pl.BlockSpec((tm, w), lambda i: (i, 0))
    tab = pl.BlockSpec((tm, GROUP_WIDTH), lambda i: (i % tpb, 0))
    widths = [256] * 12 + [512, IDX_DIM]
    dtypes = [BF16, BF16, BF16, BF16, BF16, BF16, F32, F32, F32, BF16, BF16, BF16, BF16, BF16]
    return pl.pallas_call(
        _proj_kernel,
        grid=(N // tm,),
        in_specs=[row(D_MODEL), pl.BlockSpec((D_MODEL, IN_PAD), lambda i: (0, 0)), tab, tab],
        out_specs=[row(w) for w in widths] + [pl.BlockSpec((IDX_HEADS, tm), lambda i: (0, i))],
        out_shape=[jax.ShapeDtypeStruct((N, w), d) for w, d in zip(widths, dtypes)]
        + [jax.ShapeDtypeStruct((IDX_HEADS, N), F32)],
        compiler_params=_cparams(1),
        name="proj_rope",
    )(x2, w_pad, cos_t, sin_t)


def _moba_kernel(q_ref, k_ref, v_ref, o_ref, kmean_ref, sel_ref, qk_ref, m_ref, l_ref, acc_ref, *, n_blocks):
    j = pl.program_id(1)
    R = MOBA_Q
    SR = GROUP_HEADS * R
    KB = MOBA_BLOCK

    @pl.when(j == 0)
    def _():
        kmean_ref[...] = jnp.zeros_like(kmean_ref)
        for n in range(n_blocks):
            kb = k_ref[0, n * KB:(n + 1) * KB, :].astype(F32)
            kmean_ref[n:n + 1, :] = jnp.mean(kb, axis=0, keepdims=True)

    own = j // (KB // R)
    q_raw = _head_stack(q_ref[0])
    q_stack = (q_raw.astype(F32) * (HEAD_DIM ** -0.5)).astype(BF16)

    gate = sum(_dot_nt(part, q_raw) for part in _split3(kmean_ref[...]))
    blk = lax.broadcasted_iota(I32, gate.shape, 0)
    past = blk < own
    g = jnp.where(past, gate, -jnp.inf)
    sel = jnp.zeros(gate.shape, F32)
    for _ in range(MOBA_TOPK):
        mx = jnp.max(g, axis=0, keepdims=True)
        first = jnp.min(jnp.where(g == mx, blk, MOBA_MAX_BLOCKS), axis=0, keepdims=True)
        pick = blk == first
        sel = jnp.where(pick & past, 1.0, sel)
        g = jnp.where(pick, -jnp.inf, g)
    sel_ref[...] = sel

    m_ref[...] = jnp.full(m_ref.shape, M_FLOOR, F32)
    l_ref[...] = jnp.zeros(l_ref.shape, F32)
    acc_ref[...] = jnp.zeros(acc_ref.shape, F32)

    def softmax_pv(s_raw, vb, bias):
        s = s_raw + bias
        m_old = m_ref[...]
        m_new = jnp.maximum(m_old, jnp.max(s, axis=0, keepdims=True))
        alpha = jnp.exp(m_old - m_new)
        p = jnp.exp(s - m_new)
        l_ref[...] = alpha * l_ref[...] + jnp.sum(p, axis=0, keepdims=True)
        acc_ref[...] = alpha * acc_ref[...] + _dot_tn(vb, p.astype(BF16))
        m_ref[...] = m_new

    def block_bias(n):
        return jnp.broadcast_to(jnp.where(sel_ref[pl.ds(n, 1), :] > 0.5, 0.0, NEG), (KB, SR))

    def qk(first_block, rows):
        st = pl.multiple_of(first_block * KB, KB)
        return _dot_nt(k_ref[0, pl.ds(st, rows), :], q_stack)

    def attend_span(first_block, bias, s_raw=None):
        st = pl.multiple_of(first_block * KB, KB)
        rows = bias.shape[0]
        s_raw = qk(first_block, rows) if s_raw is None else s_raw
        softmax_pv(s_raw, v_ref[0, pl.ds(st, rows), :], bias)

    keypos = own * KB + lax.broadcasted_iota(I32, (KB, SR), 0)
    qpos = j * R + (lax.broadcasted_iota(I32, (KB, SR), 1) & (R - 1))
    causal = jnp.where(keypos <= qpos, 0.0, NEG)
    odd = (own & 1) == 1

    @pl.when(odd)
    def _():
        attend_span(own - 1, jnp.concatenate([block_bias(own - 1), causal], axis=0))

    @pl.when(jnp.logical_not(odd))
    def _():
        attend_span(own, causal)

    n_pairs = own // 2

    @pl.when(n_pairs > 0)
    def _():
        qk_ref[...] = qk(0, 2 * KB)

    def body(pair, c):
        s_raw = qk_ref[...]
        qk_ref[...] = qk(2 * jnp.minimum(pair + 1, n_pairs - 1), 2 * KB)
        attend_span(2 * pair, jnp.concatenate([block_bias(2 * pair), block_bias(2 * pair + 1)], axis=0), s_raw)
        return c

    lax.fori_loop(0, n_pairs, body, 0)
    o_ref[0] = _head_unstack((acc_ref[...] / l_ref[...]).T, R).astype(o_ref.dtype)


def _moba(q, k, v):
    B, T, _ = q.shape
    n_blocks = T // MOBA_BLOCK
    assert T % MOBA_BLOCK == 0 and n_blocks <= MOBA_MAX_BLOCKS
    SR = GROUP_HEADS * MOBA_Q
    return pl.pallas_call(
        functools.partial(_moba_kernel, n_blocks=n_blocks),
        grid=(B, T // MOBA_Q),
        in_specs=[pl.BlockSpec((1, MOBA_Q, GROUP_WIDTH), lambda b, j: (b, j, 0)),
                  pl.BlockSpec((1, T, GROUP_WIDTH), lambda b, j: (b, 0, 0)),
                  pl.BlockSpec((1, T, GROUP_WIDTH), lambda b, j: (b, 0, 0))],
        out_specs=pl.BlockSpec((1, MOBA_Q, GROUP_WIDTH), lambda b, j: (b, j, 0)),
        out_shape=jax.ShapeDtypeStruct((B, T, GROUP_WIDTH), BF16),
        scratch_shapes=[pltpu.VMEM((MOBA_MAX_BLOCKS, GROUP_WIDTH), F32),
                        pltpu.VMEM((MOBA_MAX_BLOCKS, SR), F32),
                        pltpu.VMEM((2 * MOBA_BLOCK, SR), F32),
                        pltpu.VMEM((1, SR), F32), pltpu.VMEM((1, SR), F32),
                        pltpu.VMEM((GROUP_WIDTH, SR), F32)],
        compiler_params=_cparams(2),
        name="moba_attention",
    )(q, k, v)


def _ret_kernel(q_ref, k_ref, v_ref, g_ref, dmask_ref, xi_ref, zeta_ref, gdec_ref, bd_ref, avg_ref,
                gng_ref, gnb_ref, o_ref, r_ref):
    j = pl.program_id(0)

    @pl.when(j == 0)
    def _():
        r_ref[...] = jnp.zeros_like(r_ref)

    C = RET_CHUNK
    for b in range(q_ref.shape[0]):
        q = q_ref[b]
        k = k_ref[b]
        v = v_ref[b]
        q_stack = _head_stack(q)
        inner = _dot_nt(q_stack, k) * dmask_ref[...]
        o = _head_unstack(_dot(inner.astype(BF16), v), C)
        R = r_ref[b]
        o = o + _dot(q, R.astype(BF16)) * xi_ref[...]
        kz = (k.astype(F32) * zeta_ref[...]).astype(BF16)
        r_ref[b] = gdec_ref[...] * R + bd_ref[...] * _dot_tn(kz, v)

        avg = avg_ref[...]
        mu = sum(_dot(part, avg) for part in _split3(o))
        oc = o - mu
        var = sum(_dot(part, avg) for part in _split3(oc * oc))
        y = oc * lax.rsqrt(var + LN_EPS) * gng_ref[...] + gnb_ref[...]
        gte = g_ref[b]
        o_ref[b] = (y * (gte * jax.nn.sigmoid(gte))).astype(o_ref.dtype)


def _ret_tables():
    H, C, d = GROUP_HEADS, RET_CHUNK, HEAD_DIM
    log_g = np.log(1.0 - 2.0 ** (-5.0 - np.arange(H, dtype=np.float64)))
    n = np.arange(C, dtype=np.float64)
    diff = n[:, None] - n[None, :]
    dmask = np.where(diff >= 0, np.exp(log_g[:, None, None] * np.maximum(diff, 0.0)), 0.0)
    xi = np.exp(log_g[:, None] * (n + 1.0))
    zeta = np.exp(log_g[:, None] * (C - 1.0 - n))
    g_chunk = np.exp(log_g * C)
    head = np.arange(GROUP_WIDTH) // d
    bd = (head[:, None] == head[None, :]).astype(np.float64)
    to32 = lambda a: jnp.asarray(a, dtype=F32)
    return dict(dmask=to32(dmask.reshape(H * C, C)), xi=to32(xi.T[:, head]), zeta=to32(zeta.T[:, head]),
                gdec=to32(bd * g_chunk[head][:, None]), bd=to32(bd),
                avg=jnp.asarray(bd / d, dtype=BF16))


def _retention(rq, rk, rv, rg, gn_g, gn_b, tabs):
    B, T, _ = rq.shape
    C = RET_CHUNK
    blk = pl.BlockSpec((B, C, GROUP_WIDTH), lambda j: (0, j, 0))
    const = lambda a: pl.BlockSpec(a.shape, lambda j: (0,) * a.ndim)
    consts = [tabs["dmask"], tabs["xi"], tabs["zeta"], tabs["gdec"], tabs["bd"], tabs["avg"], gn_g, gn_b]
    return pl.pallas_call(
        _ret_kernel,
        grid=(T // C,),
        in_specs=[blk, blk, blk, blk] + [const(a) for a in consts],
        out_specs=blk,
        out_shape=jax.ShapeDtypeStruct((B, T, GROUP_WIDTH), BF16),
        scratch_shapes=[pltpu.VMEM((B, GROUP_WIDTH, GROUP_WIDTH), F32)],
        compiler_params=_cparams(1),
        name="retention",
    )(rq, rk, rv, rg, *consts)


def _rglru_kernel(x_ref, g_ref, cw_ref, cb_ref, wx_ref, bx_ref, wa_ref, ba_ref, lam_ref, o_ref,
                  xbuf, h_ref):
    j = pl.program_id(1)
    tc = RG_TC

    @pl.when(j == 0)
    def _():
        xbuf[0:SUBLANES, :] = jnp.zeros((SUBLANES, GROUP_WIDTH), F32)
        h_ref[...] = jnp.zeros_like(h_ref)

    xbuf[SUBLANES:SUBLANES + tc, :] = x_ref[0]
    xc = cb_ref[...] + cw_ref[RG_CONV - 1:RG_CONV, :] * xbuf[SUBLANES:SUBLANES + tc, :]
    for i in range(RG_CONV - 1):
        off = SUBLANES - (RG_CONV - 1) + i
        xc = xc + cw_ref[i:i + 1, :] * xbuf[off:off + tc, :]
    xbuf[0:SUBLANES, :] = xbuf[tc:tc + SUBLANES, :]

    xcb = xc.astype(BF16)
    gate_x = jax.nn.sigmoid(_dot(xcb, wx_ref[...]) + bx_ref[...])
    gate_a = jax.nn.sigmoid(_dot(xcb, wa_ref[...]) + ba_ref[...])
    lam = lam_ref[...]
    softplus_neg = jnp.maximum(-lam, 0.0) + jnp.log1p(jnp.exp(-jnp.abs(lam)))
    log_a = -RG_C * gate_a * softplus_neg
    a = jnp.exp(log_a)
    th = jnp.tanh(log_a)
    b = jnp.sqrt(-2.0 * th / (1.0 - th)) * (gate_x * xc)

    row = lax.broadcasted_iota(I32, (tc, GROUP_WIDTH), 0)
    d = 1
    while d < tc:
        keep = row >= d
        a_sh = jnp.where(keep, pltpu.roll(a, d, 0), 1.0)
        b_sh = jnp.where(keep, pltpu.roll(b, d, 0), 0.0)
        b = a * b_sh + b
        a = a * a_sh
        d *= 2
    h = b + a * h_ref[...]
    h_ref[...] = h[tc - 1:tc, :]

    xg = g_ref[0]
    gelu = 0.5 * xg * (1.0 + jnp.tanh(np.sqrt(2.0 / np.pi) * (xg + 0.044715 * xg * xg * xg)))
    o_ref[0] = (h * gelu).astype(o_ref.dtype)


def _block_diag(w):
    n, c, _ = w.shape
    eye = jnp.eye(n, dtype=w.dtype)
    return (eye[:, None, :, None] * w[:, :, None, :]).reshape(n * c, n * c)


def _rglru(cx, cg, conv_w, conv_b, wx, bx, wa, ba, lam):
    B, T, _ = cx.shape
    tc = RG_TC
    blk = pl.BlockSpec((1, tc, GROUP_WIDTH), lambda b, j: (b, j, 0))
    const = lambda a: pl.BlockSpec(a.shape, lambda b, j: (0,) * a.ndim)
    consts = [conv_w, conv_b, wx, bx, wa, ba, lam]
    return pl.pallas_call(
        _rglru_kernel,
        grid=(B, T // tc),
        in_specs=[blk, blk] + [const(a) for a in consts],
        out_specs=blk,
        out_shape=jax.ShapeDtypeStruct((B, T, GROUP_WIDTH), BF16),
        scratch_shapes=[pltpu.VMEM((tc + SUBLANES, GROUP_WIDTH), F32), pltpu.VMEM((1, GROUP_WIDTH), F32)],
        compiler_params=_cparams(2),
        name="rg_lru",
    )(cx, cg, *consts)


def _dsa_kernel(q_ref, k_ref, v_ref, qi_ref, ki_ref, wt_ref, o_ref,
                sc_ref, jcut_ref, qk_ref, m_ref, l_ref, acc_ref, *, n_sel, n_keys):
    j = pl.program_id(1)
    R = DSA_Q
    KB = DSA_KEY_BLOCK
    SR = GROUP_HEADS * R
    nkb = (j * R + R + KB - 1) // KB
    nsel = float(n_sel)

    rowk = lax.broadcasted_iota(I32, (KB, R), 0)
    qpos = j * R + lax.broadcasted_iota(I32, (KB, R), 1)

    qi = qi_ref[0]
    qi_stack = jnp.concatenate([qi[:, h * IDX_DIM:(h + 1) * IDX_DIM] for h in range(IDX_HEADS)], axis=0)
    wt = wt_ref[...]

    def score_body(kb, c):
        st = pl.multiple_of(kb * KB, KB)
        rel = _dot_nt(ki_ref[0, pl.ds(st, KB), :], qi_stack)
        sc = wt[0:1, :] * jnp.maximum(rel[:, 0:R], 0.0)
        for h in range(1, IDX_HEADS):
            sc = sc + wt[h:h + 1, :] * jnp.maximum(rel[:, h * R:(h + 1) * R], 0.0)
        sc_ref[pl.ds(st, KB), :] = jnp.where(st + rowk <= qpos, sc * IDX_SCALE, NEG)
        return c

    lax.fori_loop(0, nkb, score_body, 0)

    def blocks(fn, init):
        def body(kb, c):
            st = pl.multiple_of(kb * KB, KB)
            return fn(st, sc_ref[pl.ds(st, KB), :], c)
        return lax.fori_loop(0, nkb, body, init)

    def survey(st, s, c):
        mn, mx, gt, ge = c
        return (jnp.minimum(mn, _fold_rows(jnp.where(s > 0.5 * NEG, s, BIG), jnp.min)),
                jnp.maximum(mx, _fold_rows(s, jnp.max)),
                gt + _fold_rows(jnp.where(s > 0.0, 1.0, 0.0), jnp.sum),
                ge + _fold_rows(jnp.where(s >= 0.0, 1.0, 0.0), jnp.sum))

    zeros_fold = jnp.zeros((FOLD_ROWS, R), F32)
    mn8, mx8, gt8, ge8 = blocks(survey, (jnp.full((FOLD_ROWS, R), BIG, F32), jnp.full((FOLD_ROWS, R), NEG, F32),
                                         zeros_fold, zeros_fold))
    mn = jnp.min(mn8, axis=0, keepdims=True)
    mx = jnp.max(mx8, axis=0, keepdims=True)
    c_gt0 = jnp.sum(gt8, axis=0, keepdims=True)
    c_ge0 = jnp.sum(ge8, axis=0, keepdims=True)

    def count_ge(th):
        acc = blocks(lambda st, s, c: c + _fold_rows(jnp.where(s >= th, 1.0, 0.0), jnp.sum),
                     jnp.zeros((FOLD_ROWS, R), F32))
        return jnp.sum(acc, axis=0, keepdims=True)

    n_adm = (j * R + 1 + lax.broadcasted_iota(I32, (1, R), 1)).astype(F32)
    need = n_adm > nsel
    lo0 = jnp.where(need, mn, 0.5 * NEG)
    above = need & (c_gt0 >= nsel)
    below = need & (c_ge0 < nsel)
    at_zero = need & jnp.logical_not(above | below)
    cgt0 = jnp.where(below, c_ge0, jnp.where(at_zero, c_gt0, 0.0))
    act0 = jnp.where(above | below, 1.0, 0.0)
    tie0 = jnp.where(at_zero, 1.0, 0.0)

    def to_key(f):
        b = lax.bitcast_convert_type(f, I32)
        return b ^ (lax.shift_right_arithmetic(b, 31) & KEY_MAGNITUDE_BITS)

    def from_key(kk):
        return lax.bitcast_convert_type(kk ^ (lax.shift_right_arithmetic(kk, 31) & KEY_MAGNITUDE_BITS), F32)

    def bis_step(klo, khi, cgt, act, tie):
        on = act > 0.0
        kmid = lax.shift_right_arithmetic(klo, 1) + lax.shift_right_arithmetic(khi, 1) + (klo & khi & 1)
        stuck = kmid == klo
        cnt = count_ge(from_key(kmid))
        go = on & jnp.logical_not(stuck)
        up = go & (cnt >= nsel)
        dn = go & (cnt < nsel)
        return (jnp.where(up, kmid, klo), jnp.where(dn, kmid, khi), jnp.where(dn, cnt, cgt),
                jnp.where(go & (cnt != nsel), 1.0, 0.0), jnp.where(on & stuck, 1.0, tie))

    def bis_cond(c):
        return (c[1] > 0.0) & (c[0] < BISECT_MAX_STEPS)

    def bis_body(c):
        flag = jnp.max(c[5])
        st = bis_step(*bis_step(*c[2:]))
        return (c[0] + 2, flag) + st

    klo0 = jnp.where(above, 1, jnp.where(at_zero, 0, to_key(lo0)))
    khi0 = jnp.where(below, 0, to_key(mx) + 1)
    res = lax.while_loop(bis_cond, bis_body, (jnp.int32(0), jnp.max(act0), klo0, khi0, cgt0, act0, tie0))
    lo, cgt, tie = from_key(res[2]), res[4], res[6]

    jcut_ref[...] = jnp.full((1, R), float(n_keys), F32)

    @pl.when(jnp.max(tie) > 0.0)
    def _():
        want = nsel - cgt
        tied = tie > 0.0

        def jb(it, c):
            a, b = c
            mid = jnp.floor((a + b) * 0.5)
            hit8 = blocks(lambda st, s, cc: cc + _fold_rows(
                jnp.where((s == lo) & ((st + rowk).astype(F32) <= mid), 1.0, 0.0), jnp.sum),
                jnp.zeros((FOLD_ROWS, R), F32))
            ok = jnp.sum(hit8, axis=0, keepdims=True) >= want
            return jnp.where(ok, a, mid), jnp.where(ok, mid, b)

        _, b = lax.fori_loop(0, int(np.ceil(np.log2(n_keys))) + 1, jb,
                             (jnp.full((1, R), -1.0, F32), jnp.full((1, R), float(n_keys - 1), F32)))
        jcut_ref[...] = jnp.where(tied, b, float(n_keys))

    q_stack = _head_stack((q_ref[0].astype(F32) * (HEAD_DIM ** -0.5)).astype(BF16))
    m_ref[...] = jnp.full(m_ref.shape, M_FLOOR, F32)
    l_ref[...] = jnp.zeros(l_ref.shape, F32)
    acc_ref[...] = jnp.zeros(acc_ref.shape, F32)
    jcut = jcut_ref[...]
    AB = DSA_ATT_BLOCK
    nab = (j * R + R + AB - 1) // AB
    rowf = lax.broadcasted_iota(I32, (AB, R), 0).astype(F32)

    def qk(kb):
        st = pl.multiple_of(kb * AB, AB)
        return _dot_nt(k_ref[0, pl.ds(st, AB), :], q_stack)

    qk_ref[...] = qk(0)

    def att_body(kb, c):
        st = pl.multiple_of(kb * AB, AB)
        s_raw = qk_ref[...]
        qk_ref[...] = qk(jnp.minimum(kb + 1, nab - 1))
        sc = sc_ref[pl.ds(st, AB), :]
        keep = (sc > lo) | ((sc == lo) & (rowf <= jcut - st.astype(F32)))
        bias = jnp.where(keep, 0.0, NEG)
        s = s_raw + jnp.concatenate([bias] * GROUP_HEADS, axis=1)
        m_old = m_ref[...]
        m_new = jnp.maximum(m_old, jnp.max(s, axis=0, keepdims=True))
        alpha = jnp.exp(m_old - m_new)
        p = jnp.exp(s - m_new)
        l_ref[...] = alpha * l_ref[...] + jnp.sum(p, axis=0, keepdims=True)
        acc_ref[...] = alpha * acc_ref[...] + _dot_tn(v_ref[0, pl.ds(st, AB), :], p.astype(BF16))
        m_ref[...] = m_new
        return c

    lax.fori_loop(0, nab, att_body, 0)
    o_ref[0] = _head_unstack((acc_ref[...] / l_ref[...]).T, R).astype(o_ref.dtype)


def _dsa(q, k, v, qi, ki, wt):
    B, T, _ = q.shape
    KB = DSA_KEY_BLOCK
    assert T % KB == 0
    n_sel = min(DSA_TOPK, T // 4)
    R = DSA_Q
    assert T % R == 0
    SR = GROUP_HEADS * R
    nq = T // R
    qblk = lambda wd: pl.BlockSpec((1, R, wd), lambda b, j: (b, j, 0))
    full = lambda wd: pl.BlockSpec((1, T, wd), lambda b, j: (b, 0, 0))
    return pl.pallas_call(
        functools.partial(_dsa_kernel, n_sel=n_sel, n_keys=T),
        grid=(B, nq),
        in_specs=[qblk(GROUP_WIDTH), full(GROUP_WIDTH), full(GROUP_WIDTH),
                  qblk(IDX_HEADS * IDX_DIM), full(IDX_DIM),
                  pl.BlockSpec((IDX_HEADS, R), lambda b, j: (0, b * nq + j))],
        out_specs=qblk(GROUP_WIDTH),
        out_shape=jax.ShapeDtypeStruct((B, T, GROUP_WIDTH), BF16),
        scratch_shapes=[pltpu.VMEM((T, R), F32), pltpu.VMEM((1, R), F32), pltpu.VMEM((DSA_ATT_BLOCK, SR), F32),
                        pltpu.VMEM((1, SR), F32), pltpu.VMEM((1, SR), F32),
                        pltpu.VMEM((GROUP_WIDTH, SR), F32)],
        compiler_params=_cparams(2),
        name="dsa_attention",
    )(q, k, v, qi, ki, wt)


def _outproj_kernel(oa, orr, oc, od, x_ref, w_ref, g_ref, b_ref, rwt_ref, rb_ref, utri_ref,
                    x1_ref, ti_ref, tg_ref, rk_ref, cnt_ref, run_ref):
    GW = GROUP_WIDTH

    @pl.when(pl.program_id(0) == 0)
    def _():
        run_ref[...] = jnp.zeros_like(run_ref)

    acc = _dot(oa[...], w_ref[0:GW, :])
    acc = acc + _dot(orr[...], w_ref[GW:2 * GW, :])
    acc = acc + _dot(oc[...], w_ref[2 * GW:3 * GW, :])
    acc = acc + _dot(od[...], w_ref[3 * GW:4 * GW, :])
    x1 = _layer_norm_rows(ALPHA * x_ref[...] + acc, g_ref[...], b_ref[...])
    x1_ref[...] = x1

    logits = _dot_nt(rwt_ref[...], x1.astype(BF16)) + rb_ref[...]
    row = lax.broadcasted_iota(I32, logits.shape, 0)
    krow = lax.broadcasted_iota(I32, ti_ref.shape, 0)
    g = logits
    ti = jnp.zeros(ti_ref.shape, I32)
    tv = jnp.zeros(tg_ref.shape, F32)
    picks = []
    for kk in range(TOP_K):
        mx = jnp.max(g, axis=0, keepdims=True)
        first = jnp.min(jnp.where(g == mx, row, N_EXPERTS), axis=0, keepdims=True)
        ti = jnp.where(krow == kk, first, ti)
        tv = jnp.where(krow == kk, mx, tv)
        picks.append(row == first)
        g = jnp.where(picks[-1], -jnp.inf, g)
    e = jnp.exp(tv - jnp.max(tv, axis=0, keepdims=True))
    ti_ref[...] = ti
    tg_ref[...] = e / jnp.sum(e, axis=0, keepdims=True)

    sel = jnp.where(picks[0] | picks[1] | picks[2] | picks[3], 1.0, 0.0)
    before = run_ref[...] + _dot(sel.astype(BF16), utri_ref[...])
    rk = jnp.zeros(rk_ref.shape, F32)
    for kk in range(TOP_K):
        rk = jnp.where(krow == kk, jnp.sum(jnp.where(picks[kk], before, 0.0), axis=0, keepdims=True), rk)
    rk_ref[...] = rk.astype(I32)
    run_ref[...] = run_ref[...] + jnp.sum(sel, axis=1, keepdims=True)
    cnt_ref[...] = run_ref[...]


def _outproj(oa, orr, oc, od, x2, w_out, g, b, rwt, rb):
    N = x2.shape[0]
    tm = OUT_TM
    row = lambda w: pl.BlockSpec((tm, w), lambda i: (i, 0))
    picks = pl.BlockSpec((TOP_K, tm), lambda i: (0, i))
    const = lambda a: pl.BlockSpec(a.shape, lambda i: (0,) * a.ndim)
    utri = jnp.asarray(np.triu(np.ones((tm, tm), np.float32), 1), dtype=BF16)
    return pl.pallas_call(
        _outproj_kernel,
        grid=(N // tm,),
        in_specs=[row(GROUP_WIDTH)] * 4 + [row(D_MODEL), const(w_out), const(g), const(b), const(rwt), const(rb),
                                           const(utri)],
        out_specs=[row(D_MODEL), picks, picks, picks, pl.BlockSpec((N_EXPERTS, 1), lambda i: (0, 0))],
        out_shape=[jax.ShapeDtypeStruct((N, D_MODEL), F32), jax.ShapeDtypeStruct((TOP_K, N), I32),
                   jax.ShapeDtypeStruct((TOP_K, N), F32), jax.ShapeDtypeStruct((TOP_K, N), I32),
                   jax.ShapeDtypeStruct((N_EXPERTS, 1), F32)],
        scratch_shapes=[pltpu.VMEM((N_EXPERTS, 1), F32)],
        compiler_params=_cparams(1),
        name="outproj_ln_router",
    )(oa, orr, oc, od, x2, w_out, g, b, rwt, rb, utri)


def _dispatch_kernel(tv_ref, pos_ref, x_ref, xs_hbm, xbuf, sem, zsem, *, n_tiles):
    i = pl.program_id(0)
    n = pl.num_programs(0)
    tm = DSP_TM
    par = i % 2

    @pl.when(i == 0)
    def _():
        xbuf[1] = jnp.zeros((tm, D_MODEL), F32)

        def fill(t, c):
            @pl.when(tv_ref[t] < MOE_TM)
            def _():
                pltpu.make_async_copy(xbuf.at[1], xs_hbm.at[pl.ds(pl.multiple_of(t * MOE_TM, MOE_TM), MOE_TM)],
                                      zsem).start()
            return c

        def drain(t, c):
            @pl.when(tv_ref[t] < MOE_TM)
            def _():
                pltpu.make_async_copy(xbuf.at[1], xs_hbm.at[pl.ds(0, MOE_TM)], zsem).wait()
            return c

        lax.fori_loop(0, n_tiles, fill, 0)
        lax.fori_loop(0, n_tiles, drain, 0)

    def wait_step(p):
        for _ in range(TOP_K):
            pltpu.make_async_copy(xbuf.at[p], xbuf.at[p], sem.at[p]).wait()

    @pl.when(i >= 2)
    def _():
        wait_step(par)

    xbuf[par] = x_ref[...]

    def body(r, c):
        for kk in range(TOP_K):
            pltpu.make_async_copy(xbuf.at[par, pl.ds(r, 1)], xs_hbm.at[pl.ds(pos_ref[r * TOP_K + kk], 1)],
                                  sem.at[par]).start()
        return c

    lax.fori_loop(0, tm, body, 0, unroll=4)

    @pl.when(i == n - 1)
    def _():
        wait_step(1 - par)
        wait_step(par)


def _dispatch(pos, x1, tile_valid):
    N = x1.shape[0]
    tm = DSP_TM
    n_tiles = tile_valid.shape[0]
    assert N // tm >= 2 and tm == MOE_TM
    grid_spec = pltpu.PrefetchScalarGridSpec(
        num_scalar_prefetch=1,
        grid=(N // tm,),
        in_specs=[pl.BlockSpec((tm * TOP_K,), lambda i, tv: (i,), memory_space=pltpu.SMEM),
                  pl.BlockSpec((tm, D_MODEL), lambda i, tv: (i, 0))],
        out_specs=pl.BlockSpec(memory_space=pl.ANY),
        scratch_shapes=[pltpu.VMEM((2, tm, D_MODEL), F32), pltpu.SemaphoreType.DMA((2,)),
                        pltpu.SemaphoreType.DMA(())],
    )
    return pl.pallas_call(
        functools.partial(_dispatch_kernel, n_tiles=n_tiles),
        grid_spec=grid_spec,
        out_shape=jax.ShapeDtypeStruct((n_tiles * MOE_TM, D_MODEL), F32),
        compiler_params=_cparams(1),
        name="moe_dispatch",
    )(tile_valid, pos, x1)


def _moe_kernel(te_ref, nv_ref, x_ref, w1_ref, b1_ref, w2_ref, b2_ref, y_ref, w1b, w2b):
    i = pl.program_id(0)
    tm = MOE_TM
    n_valid = nv_ref[i]

    @pl.when((i == 0) | (te_ref[i] != te_ref[jnp.maximum(i - 1, 0)]))
    def _():
        step = LANES
        for c in range(D_MODEL // step):
            w1b[c * step:(c + 1) * step, :] = w1_ref[0, 0, c * step:(c + 1) * step, :].astype(BF16)
        for c in range(D_FF // step):
            w2b[c * step:(c + 1) * step, :] = w2_ref[0, 0, c * step:(c + 1) * step, :].astype(BF16)

    @pl.when(n_valid > 0)
    def _():
        h = _dot(x_ref[...].astype(BF16), w1b[...]) + b1_ref[0]
        glu_in = jnp.minimum(h[:, :D_FF], SWIGLU_LIMIT)
        up = jnp.clip(h[:, D_FF:], -SWIGLU_LIMIT, SWIGLU_LIMIT)
        glu = glu_in * jax.nn.sigmoid(SWIGLU_ALPHA * glu_in)
        y_ref[...] = _dot(((up + 1.0) * glu).astype(BF16), w2b[...]) + b2_ref[0]

    @pl.when(n_valid == 0)
    def _():
        y_ref[...] = jnp.zeros_like(y_ref)


def _moe_experts(xs, tile_expert, tile_valid, w1, b1, w2, b2, layer):
    tm = MOE_TM
    n_tiles = tile_expert.shape[0]
    grid_spec = pltpu.PrefetchScalarGridSpec(
        num_scalar_prefetch=2,
        grid=(n_tiles,),
        in_specs=[
            pl.BlockSpec((tm, D_MODEL), lambda i, te, nv: (i, 0)),
            pl.BlockSpec((1, 1, D_MODEL, 2 * D_FF), lambda i, te, nv: (layer, te[i], 0, 0)),
            pl.BlockSpec((1, 1, 2 * D_FF), lambda i, te, nv: (te[i], 0, 0)),
            pl.BlockSpec((1, 1, D_FF, D_MODEL), lambda i, te, nv: (layer, te[i], 0, 0)),
            pl.BlockSpec((1, 1, D_MODEL), lambda i, te, nv: (te[i], 0, 0)),
        ],
        out_specs=pl.BlockSpec((tm, D_MODEL), lambda i, te, nv: (i, 0)),
        scratch_shapes=[pltpu.VMEM((D_MODEL, 2 * D_FF), BF16), pltpu.VMEM((D_FF, D_MODEL), BF16)],
    )
    return pl.pallas_call(
        _moe_kernel,
        grid_spec=grid_spec,
        out_shape=jax.ShapeDtypeStruct((n_tiles * tm, D_MODEL), F32),
        compiler_params=_cparams(1),
        name="moe_experts",
    )(tile_expert, tile_valid, xs, w1, b1, w2, b2)


def _combine_kernel(pos_ref, posn_ref, y_hbm, x1_ref, tg_ref, g_ref, b_ref, o_ref, ybuf, sem):
    i = pl.program_id(0)
    n = pl.num_programs(0)
    tm = CMB_TM
    slot = i % 2

    def issue(idx_ref, s):
        def body(r, c):
            for kk in range(TOP_K):
                pltpu.make_async_copy(y_hbm.at[pl.ds(idx_ref[r * TOP_K + kk], 1)],
                                      ybuf.at[s, kk, pl.ds(r, 1)], sem.at[s]).start()
            return c
        lax.fori_loop(0, tm, body, 0, unroll=4)

    @pl.when(i == 0)
    def _():
        issue(pos_ref, 0)

    @pl.when(i + 1 < n)
    def _():
        issue(posn_ref, 1 - slot)

    pltpu.make_async_copy(ybuf.at[slot], ybuf.at[slot], sem.at[slot]).wait()
    tg = tg_ref[...]
    moe = tg[:, 0:1] * ybuf[slot, 0]
    for kk in range(1, TOP_K):
        moe = moe + tg[:, kk:kk + 1] * ybuf[slot, kk]
    o_ref[...] = _layer_norm_rows(ALPHA * x1_ref[...] + moe, g_ref[...], b_ref[...])


def _combine(pos, y_sorted, x1, tg, g, b):
    N = x1.shape[0]
    tm = CMB_TM
    n = N // tm
    row = lambda w: pl.BlockSpec((tm, w), lambda i: (i, 0))
    const = lambda a: pl.BlockSpec(a.shape, lambda i: (0,) * a.ndim)
    return pl.pallas_call(
        _combine_kernel,
        grid=(n,),
        in_specs=[pl.BlockSpec((tm * TOP_K,), lambda i: (i,), memory_space=pltpu.SMEM),
                  pl.BlockSpec((tm * TOP_K,), lambda i: (jnp.minimum(i + 1, n - 1),), memory_space=pltpu.SMEM),
                  pl.BlockSpec(memory_space=pl.ANY), row(D_MODEL), row(TOP_K), const(g), const(b)],
        out_specs=row(D_MODEL),
        out_shape=jax.ShapeDtypeStruct((N, D_MODEL), F32),
        scratch_shapes=[pltpu.VMEM((2, TOP_K, tm, D_MODEL), F32), pltpu.SemaphoreType.DMA((2,))],
        compiler_params=_cparams(1),
        name="moe_combine_ln",
    )(pos, pos, y_sorted, x1, tg, g, b)


def _routing_tables(top_i, rank, counts_f, n_tiles):
    tm = MOE_TM
    counts = counts_f.reshape(-1).astype(I32)
    padded = ((counts + tm - 1) // tm) * tm
    ends = jnp.cumsum(padded)
    offsets = ends - padded
    onehot = top_i[:, :, None] == jnp.arange(N_EXPERTS, dtype=I32)[None, None, :]
    pos = jnp.sum(jnp.where(onehot, offsets[None, None, :], 0), axis=-1) + rank
    tile_start = jnp.arange(n_tiles, dtype=I32) * tm
    tile_expert = jnp.sum((ends[None, :] <= tile_start[:, None]).astype(I32), axis=1)
    tile_expert = jnp.minimum(tile_expert, N_EXPERTS - 1)
    n_used = ends[-1] // tm
    last_expert = tile_expert[jnp.maximum(n_used - 1, 0)]
    tile_expert = jnp.where(tile_start < ends[-1], tile_expert, last_expert)
    valid_end = (offsets + counts)[tile_expert]
    tile_valid = jnp.clip(valid_end - tile_start, 0, tm)
    return pos.reshape(-1).astype(I32), tile_expert.astype(I32), tile_valid.astype(I32)


def _rope_tables(T):
    inv = ROPE_THETA ** (-jnp.arange(0, HEAD_DIM, 2, dtype=F32) / HEAD_DIM)
    ang = jnp.arange(T, dtype=F32)[:, None] * inv[None, :]
    cos, sin = jnp.cos(ang), jnp.sin(ang)
    cos_t = jnp.tile(jnp.concatenate([cos, cos], axis=-1), (1, GROUP_HEADS))
    sin_t = jnp.tile(jnp.concatenate([-sin, sin], axis=-1), (1, GROUP_HEADS))
    return cos_t, sin_t


def _pad_w_in(w_in):
    base = 12 * GROUP_WIDTH + IDX_HEADS * IDX_DIM
    w = jnp.zeros((D_MODEL, IN_PAD), F32)
    w = w.at[:, :base + IDX_DIM].set(w_in[:, :base + IDX_DIM])
    w = w.at[:, base + LANES:base + LANES + IDX_HEADS].set(w_in[:, base + IDX_DIM:])
    return w.astype(BF16)


def _layer(x2, B, T, cos_t, sin_t, tabs, w_in, ret_gn_g, ret_gn_b, conv_w, conv_b, rg_wx, rg_bx, rg_wa,
           rg_ba, rg_lambda, w_out, ln1_g, ln1_b, router_w, router_b, exp_w1, exp_b1, exp_w2, exp_b2,
           ln2_g, ln2_b, layer):
    N = B * T
    r2 = lambda a: a.reshape(1, -1)
    (aq, ak, av, rq, rk, rv, rg, cx, cg, dq, dk, dv, dqi, dki, dwt) = _proj(x2, _pad_w_in(w_in), cos_t, sin_t, T)
    seq = lambda a: a.reshape(B, T, a.shape[-1])
    o_a = _moba(seq(aq), seq(ak), seq(av))
    o_r = _retention(seq(rq), seq(rk), seq(rv), seq(rg), r2(ret_gn_g), r2(ret_gn_b), tabs)
    o_c = _rglru(seq(cx), seq(cg), conv_w, r2(conv_b), _block_diag(rg_wx).astype(BF16), r2(rg_bx),
                 _block_diag(rg_wa).astype(BF16), r2(rg_ba), r2(rg_lambda))
    o_d = _dsa(seq(dq), seq(dk), seq(dv), seq(dqi), seq(dki), dwt)
    flat = lambda a: a.reshape(N, GROUP_WIDTH)
    x1, top_i, top_g, rank, counts = _outproj(flat(o_a), flat(o_r), flat(o_c), flat(o_d), x2,
                                              w_out.astype(BF16), r2(ln1_g), r2(ln1_b),
                                              router_w.T.astype(BF16), router_b.reshape(-1, 1))
    top_i, top_g, rank = top_i.T, top_g.T, rank.T
    n_tiles = (N * TOP_K) // MOE_TM + N_EXPERTS
    pos, tile_expert, tile_valid = _routing_tables(top_i, rank, counts, n_tiles)
    xs = _dispatch(pos, x1, tile_valid)
    y_sorted = _moe_experts(xs, tile_expert, tile_valid, exp_w1, exp_b1.reshape(N_EXPERTS, 1, -1), exp_w2,
                            exp_b2.reshape(N_EXPERTS, 1, -1), layer)
    return _combine(pos, y_sorted, x1, top_g, r2(ln2_g), r2(ln2_b))


def kernel(x, w_in, ret_gn_g, ret_gn_b, conv_w, conv_b, rg_wx, rg_bx, rg_wa, rg_ba, rg_lambda, w_out,
           ln1_g, ln1_b, router_w, router_b, exp_w1, exp_b1, exp_w2, exp_b2, ln2_g, ln2_b):
    B, T, D = x.shape
    cos_t, sin_t = _rope_tables(T)
    tabs = _ret_tables()
    x2 = x.reshape(B * T, D)
    for l in range(w_in.shape[0]):
        x2 = _layer(x2, B, T, cos_t, sin_t, tabs, w_in[l], ret_gn_g[l], ret_gn_b[l], conv_w[l], conv_b[l],
                    rg_wx[l], rg_bx[l], rg_wa[l], rg_ba[l], rg_lambda[l], w_out[l], ln1_g[l], ln1_b[l],
                    router_w[l], router_b[l], exp_w1, exp_b1[l], exp_w2, exp_b2[l], ln2_g[l], ln2_b[l], l)
    return x2.reshape(B, T, D)
```

```python
import functools

import numpy as np
import jax
import jax.numpy as jnp
from jax import lax
from jax.experimental import pallas as pl
from jax.experimental.pallas import tpu as pltpu

F32 = jnp.float32
BF16 = jnp.bfloat16
I32 = jnp.int32

D_MODEL = 1024
DEPTH = 2
HEAD_DIM = 64
GROUP_WIDTH = 256
GROUP_HEADS = 4
ROPE_THETA = 10000.0
MOBA_Q = 256
MOBA_BLOCK = 256
MOBA_TOPK = 3
MOBA_MAX_BLOCKS = 16
RET_CHUNK = 128
RG_CONV = 4
RG_C = 8.0
IDX_HEADS = 8
IDX_DIM = 64
IDX_SCALE = (IDX_HEADS ** -0.5) * (IDX_DIM ** -0.5)
DSA_TOPK = 256
DSA_Q = 256
DSA_KEY_BLOCK = 512
DSA_ATT_BLOCK = 512
N_EXPERTS = 32
TOP_K = 4
D_FF = 1024
SWIGLU_LIMIT = 7.0
SWIGLU_ALPHA = 1.702
ALPHA = (2 * DEPTH) ** 0.25
LN_EPS = 1e-5
IN_WIDTH = 12 * GROUP_WIDTH + IDX_HEADS * IDX_DIM + IDX_DIM + IDX_HEADS
IN_PAD = 15 * GROUP_WIDTH

LANES = 128
SUBLANES = 8
HEAD_SHIFT = HEAD_DIM.bit_length() - 1
KEY_MAGNITUDE_BITS = 0x7FFFFFFF
BISECT_MAX_STEPS = 36
NEG = -3.0e38
M_FLOOR = -1.5e38
BIG = 3.0e38
VMEM_LIMIT = 56 * 1024 * 1024
FOLD_ROWS = 32

PROJ_TM = 256
OUT_TM = 512
RG_TC = 256
MOE_TM = 512
DSP_TM = 512
CMB_TM = 256


def _cparams(ndims):
    return pltpu.CompilerParams(dimension_semantics=("arbitrary",) * ndims,
                                vmem_limit_bytes=VMEM_LIMIT)


def _dot(a, b, precision=None):
    return jnp.dot(a, b, preferred_element_type=F32, precision=precision)


def _dot_nt(a, b, precision=None):
    return lax.dot_general(a, b, (((1,), (1,)), ((), ())), preferred_element_type=F32,
                           precision=precision)


def _dot_tn(a, b):
    return lax.dot_general(a, b, (((0,), (0,)), ((), ())), preferred_element_type=F32)


def _split3(x):
    hi = x.astype(BF16)
    r1 = x - hi.astype(F32)
    mid = r1.astype(BF16)
    lo = (r1 - mid.astype(F32)).astype(BF16)
    return lo, mid, hi


def _head_stack(q):
    head = lax.shift_right_logical(lax.broadcasted_iota(I32, q.shape, 1), HEAD_SHIFT)
    qf = q.astype(F32)
    return jnp.concatenate([jnp.where(head == h, qf, 0.0) for h in range(GROUP_HEADS)],
                           axis=0).astype(q.dtype)


def _head_unstack(s, rows):
    head = lax.shift_right_logical(lax.broadcasted_iota(I32, (rows, GROUP_WIDTH), 1), HEAD_SHIFT)
    out = jnp.zeros((rows, GROUP_WIDTH), F32)
    for h in range(GROUP_HEADS):
        out = out + jnp.where(head == h, s[h * rows:(h + 1) * rows], 0.0)
    return out


def _fold_rows(x, op):
    return op(x.reshape(x.shape[0] // FOLD_ROWS, FOLD_ROWS, x.shape[1]), axis=0)


def _layer_norm_rows(y, g, b):
    mu = jnp.mean(y, axis=-1, keepdims=True)
    yc = y - mu
    var = jnp.mean(yc * yc, axis=-1, keepdims=True)
    return yc * lax.rsqrt(var + LN_EPS) * g + b


def _proj_kernel(x_ref, w_ref, cos_ref, sin_ref,
                 aq, ak, av, rq, rk, rv, rg, cx, cg, dq, dk, dv, dqi, dki, dwt):
    xb = x_ref[...].astype(BF16)
    cos = cos_ref[...]
    sin = sin_ref[...]
    first_half = (lax.broadcasted_iota(I32, cos.shape, 1) & (HEAD_DIM - 1)) < (HEAD_DIM // 2)

    def seg(i):
        return _dot(xb, w_ref[:, i * GROUP_WIDTH:(i + 1) * GROUP_WIDTH])

    def rope(p):
        rot = jnp.where(first_half, pltpu.roll(p, GROUP_WIDTH - HEAD_DIM // 2, 1),
                        pltpu.roll(p, HEAD_DIM // 2, 1))
        return p * cos + rot * sin

    aq[...] = rope(seg(0)).astype(BF16)
    ak[...] = rope(seg(1)).astype(BF16)
    av[...] = seg(2).astype(BF16)
    rq[...] = rope(seg(3)).astype(BF16)
    rk[...] = (rope(seg(4)) * (HEAD_DIM ** -0.5)).astype(BF16)
    rv[...] = seg(5).astype(BF16)
    rg[...] = seg(6)
    cx[...] = seg(7)
    cg[...] = seg(8)
    dq[...] = rope(seg(9)).astype(BF16)
    dk[...] = rope(seg(10)).astype(BF16)
    dv[...] = seg(11).astype(BF16)
    dqi[:, 0:GROUP_WIDTH] = rope(seg(12)).astype(BF16)
    dqi[:, GROUP_WIDTH:2 * GROUP_WIDTH] = rope(seg(13)).astype(BF16)
    last = seg(14)
    dki[...] = rope(last)[:, 0:IDX_DIM].astype(BF16)
    dwt[...] = last[:, LANES:2 * LANES].T[0:IDX_HEADS, :]


def _proj(x2, w_pad, cos_t, sin_t, T):
    N = x2.shape[0]
    tm = PROJ_TM
    tpb = T // tm
    row = lambda w: pl.BlockSpec((tm, w), lambda i: (i, 0))
    tab = pl.BlockSpec((tm, GROUP_WIDTH), lambda i: (i % tpb, 0))
    widths = [256] * 12 + [512, IDX_DIM]
    dtypes = [BF16, BF16, BF16, BF16, BF16, BF16, F32, F32, F32, BF16, BF16, BF16, BF16, BF16]
    return pl.pallas_call(
        _proj_kernel,
        grid=(N // tm,),
        in_specs=[row(D_MODEL), pl.BlockSpec((D_MODEL, IN_PAD), lambda i: (0, 0)), tab, tab],
        out_specs=[row(w) for w in widths] + [pl.BlockSpec((IDX_HEADS, tm), lambda i: (0, i))],
        out_shape=[jax.ShapeDtypeStruct((N, w), d) for w, d in zip(widths, dtypes)]
        + [jax.ShapeDtypeStruct((IDX_HEADS, N), F32)],
        compiler_params=_cparams(1),
        name="proj_rope",
    )(x2, w_pad, cos_t, sin_t)


def _moba_kernel(q_ref, k_ref, v_ref, o_ref, kmean_ref, sel_ref, qk_ref, m_ref, l_ref, acc_ref, *, n_blocks):
    j = pl.program_id(1)
    R = MOBA_Q
    SR = GROUP_HEADS * R
    KB = MOBA_BLOCK

    @pl.when(j == 0)
    def _():
        kmean_ref[...] = jnp.zeros_like(kmean_ref)
        for n in range(n_blocks):
            kb = k_ref[0, n * KB:(n + 1) * KB, :].astype(F32)
            kmean_ref[n:n + 1, :] = jnp.mean(kb, axis=0, keepdims=True)

    own = j // (KB // R)
    q_raw = _head_stack(q_ref[0])
    q_stack = (q_raw.astype(F32) * (HEAD_DIM ** -0.5)).astype(BF16)

    gate = sum(_dot_nt(part, q_raw) for part in _split3(kmean_ref[...]))
    blk = lax.broadcasted_iota(I32, gate.shape, 0)
    past = blk < own
    g = jnp.where(past, gate, -jnp.inf)
    sel = jnp.zeros(gate.shape, F32)
    for _ in range(MOBA_TOPK):
        mx = jnp.max(g, axis=0, keepdims=True)
        first = jnp.min(jnp.where(g == mx, blk, MOBA_MAX_BLOCKS), axis=0, keepdims=True)
        pick = blk == first
        sel = jnp.where(pick & past, 1.0, sel)
        g = jnp.where(pick, -jnp.inf, g)
    sel_ref[...] = sel

    m_ref[...] = jnp.full(m_ref.shape, M_FLOOR, F32)
    l_ref[...] = jnp.zeros(l_ref.shape, F32)
    acc_ref[...] = jnp.zeros(acc_ref.shape, F32)

    def softmax_pv(s_raw, vb, bias):
        s = s_raw + bias
        m_old = m_ref[...]
        m_new = jnp.maximum(m_old, jnp.max(s, axis=0, keepdims=True))
        alpha = jnp.exp(m_old - m_new)
        p = jnp.exp(s - m_new)
        l_ref[...] = alpha * l_ref[...] + jnp.sum(p, axis=0, keepdims=True)
        acc_ref[...] = alpha * acc_ref[...] + _dot_tn(vb, p.astype(BF16))
        m_ref[...] = m_new

    def block_bias(n):
        return jnp.broadcast_to(jnp.where(sel_ref[pl.ds(n, 1), :] > 0.5, 0.0, NEG), (KB, SR))

    def qk(first_block, rows):
        st = pl.multiple_of(first_block * KB, KB)
        return _dot_nt(k_ref[0, pl.ds(st, rows), :], q_stack)

    def attend_span(first_block, bias, s_raw=None):
        st = pl.multiple_of(first_block * KB, KB)
        rows = bias.shape[0]
        s_raw = qk(first_block, rows) if s_raw is None else s_raw
        softmax_pv(s_raw, v_ref[0, pl.ds(st, rows), :], bias)

    keypos = own * KB + lax.broadcasted_iota(I32, (KB, SR), 0)
    qpos = j * R + (lax.broadcasted_iota(I32, (KB, SR), 1) & (R - 1))
    causal = jnp.where(keypos <= qpos, 0.0, NEG)
    odd = (own & 1) == 1

    @pl.when(odd)
    def _():
        attend_span(own - 1, jnp.concatenate([block_bias(own - 1), causal], axis=0))

    @pl.when(jnp.logical_not(odd))
    def _():
        attend_span(own, causal)

    n_pairs = own // 2

    @pl.when(n_pairs > 0)
    def _():
        qk_ref[...] = qk(0, 2 * KB)

    def body(pair, c):
        s_raw = qk_ref[...]
        qk_ref[...] = qk(2 * jnp.minimum(pair + 1, n_pairs - 1), 2 * KB)
        attend_span(2 * pair, jnp.concatenate([block_bias(2 * pair), block_bias(2 * pair + 1)], axis=0), s_raw)
        return c

    lax.fori_loop(0, n_pairs, body, 0)
    o_ref[0] = _head_unstack((acc_ref[...] / l_ref[...]).T, R).astype(o_ref.dtype)


def _moba(q, k, v):
    B, T, _ = q.shape
    n_blocks = T // MOBA_BLOCK
    assert T % MOBA_BLOCK == 0 and n_blocks <= MOBA_MAX_BLOCKS
    SR = GROUP_HEADS * MOBA_Q
    return pl.pallas_call(
        functools.partial(_moba_kernel, n_blocks=n_blocks),
        grid=(B, T // MOBA_Q),
        in_specs=[pl.BlockSpec((1, MOBA_Q, GROUP_WIDTH), lambda b, j: (b, j, 0)),
                  pl.BlockSpec((1, T, GROUP_WIDTH), lambda b, j: (b, 0, 0)),
                  pl.BlockSpec((1, T, GROUP_WIDTH), lambda b, j: (b, 0, 0))],
        out_specs=pl.BlockSpec((1, MOBA_Q, GROUP_WIDTH), lambda b, j: (b, j, 0)),
        out_shape=jax.ShapeDtypeStruct((B, T, GROUP_WIDTH), BF16),
        scratch_shapes=[pltpu.VMEM((MOBA_MAX_BLOCKS, GROUP_WIDTH), F32),
                        pltpu.VMEM((MOBA_MAX_BLOCKS, SR), F32),
                        pltpu.VMEM((2 * MOBA_BLOCK, SR), F32),
                        pltpu.VMEM((1, SR), F32), pltpu.VMEM((1, SR), F32),
                        pltpu.VMEM((GROUP_WIDTH, SR), F32)],
        compiler_params=_cparams(2),
        name="moba_attention",
    )(q, k, v)


def _ret_kernel(q_ref, k_ref, v_ref, g_ref, dmask_ref, xi_ref, zeta_ref, gdec_ref, bd_ref, avg_ref,
                gng_ref, gnb_ref, o_ref, r_ref):
    j = pl.program_id(0)

    @pl.when(j == 0)
    def _():
        r_ref[...] = jnp.zeros_like(r_ref)

    C = RET_CHUNK
    for b in range(q_ref.shape[0]):
        q = q_ref[b]
        k = k_ref[b]
        v = v_ref[b]
        q_stack = _head_stack(q)
        inner = _dot_nt(q_stack, k) * dmask_ref[...]
        o = _head_unstack(_dot(inner.astype(BF16), v), C)
        R = r_ref[b]
        o = o + _dot(q, R.astype(BF16)) * xi_ref[...]
        kz = (k.astype(F32) * zeta_ref[...]).astype(BF16)
        r_ref[b] = gdec_ref[...] * R + bd_ref[...] * _dot_tn(kz, v)

        avg = avg_ref[...]
        mu = sum(_dot(part, avg) for part in _split3(o))
        oc = o - mu
        var = sum(_dot(part, avg) for part in _split3(oc * oc))
        y = oc * lax.rsqrt(var + LN_EPS) * gng_ref[...] + gnb_ref[...]
        gte = g_ref[b]
        o_ref[b] = (y * (gte * jax.nn.sigmoid(gte))).astype(o_ref.dtype)


def _ret_tables():
    H, C, d = GROUP_HEADS, RET_CHUNK, HEAD_DIM
    log_g = np.log(1.0 - 2.0 ** (-5.0 - np.arange(H, dtype=np.float64)))
    n = np.arange(C, dtype=np.float64)
    diff = n[:, None] - n[None, :]
    dmask = np.where(diff >= 0, np.exp(log_g[:, None, None] * np.maximum(diff, 0.0)), 0.0)
    xi = np.exp(log_g[:, None] * (n + 1.0))
    zeta = np.exp(log_g[:, None] * (C - 1.0 - n))
    g_chunk = np.exp(log_g * C)
    head = np.arange(GROUP_WIDTH) // d
    bd = (head[:, None] == head[None, :]).astype(np.float64)
    to32 = lambda a: jnp.asarray(a, dtype=F32)
    return dict(dmask=to32(dmask.reshape(H * C, C)), xi=to32(xi.T[:, head]), zeta=to32(zeta.T[:, head]),
                gdec=to32(bd * g_chunk[head][:, None]), bd=to32(bd),
                avg=jnp.asarray(bd / d, dtype=BF16))


def _retention(rq, rk, rv, rg, gn_g, gn_b, tabs):
    B, T, _ = rq.shape
    C = RET_CHUNK
    blk = pl.BlockSpec((B, C, GROUP_WIDTH), lambda j: (0, j, 0))
    const = lambda a: pl.BlockSpec(a.shape, lambda j: (0,) * a.ndim)
    consts = [tabs["dmask"], tabs["xi"], tabs["zeta"], tabs["gdec"], tabs["bd"], tabs["avg"], gn_g, gn_b]
    return pl.pallas_call(
        _ret_kernel,
        grid=(T // C,),
        in_specs=[blk, blk, blk, blk] + [const(a) for a in consts],
        out_specs=blk,
        out_shape=jax.ShapeDtypeStruct((B, T, GROUP_WIDTH), BF16),
        scratch_shapes=[pltpu.VMEM((B, GROUP_WIDTH, GROUP_WIDTH), F32)],
        compiler_params=_cparams(1),
        name="retention",
    )(rq, rk, rv, rg, *consts)


def _rglru_kernel(x_ref, g_ref, cw_ref, cb_ref, wx_ref, bx_ref, wa_ref, ba_ref, lam_ref, o_ref,
                  xbuf, h_ref):
    j = pl.program_id(1)
    tc = RG_TC

    @pl.when(j == 0)
    def _():
        xbuf[0:SUBLANES, :] = jnp.zeros((SUBLANES, GROUP_WIDTH), F32)
        h_ref[...] = jnp.zeros_like(h_ref)

    xbuf[SUBLANES:SUBLANES + tc, :] = x_ref[0]
    xc = cb_ref[...] + cw_ref[RG_CONV - 1:RG_CONV, :] * xbuf[SUBLANES:SUBLANES + tc, :]
    for i in range(RG_CONV - 1):
        off = SUBLANES - (RG_CONV - 1) + i
        xc = xc + cw_ref[i:i + 1, :] * xbuf[off:off + tc, :]
    xbuf[0:SUBLANES, :] = xbuf[tc:tc + SUBLANES, :]

    xcb = xc.astype(BF16)
    gate_x = jax.nn.sigmoid(_dot(xcb, wx_ref[...]) + bx_ref[...])
    gate_a = jax.nn.sigmoid(_dot(xcb, wa_ref[...]) + ba_ref[...])
    lam = lam_ref[...]
    softplus_neg = jnp.maximum(-lam, 0.0) + jnp.log1p(jnp.exp(-jnp.abs(lam)))
    log_a = -RG_C * gate_a * softplus_neg
    a = jnp.exp(log_a)
    th = jnp.tanh(log_a)
    b = jnp.sqrt(-2.0 * th / (1.0 - th)) * (gate_x * xc)

    row = lax.broadcasted_iota(I32, (tc, GROUP_WIDTH), 0)
    d = 1
    while d < tc:
        keep = row >= d
        a_sh = jnp.where(keep, pltpu.roll(a, d, 0), 1.0)
        b_sh = jnp.where(keep, pltpu.roll(b, d, 0), 0.0)
        b = a * b_sh + b
        a = a * a_sh
        d *= 2
    h = b + a * h_ref[...]
    h_ref[...] = h[tc - 1:tc, :]

    xg = g_ref[0]
    gelu = 0.5 * xg * (1.0 + jnp.tanh(np.sqrt(2.0 / np.pi) * (xg + 0.044715 * xg * xg * xg)))
    o_ref[0] = (h * gelu).astype(o_ref.dtype)


def _block_diag(w):
    n, c, _ = w.shape
    eye = jnp.eye(n, dtype=w.dtype)
    return (eye[:, None, :, None] * w[:, :, None, :]).reshape(n * c, n * c)


def _rglru(cx, cg, conv_w, conv_b, wx, bx, wa, ba, lam):
    B, T, _ = cx.shape
    tc = RG_TC
    blk = pl.BlockSpec((1, tc, GROUP_WIDTH), lambda b, j: (b, j, 0))
    const = lambda a: pl.BlockSpec(a.shape, lambda b, j: (0,) * a.ndim)
    consts = [conv_w, conv_b, wx, bx, wa, ba, lam]
    return pl.pallas_call(
        _rglru_kernel,
        grid=(B, T // tc),
        in_specs=[blk, blk] + [const(a) for a in consts],
        out_specs=blk,
        out_shape=jax.ShapeDtypeStruct((B, T, GROUP_WIDTH), BF16),
        scratch_shapes=[pltpu.VMEM((tc + SUBLANES, GROUP_WIDTH), F32), pltpu.VMEM((1, GROUP_WIDTH), F32)],
        compiler_params=_cparams(2),
        name="rg_lru",
    )(cx, cg, *consts)


def _dsa_kernel(q_ref, k_ref, v_ref, qi_ref, ki_ref, wt_ref, o_ref,
                sc_ref, jcut_ref, qk_ref, m_ref, l_ref, acc_ref, *, n_sel, n_keys):
    j = pl.program_id(1)
    R = DSA_Q
    KB = DSA_KEY_BLOCK
    SR = GROUP_HEADS * R
    nkb = (j * R + R + KB - 1) // KB
    nsel = float(n_sel)

    rowk = lax.broadcasted_iota(I32, (KB, R), 0)
    qpos = j * R + lax.broadcasted_iota(I32, (KB, R), 1)

    qi = qi_ref[0]
    qi_stack = jnp.concatenate([qi[:, h * IDX_DIM:(h + 1) * IDX_DIM] for h in range(IDX_HEADS)], axis=0)
    wt = wt_ref[...]

    def score_body(kb, c):
        st = pl.multiple_of(kb * KB, KB)
        rel = _dot_nt(ki_ref[0, pl.ds(st, KB), :], qi_stack)
        sc = wt[0:1, :] * jnp.maximum(rel[:, 0:R], 0.0)
        for h in range(1, IDX_HEADS):
            sc = sc + wt[h:h + 1, :] * jnp.maximum(rel[:, h * R:(h + 1) * R], 0.0)
        sc_ref[pl.ds(st, KB), :] = jnp.where(st + rowk <= qpos, sc * IDX_SCALE, NEG)
        return c

    lax.fori_loop(0, nkb, score_body, 0)

    def blocks(fn, init):
        def body(kb, c):
            st = pl.multiple_of(kb * KB, KB)
            return fn(st, sc_ref[pl.ds(st, KB), :], c)
        return lax.fori_loop(0, nkb, body, init)

    def survey(st, s, c):
        mn, mx, gt, ge = c
        return (jnp.minimum(mn, _fold_rows(jnp.where(s > 0.5 * NEG, s, BIG), jnp.min)),
                jnp.maximum(mx, _fold_rows(s, jnp.max)),
                gt + _fold_rows(jnp.where(s > 0.0, 1.0, 0.0), jnp.sum),
                ge + _fold_rows(jnp.where(s >= 0.0, 1.0, 0.0), jnp.sum))

    zeros_fold = jnp.zeros((FOLD_ROWS, R), F32)
    mn8, mx8, gt8, ge8 = blocks(survey, (jnp.full((FOLD_ROWS, R), BIG, F32), jnp.full((FOLD_ROWS, R), NEG, F32),
                                         zeros_fold, zeros_fold))
    mn = jnp.min(mn8, axis=0, keepdims=True)
    mx = jnp.max(mx8, axis=0, keepdims=True)
    c_gt0 = jnp.sum(gt8, axis=0, keepdims=True)
    c_ge0 = jnp.sum(ge8, axis=0, keepdims=True)

    def count_ge(th):
        acc = blocks(lambda st, s, c: c + _fold_rows(jnp.where(s >= th, 1.0, 0.0), jnp.sum),
                     jnp.zeros((FOLD_ROWS, R), F32))
        return jnp.sum(acc, axis=0, keepdims=True)

    n_adm = (j * R + 1 + lax.broadcasted_iota(I32, (1, R), 1)).astype(F32)
    need = n_adm > nsel
    lo0 = jnp.where(need, mn, 0.5 * NEG)
    above = need & (c_gt0 >= nsel)
    below = need & (c_ge0 < nsel)
    at_zero = need & jnp.logical_not(above | below)
    cgt0 = jnp.where(below, c_ge0, jnp.where(at_zero, c_gt0, 0.0))
    act0 = jnp.where(above | below, 1.0, 0.0)
    tie0 = jnp.where(at_zero, 1.0, 0.0)

    def to_key(f):
        b = lax.bitcast_convert_type(f, I32)
        return b ^ (lax.shift_right_arithmetic(b, 31) & KEY_MAGNITUDE_BITS)

    def from_key(kk):
        return lax.bitcast_convert_type(kk ^ (lax.shift_right_arithmetic(kk, 31) & KEY_MAGNITUDE_BITS), F32)

    def bis_step(klo, khi, cgt, act, tie):
        on = act > 0.0
        kmid = lax.shift_right_arithmetic(klo, 1) + lax.shift_right_arithmetic(khi, 1) + (klo & khi & 1)
        stuck = kmid == klo
        cnt = count_ge(from_key(kmid))
        go = on & jnp.logical_not(stuck)
        up = go & (cnt >= nsel)
        dn = go & (cnt < nsel)
        return (jnp.where(up, kmid, klo), jnp.where(dn, kmid, khi), jnp.where(dn, cnt, cgt),
                jnp.where(go & (cnt != nsel), 1.0, 0.0), jnp.where(on & stuck, 1.0, tie))

    def bis_cond(c):
        return (c[1] > 0.0) & (c[0] < BISECT_MAX_STEPS)

    def bis_body(c):
        flag = jnp.max(c[5])
        st = bis_step(*bis_step(*c[2:]))
        return (c[0] + 2, flag) + st

    klo0 = jnp.where(above, 1, jnp.where(at_zero, 0, to_key(lo0)))
    khi0 = jnp.where(below, 0, to_key(mx) + 1)
    res = lax.while_loop(bis_cond, bis_body, (jnp.int32(0), jnp.max(act0), klo0, khi0, cgt0, act0, tie0))
    lo, cgt, tie = from_key(res[2]), res[4], res[6]

    jcut_ref[...] = jnp.full((1, R), float(n_keys), F32)

    @pl.when(jnp.max(tie) > 0.0)
    def _():
        want = nsel - cgt
        tied = tie > 0.0

        def jb(it, c):
            a, b = c
            mid = jnp.floor((a + b) * 0.5)
            hit8 = blocks(lambda st, s, cc: cc + _fold_rows(
                jnp.where((s == lo) & ((st + rowk).astype(F32) <= mid), 1.0, 0.0), jnp.sum),
                jnp.zeros((FOLD_ROWS, R), F32))
            ok = jnp.sum(hit8, axis=0, keepdims=True) >= want
            return jnp.where(ok, a, mid), jnp.where(ok, mid, b)

        _, b = lax.fori_loop(0, int(np.ceil(np.log2(n_keys))) + 1, jb,
                             (jnp.full((1, R), -1.0, F32), jnp.full((1, R), float(n_keys - 1), F32)))
        jcut_ref[...] = jnp.where(tied, b, float(n_keys))

    q_stack = _head_stack((q_ref[0].astype(F32) * (HEAD_DIM ** -0.5)).astype(BF16))
    m_ref[...] = jnp.full(m_ref.shape, M_FLOOR, F32)
    l_ref[...] = jnp.zeros(l_ref.shape, F32)
    acc_ref[...] = jnp.zeros(acc_ref.shape, F32)
    jcut = jcut_ref[...]
    AB = DSA_ATT_BLOCK
    nab = (j * R + R + AB - 1) // AB
    rowf = lax.broadcasted_iota(I32, (AB, R), 0).astype(F32)

    def qk(kb):
        st = pl.multiple_of(kb * AB, AB)
        return _dot_nt(k_ref[0, pl.ds(st, AB), :], q_stack)

    qk_ref[...] = qk(0)

    def att_body(kb, c):
        st = pl.multiple_of(kb * AB, AB)
        s_raw = qk_ref[...]
        qk_ref[...] = qk(jnp.minimum(kb + 1, nab - 1))
        sc = sc_ref[pl.ds(st, AB), :]
        keep = (sc > lo) | ((sc == lo) & (rowf <= jcut - st.astype(F32)))
        bias = jnp.where(keep, 0.0, NEG)
        s = s_raw + jnp.concatenate([bias] * GROUP_HEADS, axis=1)
        m_old = m_ref[...]
        m_new = jnp.maximum(m_old, jnp.max(s, axis=0, keepdims=True))
        alpha = jnp.exp(m_old - m_new)
        p = jnp.exp(s - m_new)
        l_ref[...] = alpha * l_ref[...] + jnp.sum(p, axis=0, keepdims=True)
        acc_ref[...] = alpha * acc_ref[...] + _dot_tn(v_ref[0, pl.ds(st, AB), :], p.astype(BF16))
        m_ref[...] = m_new
        return c

    lax.fori_loop(0, nab, att_body, 0)
    o_ref[0] = _head_unstack((acc_ref[...] / l_ref[...]).T, R).astype(o_ref.dtype)


def _dsa(q, k, v, qi, ki, wt):
    B, T, _ = q.shape
    KB = DSA_KEY_BLOCK
    assert T % KB == 0
    n_sel = min(DSA_TOPK, T // 4)
    R = DSA_Q
    assert T % R == 0
    SR = GROUP_HEADS * R
    nq = T // R
    qblk = lambda wd: pl.BlockSpec((1, R, wd), lambda b, j: (b, j, 0))
    full = lambda wd: pl.BlockSpec((1, T, wd), lambda b, j: (b, 0, 0))
    return pl.pallas_call(
        functools.partial(_dsa_kernel, n_sel=n_sel, n_keys=T),
        grid=(B, nq),
        in_specs=[qblk(GROUP_WIDTH), full(GROUP_WIDTH), full(GROUP_WIDTH),
                  qblk(IDX_HEADS * IDX_DIM), full(IDX_DIM),
                  pl.BlockSpec((IDX_HEADS, R), lambda b, j: (0, b * nq + j))],
        out_specs=qblk(GROUP_WIDTH),
        out_shape=jax.ShapeDtypeStruct((B, T, GROUP_WIDTH), BF16),
        scratch_shapes=[pltpu.VMEM((T, R), F32), pltpu.VMEM((1, R), F32), pltpu.VMEM((DSA_ATT_BLOCK, SR), F32),
                        pltpu.VMEM((1, SR), F32), pltpu.VMEM((1, SR), F32),
                        pltpu.VMEM((GROUP_WIDTH, SR), F32)],
        compiler_params=_cparams(2),
        name="dsa_attention",
    )(q, k, v, qi, ki, wt)


def _outproj_kernel(oa, orr, oc, od, x_ref, w_ref, g_ref, b_ref, rwt_ref, rb_ref, utri_ref,
                    x1_ref, ti_ref, tg_ref, rk_ref, cnt_ref, run_ref):
    GW = GROUP_WIDTH

    @pl.when(pl.program_id(0) == 0)
    def _():
        run_ref[...] = jnp.zeros_like(run_ref)

    acc = _dot(oa[...], w_ref[0:GW, :])
    acc = acc + _dot(orr[...], w_ref[GW:2 * GW, :])
    acc = acc + _dot(oc[...], w_ref[2 * GW:3 * GW, :])
    acc = acc + _dot(od[...], w_ref[3 * GW:4 * GW, :])
    x1 = _layer_norm_rows(ALPHA * x_ref[...] + acc, g_ref[...], b_ref[...])
    x1_ref[...] = x1

    logits = _dot_nt(rwt_ref[...], x1.astype(BF16)) + rb_ref[...]
    row = lax.broadcasted_iota(I32, logits.shape, 0)
    krow = lax.broadcasted_iota(I32, ti_ref.shape, 0)
    g = logits
    ti = jnp.zeros(ti_ref.shape, I32)
    tv = jnp.zeros(tg_ref.shape, F32)
    picks = []
    for kk in range(TOP_K):
        mx = jnp.max(g, axis=0, keepdims=True)
        first = jnp.min(jnp.where(g == mx, row, N_EXPERTS), axis=0, keepdims=True)
        ti = jnp.where(krow == kk, first, ti)
        tv = jnp.where(krow == kk, mx, tv)
        picks.append(row == first)
        g = jnp.where(picks[-1], -jnp.inf, g)
    e = jnp.exp(tv - jnp.max(tv, axis=0, keepdims=True))
    ti_ref[...] = ti
    tg_ref[...] = e / jnp.sum(e, axis=0, keepdims=True)

    sel = jnp.where(picks[0] | picks[1] | picks[2] | picks[3], 1.0, 0.0)
    before = run_ref[...] + _dot(sel.astype(BF16), utri_ref[...])
    rk = jnp.zeros(rk_ref.shape, F32)
    for kk in range(TOP_K):
        rk = jnp.where(krow == kk, jnp.sum(jnp.where(picks[kk], before, 0.0), axis=0, keepdims=True), rk)
    rk_ref[...] = rk.astype(I32)
    run_ref[...] = run_ref[...] + jnp.sum(sel, axis=1, keepdims=True)
    cnt_ref[...] = run_ref[...]


def _outproj(oa, orr, oc, od, x2, w_out, g, b, rwt, rb):
    N = x2.shape[0]
    tm = OUT_TM
    row = lambda w: pl.BlockSpec((tm, w), lambda i: (i, 0))
    picks = pl.BlockSpec((TOP_K, tm), lambda i: (0, i))
    const = lambda a: pl.BlockSpec(a.shape, lambda i: (0,) * a.ndim)
    utri = jnp.asarray(np.triu(np.ones((tm, tm), np.float32), 1), dtype=BF16)
    return pl.pallas_call(
        _outproj_kernel,
        grid=(N // tm,),
        in_specs=[row(GROUP_WIDTH)] * 4 + [row(D_MODEL), const(w_out), const(g), const(b), const(rwt), const(rb),
                                           const(utri)],
        out_specs=[row(D_MODEL), picks, picks, picks, pl.BlockSpec((N_EXPERTS, 1), lambda i: (0, 0))],
        out_shape=[jax.ShapeDtypeStruct((N, D_MODEL), F32), jax.ShapeDtypeStruct((TOP_K, N), I32),
                   jax.ShapeDtypeStruct((TOP_K, N), F32), jax.ShapeDtypeStruct((TOP_K, N), I32),
                   jax.ShapeDtypeStruct((N_EXPERTS, 1), F32)],
        scratch_shapes=[pltpu.VMEM((N_EXPERTS, 1), F32)],
        compiler_params=_cparams(1),
        name="outproj_ln_router",
    )(oa, orr, oc, od, x2, w_out, g, b, rwt, rb, utri)


def _dispatch_kernel(tv_ref, pos_ref, x_ref, xs_hbm, xbuf, sem, zsem, *, n_tiles):
    i = pl.program_id(0)
    n = pl.num_programs(0)
    tm = DSP_TM
    par = i % 2

    @pl.when(i == 0)
    def _():
        xbuf[1] = jnp.zeros((tm, D_MODEL), F32)

        def fill(t, c):
            @pl.when(tv_ref[t] < MOE_TM)
            def _():
                pltpu.make_async_copy(xbuf.at[1], xs_hbm.at[pl.ds(pl.multiple_of(t * MOE_TM, MOE_TM), MOE_TM)],
                                      zsem).start()
            return c

        def drain(t, c):
            @pl.when(tv_ref[t] < MOE_TM)
            def _():
                pltpu.make_async_copy(xbuf.at[1], xs_hbm.at[pl.ds(0, MOE_TM)], zsem).wait()
            return c

        lax.fori_loop(0, n_tiles, fill, 0)
        lax.fori_loop(0, n_tiles, drain, 0)

    def wait_step(p):
        for _ in range(TOP_K):
            pltpu.make_async_copy(xbuf.at[p], xbuf.at[p], sem.at[p]).wait()

    @pl.when(i >= 2)
    def _():
        wait_step(par)

    xbuf[par] = x_ref[...]

    def body(r, c):
        for kk in range(TOP_K):
            pltpu.make_async_copy(xbuf.at[par, pl.ds(r, 1)], xs_hbm.at[pl.ds(pos_ref[r * TOP_K + kk], 1)],
                                  sem.at[par]).start(priority=kk % 2)
        return c

    lax.fori_loop(0, tm, body, 0, unroll=4)

    @pl.when(i == n - 1)
    def _():
        wait_step(1 - par)
        wait_step(par)


def _dispatch(pos, x1, tile_valid):
    N = x1.shape[0]
    tm = DSP_TM
    n_tiles = tile_valid.shape[0]
    assert N // tm >= 2 and tm == MOE_TM
    grid_spec = pltpu.PrefetchScalarGridSpec(
        num_scalar_prefetch=1,
        grid=(N // tm,),
        in_specs=[pl.BlockSpec((tm * TOP_K,), lambda i, tv: (i,), memory_space=pltpu.SMEM),
                  pl.BlockSpec((tm, D_MODEL), lambda i, tv: (i, 0))],
        out_specs=pl.BlockSpec(memory_space=pl.ANY),
        scratch_shapes=[pltpu.VMEM((2, tm, D_MODEL), F32), pltpu.SemaphoreType.DMA((2,)),
                        pltpu.SemaphoreType.DMA(())],
    )
    return pl.pallas_call(
        functools.partial(_dispatch_kernel, n_tiles=n_tiles),
        grid_spec=grid_spec,
        out_shape=jax.ShapeDtypeStruct((n_tiles * MOE_TM, D_MODEL), F32),
        compiler_params=_cparams(1),
        name="moe_dispatch",
    )(tile_valid, pos, x1)


def _moe_kernel(te_ref, nv_ref, x_ref, w1_ref, b1_ref, w2_ref, b2_ref, y_ref, w1b, w2b):
    i = pl.program_id(0)
    tm = MOE_TM
    n_valid = nv_ref[i]

    @pl.when((i == 0) | (te_ref[i] != te_ref[jnp.maximum(i - 1, 0)]))
    def _():
        step = LANES
        for c in range(D_MODEL // step):
            w1b[c * step:(c + 1) * step, :] = w1_ref[0, 0, c * step:(c + 1) * step, :].astype(BF16)
        for c in range(D_FF // step):
            w2b[c * step:(c + 1) * step, :] = w2_ref[0, 0, c * step:(c + 1) * step, :].astype(BF16)

    @pl.when(n_valid > 0)
    def _():
        h = _dot(x_ref[...].astype(BF16), w1b[...]) + b1_ref[0]
        glu_in = jnp.minimum(h[:, :D_FF], SWIGLU_LIMIT)
        up = jnp.clip(h[:, D_FF:], -SWIGLU_LIMIT, SWIGLU_LIMIT)
        glu = glu_in * jax.nn.sigmoid(SWIGLU_ALPHA * glu_in)
        y_ref[...] = _dot(((up + 1.0) * glu).astype(BF16), w2b[...]) + b2_ref[0]

    @pl.when(n_valid == 0)
    def _():
        y_ref[...] = jnp.zeros_like(y_ref)


def _moe_experts(xs, tile_expert, tile_valid, w1, b1, w2, b2, layer):
    tm = MOE_TM
    n_tiles = tile_expert.shape[0]
    grid_spec = pltpu.PrefetchScalarGridSpec(
        num_scalar_prefetch=2,
        grid=(n_tiles,),
        in_specs=[
            pl.BlockSpec((tm, D_MODEL), lambda i, te, nv: (i, 0)),
            pl.BlockSpec((1, 1, D_MODEL, 2 * D_FF), lambda i, te, nv: (layer, te[i], 0, 0)),
            pl.BlockSpec((1, 1, 2 * D_FF), lambda i, te, nv: (te[i], 0, 0)),
            pl.BlockSpec((1, 1, D_FF, D_MODEL), lambda i, te, nv: (layer, te[i], 0, 0)),
            pl.BlockSpec((1, 1, D_MODEL), lambda i, te, nv: (te[i], 0, 0)),
        ],
        out_specs=pl.BlockSpec((tm, D_MODEL), lambda i, te, nv: (i, 0)),
        scratch_shapes=[pltpu.VMEM((D_MODEL, 2 * D_FF), BF16), pltpu.VMEM((D_FF, D_MODEL), BF16)],
    )
    return pl.pallas_call(
        _moe_kernel,
        grid_spec=grid_spec,
        out_shape=jax.ShapeDtypeStruct((n_tiles * tm, D_MODEL), F32),
        compiler_params=_cparams(1),
        name="moe_experts",
    )(tile_expert, tile_valid, xs, w1, b1, w2, b2)


def _combine_kernel(pos_ref, posn_ref, y_hbm, x1_ref, tg_ref, g_ref, b_ref, o_ref, ybuf, sem):
    i = pl.program_id(0)
    n = pl.num_programs(0)
    tm = CMB_TM
    slot = i % 2

    def issue(idx_ref, s):
        def body(r, c):
            for kk in range(TOP_K):
                pltpu.make_async_copy(y_hbm.at[pl.ds(idx_ref[r * TOP_K + kk], 1)],
                                      ybuf.at[s, kk, pl.ds(r, 1)], sem.at[s]).start(priority=kk % 2)
            return c
        lax.fori_loop(0, tm, body, 0, unroll=4)

    @pl.when(i == 0)
    def _():
        issue(pos_ref, 0)

    @pl.when(i + 1 < n)
    def _():
        issue(posn_ref, 1 - slot)

    pltpu.make_async_copy(ybuf.at[slot], ybuf.at[slot], sem.at[slot]).wait()
    tg = tg_ref[...]
    moe = tg[:, 0:1] * ybuf[slot, 0]
    for kk in range(1, TOP_K):
        moe = moe + tg[:, kk:kk + 1] * ybuf[slot, kk]
    o_ref[...] = _layer_norm_rows(ALPHA * x1_ref[...] + moe, g_ref[...], b_ref[...])


def _combine(pos, y_sorted, x1, tg, g, b):
    N = x1.shape[0]
    tm = CMB_TM
    n = N // tm
    row = lambda w: pl.BlockSpec((tm, w), lambda i: (i, 0))
    const = lambda a: pl.BlockSpec(a.shape, lambda i: (0,) * a.ndim)
    return pl.pallas_call(
        _combine_kernel,
        grid=(n,),
        in_specs=[pl.BlockSpec((tm * TOP_K,), lambda i: (i,), memory_space=pltpu.SMEM),
                  pl.BlockSpec((tm * TOP_K,), lambda i: (jnp.minimum(i + 1, n - 1),), memory_space=pltpu.SMEM),
                  pl.BlockSpec(memory_space=pl.ANY), row(D_MODEL), row(TOP_K), const(g), const(b)],
        out_specs=row(D_MODEL),
        out_shape=jax.ShapeDtypeStruct((N, D_MODEL), F32),
        scratch_shapes=[pltpu.VMEM((2, TOP_K, tm, D_MODEL), F32), pltpu.SemaphoreType.DMA((2,))],
        compiler_params=_cparams(1),
        name="moe_combine_ln",
    )(pos, pos, y_sorted, x1, tg, g, b)


def _routing_tables(top_i, rank, counts_f, n_tiles):
    tm = MOE_TM
    counts = counts_f.reshape(-1).astype(I32)
    padded = ((counts + tm - 1) // tm) * tm
    ends = jnp.cumsum(padded)
    offsets = ends - padded
    onehot = top_i[:, :, None] == jnp.arange(N_EXPERTS, dtype=I32)[None, None, :]
    pos = jnp.sum(jnp.where(onehot, offsets[None, None, :], 0), axis=-1) + rank
    tile_start = jnp.arange(n_tiles, dtype=I32) * tm
    tile_expert = jnp.sum((ends[None, :] <= tile_start[:, None]).astype(I32), axis=1)
    tile_expert = jnp.minimum(tile_expert, N_EXPERTS - 1)
    n_used = ends[-1] // tm
    last_expert = tile_expert[jnp.maximum(n_used - 1, 0)]
    tile_expert = jnp.where(tile_start < ends[-1], tile_expert, last_expert)
    valid_end = (offsets + counts)[tile_expert]
    tile_valid = jnp.clip(valid_end - tile_start, 0, tm)
    return pos.reshape(-1).astype(I32), tile_expert.astype(I32), tile_valid.astype(I32)


def _rope_tables(T):
    inv = ROPE_THETA ** (-jnp.arange(0, HEAD_DIM, 2, dtype=F32) / HEAD_DIM)
    ang = jnp.arange(T, dtype=F32)[:, None] * inv[None, :]
    cos, sin = jnp.cos(ang), jnp.sin(ang)
    cos_t = jnp.tile(jnp.concatenate([cos, cos], axis=-1), (1, GROUP_HEADS))
    sin_t = jnp.tile(jnp.concatenate([-sin, sin], axis=-1), (1, GROUP_HEADS))
    return cos_t, sin_t


def _pad_w_in(w_in):
    base = 12 * GROUP_WIDTH + IDX_HEADS * IDX_DIM
    w = jnp.zeros((D_MODEL, IN_PAD), F32)
    w = w.at[:, :base + IDX_DIM].set(w_in[:, :base + IDX_DIM])
    w = w.at[:, base + LANES:base + LANES + IDX_HEADS].set(w_in[:, base + IDX_DIM:])
    return w.astype(BF16)


def _layer(x2, B, T, cos_t, sin_t, tabs, w_in, ret_gn_g, ret_gn_b, conv_w, conv_b, rg_wx, rg_bx, rg_wa,
           rg_ba, rg_lambda, w_out, ln1_g, ln1_b, router_w, router_b, exp_w1, exp_b1, exp_w2, exp_b2,
           ln2_g, ln2_b, layer):
    N = B * T
    r2 = lambda a: a.reshape(1, -1)
    (aq, ak, av, rq, rk, rv, rg, cx, cg, dq, dk, dv, dqi, dki, dwt) = _proj(x2, _pad_w_in(w_in), cos_t, sin_t, T)
    seq = lambda a: a.reshape(B, T, a.shape[-1])
    o_a = _moba(seq(aq), seq(ak), seq(av))
    o_r = _retention(seq(rq), seq(rk), seq(rv), seq(rg), r2(ret_gn_g), r2(ret_gn_b), tabs)
    o_c = _rglru(seq(cx), seq(cg), conv_w, r2(conv_b), _block_diag(rg_wx).astype(BF16), r2(rg_bx),
                 _block_diag(rg_wa).astype(BF16), r2(rg_ba), r2(rg_lambda))
    o_d = _dsa(seq(dq), seq(dk), seq(dv), seq(dqi), seq(dki), dwt)
    flat = lambda a: a.reshape(N, GROUP_WIDTH)
    x1, top_i, top_g, rank, counts = _outproj(flat(o_a), flat(o_r), flat(o_c), flat(o_d), x2,
                                              w_out.astype(BF16), r2(ln1_g), r2(ln1_b),
                                              router_w.T.astype(BF16), router_b.reshape(-1, 1))
    top_i, top_g, rank = top_i.T, top_g.T, rank.T
    n_tiles = (N * TOP_K) // MOE_TM + N_EXPERTS
    pos, tile_expert, tile_valid = _routing_tables(top_i, rank, counts, n_tiles)
    xs = _dispatch(pos, x1, tile_valid)
    y_sorted = _moe_experts(xs, tile_expert, tile_valid, exp_w1, exp_b1.reshape(N_EXPERTS, 1, -1), exp_w2,
                            exp_b2.reshape(N_EXPERTS, 1, -1), layer)
    return _combine(pos, y_sorted, x1, top_g, r2(ln2_g), r2(ln2_b))


def kernel(x, w_in, ret_gn_g, ret_gn_b, conv_w, conv_b, rg_wx, rg_bx, rg_wa, rg_ba, rg_lambda, w_out,
           ln1_g, ln1_b, router_w, router_b, exp_w1, exp_b1, exp_w2, exp_b2, ln2_g, ln2_b):
    B, T, D = x.shape
    cos_t, sin_t = _rope_tables(T)
    tabs = _ret_tables()
    x2 = x.reshape(B * T, D)
    for l in range(w_in.shape[0]):
        x2 = _layer(x2, B, T, cos_t, sin_t, tabs, w_in[l], ret_gn_g[l], ret_gn_b[l], conv_w[l], conv_b[l],
                    rg_wx[l], rg_bx[l], rg_wa[l], rg_ba[l], rg_lambda[l], w_out[l], ln1_g[l], ln1_b[l],
                    router_w[l], router_b[l], exp_w1, exp_b1[l], exp_w2, exp_b2[l], ln2_g[l], ln2_b[l], l)
    return x2.reshape(B, T, D)
```
